```python
import math
import jax, jax.numpy as jnp
from jax import lax
import numpy as np

D_MODEL = 1024
BATCH = 4
SEQ = 4096
DEPTH = 2

N_Q_HEADS = 16
N_KV_HEADS = 2
HEAD_DIM = 64
Q_PER_KV = N_Q_HEADS // N_KV_HEADS
WINDOW = 128
ATTN_BLOCK = 128
ROPE_THETA = 500000.0
ROT_DIM = HEAD_DIM // 4
SGU_WIDTH = 1024
SGU_GROUPS = 8
SGU_GROUP_DIM = SGU_WIDTH // SGU_GROUPS
SGU_CHUNK = 128
FFN_DIM = 2816
CONV_WIDTH = 3
NORM_EPS = 1e-6

Q_END = N_Q_HEADS * HEAD_DIM
K_END = Q_END + N_KV_HEADS * HEAD_DIM
V_END = K_END + N_KV_HEADS * HEAD_DIM
Z_END = V_END + 2 * SGU_WIDTH
IN_COLS = Z_END + 2 * D_MODEL

kernel_name = "hybrid_gmlp_swa_sink_convffn_adaln"


def rms_norm(x, w):
    xf = x.astype(jnp.float32)
    y = xf * lax.rsqrt(jnp.mean(xf * xf, axis=-1, keepdims=True) + NORM_EPS)
    return (y * w.astype(jnp.float32)).astype(x.dtype)


def layer_norm(x, w, b):
    xf = x.astype(jnp.float32)
    mu = jnp.mean(xf, axis=-1, keepdims=True)
    var = jnp.mean(jnp.square(xf - mu), axis=-1, keepdims=True)
    y = (xf - mu) * lax.rsqrt(var + NORM_EPS)
    return (y * w.astype(jnp.float32) + b.astype(jnp.float32)).astype(x.dtype)


def rope_tables(positions, dtype):
    inv_freq = ROPE_THETA ** (-jnp.arange(0, ROT_DIM, 2, dtype=jnp.float32) / ROT_DIM)
    ang = positions.astype(jnp.float32)[..., None] * inv_freq
    return jnp.cos(ang)[:, :, None, :].astype(dtype), jnp.sin(ang)[:, :, None, :].astype(dtype)


def apply_partial_rope(x, cos, sin):
    half = ROT_DIM // 2
    x1, x2, xp = x[..., :half], x[..., half:ROT_DIM], x[..., ROT_DIM:]
    return jnp.concatenate([x1 * cos - x2 * sin, x2 * cos + x1 * sin, xp], axis=-1)


def sliding_window_attention(q, k, v, sinks):
    B, S = q.shape[0], q.shape[1]
    nb = S // ATTN_BLOCK
    qb = q.reshape(B, nb, ATTN_BLOCK, N_KV_HEADS, Q_PER_KV, HEAD_DIM)
    kb = k.reshape(B, nb, ATTN_BLOCK, N_KV_HEADS, HEAD_DIM)
    vb = v.reshape(B, nb, ATTN_BLOCK, N_KV_HEADS, HEAD_DIM)
    pad = ((0, 0), (1, 0), (0, 0), (0, 0), (0, 0))
    k_band = jnp.concatenate([jnp.pad(kb[:, :-1], pad), kb], axis=2)
    v_band = jnp.concatenate([jnp.pad(vb[:, :-1], pad), vb], axis=2)
    scores = jnp.einsum('bnqhgd,bnkhd->bnhgqk', qb, k_band).astype(jnp.float32) * (HEAD_DIM ** -0.5)
    i = jnp.arange(ATTN_BLOCK)[:, None]
    j = jnp.arange(2 * ATTN_BLOCK)[None, :]
    band = (j > i + ATTN_BLOCK - WINDOW) & (j <= i + ATTN_BLOCK)
    exists = (jnp.arange(nb)[:, None, None] > 0) | (j >= ATTN_BLOCK)[None]
    mask = (band[None] & exists)[None, :, None, None]
    scores = jnp.where(mask, scores, -jnp.inf)
    sink = sinks.astype(jnp.float32).reshape(N_KV_HEADS, Q_PER_KV)[None, None, :, :, None, None]
    m = jnp.maximum(jnp.max(scores, axis=-1, keepdims=True), sink)
    p = jnp.exp(scores - m)
    denom = jnp.sum(p, axis=-1, keepdims=True) + jnp.exp(sink - m)
    out = jnp.einsum('bnhgqk,bnkhd->bnqhgd', (p / denom).astype(v.dtype), v_band)
    return out.reshape(B, S, N_Q_HEADS * HEAD_DIM)


def spatial_gating(z, ln_w, ln_b, w_s, b_s):
    B, S = z.shape[0], z.shape[1]
    u, v = jnp.split(z, 2, axis=-1)
    v = layer_norm(v, ln_w, ln_b)
    vb = v.reshape(B, S // SGU_CHUNK, SGU_CHUNK, SGU_GROUPS, SGU_GROUP_DIM)
    causal = jnp.tril(jnp.ones((SGU_CHUNK, SGU_CHUNK), dtype=bool))
    w = jnp.where(causal[None], w_s, jnp.zeros_like(w_s))
    f = jnp.einsum('gts,bnsgc->bntgc', w, vb) + b_s.T[None, None, :, :, None]
    return u * f.reshape(B, S, SGU_WIDTH)


def conv_ffn(h, w_gate, w_up, conv_w, conv_b, w_down):
    a = h @ w_gate
    S = a.shape[1]
    a_pad = jnp.pad(a, ((0, 0), (CONV_WIDTH - 1, 0), (0, 0)))
    a = conv_b + sum(conv_w[k] * a_pad[:, k:k + S] for k in range(CONV_WIDTH))
    return (jax.nn.silu(a) * (h @ w_up)) @ w_down


def setup_inputs(seed: int = 0) -> dict:
    key = jax.random.key(seed)
    ks = jax.random.split(key, 24)
    L, D = DEPTH, D_MODEL
    nrm = lambda k, shape, s: jax.random.normal(k, shape, jnp.float32) * s
    offsets = jax.random.randint(ks[2], (BATCH, 1), 0, 2048, dtype=jnp.int32)
    return {
        "x": nrm(ks[0], (BATCH, SEQ, D), 1.0),
        "c": nrm(ks[1], (BATCH, D), 1.0),
        "positions": offsets + jnp.arange(SEQ, dtype=jnp.int32)[None, :],
        "ada_w": nrm(ks[3], (L, D, 6 * D), D ** -0.5),
        "ada_b": nrm(ks[4], (L, 6 * D), 0.02),
        "norm1_w": 1.0 + nrm(ks[5], (L, D), 0.05),
        "w_in": nrm(ks[6], (L, D, IN_COLS), D ** -0.5),
        "attn_sinks": nrm(ks[7], (L, N_Q_HEADS), 0.5),
        "sgu_ln_w": 1.0 + nrm(ks[8], (L, SGU_WIDTH), 0.05),
        "sgu_ln_b": nrm(ks[9], (L, SGU_WIDTH), 0.02),
        "sgu_w": nrm(ks[10], (L, SGU_GROUPS, SGU_CHUNK, SGU_CHUNK), SGU_CHUNK ** -0.5),
        "sgu_b": 1.0 + nrm(ks[11], (L, SGU_GROUPS, SGU_CHUNK), 0.05),
        "proj_a": nrm(ks[12], (L, SGU_WIDTH, D), SGU_WIDTH ** -0.5),
        "proj_b": nrm(ks[13], (L, N_Q_HEADS * HEAD_DIM, D), (N_Q_HEADS * HEAD_DIM) ** -0.5),
        "w_out": nrm(ks[14], (L, D, D), D ** -0.5),
        "norm2_w": 1.0 + nrm(ks[15], (L, D), 0.05),
        "ffn_w_gate": nrm(ks[16], (L, D, FFN_DIM), D ** -0.5),
        "ffn_w_up": nrm(ks[17], (L, D, FFN_DIM), D ** -0.5),
        "ffn_conv_w": nrm(ks[18], (L, CONV_WIDTH, FFN_DIM), CONV_WIDTH ** -0.5),
        "ffn_conv_b": nrm(ks[19], (L, FFN_DIM), 0.01),
        "ffn_w_down": nrm(ks[20], (L, FFN_DIM, D), FFN_DIM ** -0.5),
        "final_norm_w": 1.0 + nrm(ks[21], (D,), 0.05),
    }


def reference(x, c, positions, ada_w, ada_b, norm1_w, w_in, attn_sinks, sgu_ln_w, sgu_ln_b, sgu_w, sgu_b,
              proj_a, proj_b, w_out, norm2_w, ffn_w_gate, ffn_w_up, ffn_conv_w, ffn_conv_b, ffn_w_down,
              final_norm_w):
    B, S = x.shape[0], x.shape[1]
    cos, sin = rope_tables(positions, x.dtype)
    c_act = jax.nn.silu(c)
    for l in range(DEPTH):
        mod = (c_act @ ada_w[l] + ada_b[l])[:, None, :]
        sh1, sc1, g1, sh2, sc2, g2 = jnp.split(mod, 6, axis=-1)
        h = rms_norm(x, norm1_w[l]) * (1 + sc1) + sh1
        proj = h @ w_in[l]
        q, k, v, z, gates = jnp.split(proj, [Q_END, K_END, V_END, Z_END], axis=-1)
        q = apply_partial_rope(q.reshape(B, S, N_Q_HEADS, HEAD_DIM), cos, sin)
        k = apply_partial_rope(k.reshape(B, S, N_KV_HEADS, HEAD_DIM), cos, sin)
        v = v.reshape(B, S, N_KV_HEADS, HEAD_DIM)
        y_attn = sliding_window_attention(q, k, v, attn_sinks[l])
        y_sgu = spatial_gating(jax.nn.gelu(z, approximate=False), sgu_ln_w[l], sgu_ln_b[l], sgu_w[l], sgu_b[l])
        gate_a, gate_b = jnp.split(jax.nn.sigmoid(gates), 2, axis=-1)
        merged = gate_a * (y_sgu @ proj_a[l]) + gate_b * (y_attn @ proj_b[l])
        x = x + g1 * (merged @ w_out[l])
        h2 = rms_norm(x, norm2_w[l]) * (1 + sc2) + sh2
        x = x + g2 * conv_ffn(h2, ffn_w_gate[l], ffn_w_up[l], ffn_conv_w[l], ffn_conv_b[l], ffn_w_down[l])
    return rms_norm(x, final_norm_w)
```

```python
import functools

import jax
import jax.numpy as jnp
from jax import lax
from jax.experimental import pallas as pl
from jax.experimental.pallas import tpu as pltpu

D_MODEL = 1024
N_Q_HEADS = 16
N_KV_HEADS = 2
HEAD_DIM = 64
Q_PER_KV = N_Q_HEADS // N_KV_HEADS
ATTN_BLOCK = 128
ROPE_THETA = 500000.0
ROT_DIM = HEAD_DIM // 4
SGU_WIDTH = 1024
SGU_GROUPS = 8
SGU_CHUNK = 128
FFN_DIM = 2816
CONV_WIDTH = 3
NORM_EPS = 1e-6

Q_END = N_Q_HEADS * HEAD_DIM
K_END = Q_END + N_KV_HEADS * HEAD_DIM
V_END = K_END + N_KV_HEADS * HEAD_DIM
Z_END = V_END + 2 * SGU_WIDTH
IN_COLS = Z_END + 2 * D_MODEL

LANES = 128
SUBLANES = 8
MXU_COLS = 256
TOKEN_TILE = 512
VMEM_LIMIT_BYTES = 56 * 1024 * 1024
MASK_VALUE = -1e30

F32 = jnp.float32
BF16 = jnp.bfloat16


def _rms_norm(x, w):
    ms = jnp.mean(x * x, axis=-1, keepdims=True)
    return x * lax.rsqrt(ms + NORM_EPS) * w


def _dot(a, b):
    return jnp.dot(a, b, preferred_element_type=F32)


ADA_COL_TILE = 1536


def _ada_kernel(c_ref, w_ref, b_ref, o_ref):
    c = c_ref[...]
    c_act = c * jax.nn.sigmoid(c)
    o_ref[0] = jnp.dot(c_act, w_ref[0], preferred_element_type=F32,
                       precision=lax.Precision.HIGHEST) + b_ref[0]


def _ada_modulation(c, ada_w, ada_b):
    depth, d, n = ada_w.shape
    rows = c.shape[0]
    return pl.pallas_call(
        _ada_kernel,
        grid=(depth, n // ADA_COL_TILE),
        in_specs=[
            pl.BlockSpec((rows, d), lambda l, j: (0, 0)),
            pl.BlockSpec((1, d, ADA_COL_TILE), lambda l, j: (l, 0, j)),
            pl.BlockSpec((1, 1, ADA_COL_TILE), lambda l, j: (l, 0, j)),
        ],
        out_specs=pl.BlockSpec((1, rows, ADA_COL_TILE), lambda l, j: (l, 0, j)),
        out_shape=jax.ShapeDtypeStruct((depth, rows, n), F32),
        compiler_params=pltpu.CompilerParams(
            dimension_semantics=("arbitrary", "arbitrary"),
            vmem_limit_bytes=VMEM_LIMIT_BYTES),
        name="ada_modulation",
    )(c, ada_w, ada_b.reshape(depth, 1, n))


ROPE_ROW_TILE = 1024


def _rope_kernel(pos_ref, invf_ref, cos_ref, sin_ref):
    pos = pos_ref[0].astype(F32)
    ang = pos * invf_ref[...]
    lane = lax.broadcasted_iota(jnp.int32, ang.shape, 1) & (HEAD_DIM - 1)
    c = jnp.cos(ang)
    s = jnp.sin(ang)
    cos_ref[0] = jnp.where(lane < ROT_DIM, c, 1.0)
    sin_ref[0] = jnp.where(lane < ROT_DIM // 2, -s, jnp.where(lane < ROT_DIM, s, 0.0))


def _rope_tables(positions):
    b, s = positions.shape
    half = ROT_DIM // 2
    inv_freq = ROPE_THETA ** (-jnp.arange(0, ROT_DIM, 2, dtype=F32) / ROT_DIM)
    lane = jnp.arange(LANES) % HEAD_DIM
    invf_lane = jnp.where(lane < ROT_DIM, inv_freq[lane % half], 0.0).reshape(1, LANES)
    tab = jax.ShapeDtypeStruct((b, s, LANES), F32)
    return pl.pallas_call(
        _rope_kernel,
        grid=(b, s // ROPE_ROW_TILE),
        in_specs=[
            pl.BlockSpec((1, ROPE_ROW_TILE, 1), lambda i, j: (i, j, 0)),
            pl.BlockSpec((1, LANES), lambda i, j: (0, 0)),
        ],
        out_specs=[pl.BlockSpec((1, ROPE_ROW_TILE, LANES), lambda i, j: (i, j, 0))] * 2,
        out_shape=[tab, tab],
        compiler_params=pltpu.CompilerParams(
            dimension_semantics=("arbitrary", "arbitrary"),
            vmem_limit_bytes=VMEM_LIMIT_BYTES),
        name="rope_tables",
    )(positions.reshape(b, s, 1), invf_lane)


def _mix_kernel(x_ref, mod_ref, n1w_ref, cos_ref, sin_ref, win_ref, sink_ref, lnw_ref, lnb_ref,
                sguw_ref, sgub_ref, pa_ref, pb_ref, wo_ref, o_ref,
                q_s, k4_s, v4_s, yattn_s, ysgu_s):
    tm = TOKEN_TILE
    nblk = tm // ATTN_BLOCK
    j = pl.program_id(1)

    @pl.when(j == 0)
    def _():
        k4_s[:, 0:ATTN_BLOCK, :] = jnp.zeros((4, ATTN_BLOCK, LANES), BF16)
        v4_s[:, 0:ATTN_BLOCK, :] = jnp.zeros((4, ATTN_BLOCK, LANES), BF16)

    x = x_ref[0]
    h = _rms_norm(x, n1w_ref[...]) * (1.0 + mod_ref[0, 1:2, :]) + mod_ref[0, 0:1, :]
    hb = h.astype(BF16)

    cos_t = cos_ref[0]
    sin_t = sin_ref[0]
    lane = lax.broadcasted_iota(jnp.int32, (tm, LANES), 1)
    first_half = (lane & (HEAD_DIM - 1)) < (ROT_DIM // 2)
    left_head = lane < HEAD_DIM

    def rope(t):
        partner = jnp.where(first_half,
                            pltpu.roll(t, LANES - ROT_DIM // 2, 1),
                            pltpu.roll(t, ROT_DIM // 2, 1))
        return t * cos_t + partner * sin_t

    q = _dot(hb, win_ref[:, 0:Q_END])
    for c in range(Q_END // LANES):
        qc = rope(q[:, c * LANES:(c + 1) * LANES]) * (HEAD_DIM ** -0.5)
        q_s[:, c * LANES:(c + 1) * LANES] = qc.astype(BF16)

    kv = _dot(hb, win_ref[:, Q_END:V_END])
    k = rope(kv[:, 0:LANES])
    v = kv[:, LANES:2 * LANES]
    for src, dst in ((k, k4_s), (v, v4_s)):
        swapped = pltpu.roll(src, HEAD_DIM, 1)
        zero = jnp.zeros_like(src)
        rows = slice(ATTN_BLOCK, ATTN_BLOCK + tm)
        dst[0, rows, :] = jnp.where(left_head, src, zero).astype(BF16)
        dst[1, rows, :] = jnp.where(left_head, zero, swapped).astype(BF16)
        dst[2, rows, :] = jnp.where(left_head, swapped, zero).astype(BF16)
        dst[3, rows, :] = jnp.where(left_head, zero, src).astype(BF16)

    qi = lax.broadcasted_iota(jnp.int32, (ATTN_BLOCK, 2 * ATTN_BLOCK), 0)
    kj = lax.broadcasted_iota(jnp.int32, (ATTN_BLOCK, 2 * ATTN_BLOCK), 1)
    band = (kj > qi) & (kj <= qi + ATTN_BLOCK)
    own_block = kj >= ATTN_BLOCK
    out_lane_left = lax.broadcasted_iota(jnp.int32, (ATTN_BLOCK, LANES), 1) < HEAD_DIM
    contract_last = (((1,), (1,)), ((), ()))

    def attn_block(b, carry):
        r0 = pl.multiple_of(b * ATTN_BLOCK, ATTN_BLOCK)
        has_prev = jnp.logical_or(j > 0, b > 0)
        ok = band & (own_block | has_prev)
        bias = jnp.where(ok, 0.0, MASK_VALUE)
        bias2 = jnp.concatenate([bias, bias], axis=1)
        for g in range(N_KV_HEADS):
            kbd = jnp.concatenate([k4_s[2 * g, pl.ds(r0, 2 * ATTN_BLOCK), :],
                                   k4_s[2 * g + 1, pl.ds(r0, 2 * ATTN_BLOCK), :]], axis=0)
            vbd = jnp.concatenate([v4_s[2 * g, pl.ds(r0, 2 * ATTN_BLOCK), :],
                                   v4_s[2 * g + 1, pl.ds(r0, 2 * ATTN_BLOCK), :]], axis=0)
            for pr in range(Q_PER_KV // 2):
                col = (g * Q_PER_KV + 2 * pr) * HEAD_DIM
                qp = q_s[pl.ds(r0, ATTN_BLOCK), col:col + LANES]
                s = lax.dot_general(qp, kbd, contract_last, preferred_element_type=F32) + bias2
                probs, inv = [], []
                for hh in range(2):
                    sh = s[:, hh * 2 * ATTN_BLOCK:(hh + 1) * 2 * ATTN_BLOCK]
                    sink = sink_ref[g * Q_PER_KV + 2 * pr + hh]
                    m = jnp.maximum(jnp.max(sh, axis=-1, keepdims=True), sink)
                    p = jnp.exp(sh - m)
                    denom = jnp.sum(p, axis=-1, keepdims=True) + jnp.exp(sink - m)
                    probs.append(p.astype(BF16))
                    inv.append(1.0 / denom)
                o = _dot(jnp.concatenate(probs, axis=1), vbd)
                o = o * jnp.where(out_lane_left, inv[0], inv[1])
                yattn_s[pl.ds(r0, ATTN_BLOCK), col:col + LANES] = o.astype(BF16)
        return carry

    lax.fori_loop(0, nblk, attn_block, 0)

    k4_s[:, 0:ATTN_BLOCK, :] = k4_s[:, tm:tm + ATTN_BLOCK, :]
    v4_s[:, 0:ATTN_BLOCK, :] = v4_s[:, tm:tm + ATTN_BLOCK, :]

    z = _dot(hb, win_ref[:, V_END:Z_END])
    z = 0.5 * z * (1.0 + lax.erf(z * (2.0 ** -0.5)))
    u = z[:, 0:SGU_WIDTH]
    v_sgu = z[:, SGU_WIDTH:]
    mu = jnp.mean(v_sgu, axis=-1, keepdims=True)
    dv = v_sgu - mu
    var = jnp.mean(dv * dv, axis=-1, keepdims=True)
    vn = (dv * lax.rsqrt(var + NORM_EPS) * lnw_ref[...] + lnb_ref[...]).astype(BF16)
    ti = lax.broadcasted_iota(jnp.int32, (SGU_CHUNK, SGU_CHUNK), 0)
    si = lax.broadcasted_iota(jnp.int32, (SGU_CHUNK, SGU_CHUNK), 1)
    causal = si <= ti
    gdim = SGU_WIDTH // SGU_GROUPS
    nchunk = tm // SGU_CHUNK
    for g in range(SGU_GROUPS):
        cols = slice(g * gdim, (g + 1) * gdim)
        wm = jnp.where(causal, sguw_ref[g], jnp.zeros((SGU_CHUNK, SGU_CHUNK), BF16))
        rhs = jnp.concatenate(
            [vn[c * SGU_CHUNK:(c + 1) * SGU_CHUNK, cols] for c in range(nchunk)], axis=1)
        f = _dot(wm, rhs)
        for c in range(nchunk):
            rows = slice(c * SGU_CHUNK, (c + 1) * SGU_CHUNK)
            fc = f[:, c * gdim:(c + 1) * gdim] + sgub_ref[:, cols]
            ysgu_s[rows, cols] = (u[rows, cols] * fc).astype(BF16)

    gates = jax.nn.sigmoid(_dot(hb, win_ref[:, Z_END:IN_COLS]))
    merged = (gates[:, 0:D_MODEL] * _dot(ysgu_s[...], pa_ref[...])
              + gates[:, D_MODEL:] * _dot(yattn_s[...], pb_ref[...]))
    o_ref[0] = x + mod_ref[0, 2:3, :] * _dot(merged.astype(BF16), wo_ref[...])


def _full(shape):
    zeros = (0,) * len(shape)
    return pl.BlockSpec(shape, lambda i, j: zeros)


def _mix_layer(x, mod, n1w, cos_tab, sin_tab, w_in, sinks, ln_w, ln_b, sgu_w, sgu_b_full,
               proj_a, proj_b, w_out):
    b, s, d = x.shape
    tm = TOKEN_TILE
    tile = pl.BlockSpec((1, tm, d), lambda i, j: (i, j, 0))
    tab = pl.BlockSpec((1, tm, LANES), lambda i, j: (i, j, 0))
    return pl.pallas_call(
        _mix_kernel,
        grid=(b, s // tm),
        in_specs=[
            tile,
            pl.BlockSpec((1, 6, d), lambda i, j: (i, 0, 0)),
            _full((1, d)),
            tab, tab,
            _full((d, IN_COLS)),
            pl.BlockSpec(memory_space=pltpu.SMEM),
            _full((1, SGU_WIDTH)), _full((1, SGU_WIDTH)),
            _full((SGU_GROUPS, SGU_CHUNK, SGU_CHUNK)),
            _full((SGU_CHUNK, SGU_WIDTH)),
            _full((SGU_WIDTH, d)), _full((Q_END, d)), _full((d, d)),
        ],
        out_specs=tile,
        out_shape=jax.ShapeDtypeStruct(x.shape, x.dtype),
        scratch_shapes=[
            pltpu.VMEM((tm, Q_END), BF16),
            pltpu.VMEM((4, tm + ATTN_BLOCK, LANES), BF16),
            pltpu.VMEM((4, tm + ATTN_BLOCK, LANES), BF16),
            pltpu.VMEM((tm, Q_END), BF16),
            pltpu.VMEM((tm, SGU_WIDTH), BF16),
        ],
        compiler_params=pltpu.CompilerParams(
            dimension_semantics=("arbitrary", "arbitrary"),
            vmem_limit_bytes=VMEM_LIMIT_BYTES),
        name="token_mixer",
    )(x, mod, n1w, cos_tab, sin_tab, w_in, sinks, ln_w, ln_b, sgu_w, sgu_b_full,
      proj_a, proj_b, w_out)


CONV_PAD_ROWS = SUBLANES


def _ffn_kernel(x_ref, mod_ref, n2w_ref, wg_ref, wu_ref, cw_ref, cb_ref, wd_ref, fnw_ref, o_ref,
                a_s, act_s, *, final_norm):
    tm = TOKEN_TILE
    j = pl.program_id(1)

    @pl.when(j == 0)
    def _():
        a_s[0:CONV_PAD_ROWS, :] = jnp.zeros((CONV_PAD_ROWS, FFN_DIM), F32)

    x = x_ref[0]
    h = _rms_norm(x, n2w_ref[...]) * (1.0 + mod_ref[0, 4:5, :]) + mod_ref[0, 3:4, :]
    hb = h.astype(BF16)

    for c in range(FFN_DIM // MXU_COLS):
        cols = slice(c * MXU_COLS, (c + 1) * MXU_COLS)
        a = _dot(hb, wg_ref[:, cols])
        a_s[CONV_PAD_ROWS:CONV_PAD_ROWS + tm, cols] = a
        conv = cb_ref[:, cols] + cw_ref[2:3, cols] * a
        for tap in range(CONV_WIDTH - 1):
            back = CONV_WIDTH - 1 - tap
            conv = conv + cw_ref[tap:tap + 1, cols] * a_s[CONV_PAD_ROWS - back:CONV_PAD_ROWS - back + tm, cols]
        up = _dot(hb, wu_ref[:, cols])
        act_s[:, cols] = (conv * jax.nn.sigmoid(conv) * up).astype(BF16)

    a_s[0:CONV_PAD_ROWS, :] = a_s[tm:tm + CONV_PAD_ROWS, :]

    y = x + mod_ref[0, 5:6, :] * _dot(act_s[...], wd_ref[...])
    if final_norm:
        y = _rms_norm(y, fnw_ref[...])
    o_ref[0] = y


def _ffn_layer(x, mod, n2w, w_gate, w_up, conv_w, conv_b, w_down, final_w, final_norm):
    b, s, d = x.shape
    tm = TOKEN_TILE
    tile = pl.BlockSpec((1, tm, d), lambda i, j: (i, j, 0))
    return pl.pallas_call(
        functools.partial(_ffn_kernel, final_norm=final_norm),
        grid=(b, s // tm),
        in_specs=[
            tile,
            pl.BlockSpec((1, 6, d), lambda i, j: (i, 0, 0)),
            _full((1, d)),
            _full((d, FFN_DIM)), _full((d, FFN_DIM)),
            _full((CONV_WIDTH, FFN_DIM)), _full((1, FFN_DIM)),
            _full((FFN_DIM, d)),
            _full((1, d)),
        ],
        out_specs=tile,
        out_shape=jax.ShapeDtypeStruct(x.shape, x.dtype),
        scratch_shapes=[
            pltpu.VMEM((CONV_PAD_ROWS + tm, FFN_DIM), F32),
            pltpu.VMEM((tm, FFN_DIM), BF16),
        ],
        compiler_params=pltpu.CompilerParams(
            dimension_semantics=("arbitrary", "arbitrary"),
            vmem_limit_bytes=VMEM_LIMIT_BYTES),
        name="conv_ffn",
    )(x, mod, n2w, w_gate, w_up, conv_w, conv_b, w_down, final_w)


def kernel(x, c, positions, ada_w, ada_b, norm1_w, w_in, attn_sinks, sgu_ln_w, sgu_ln_b, sgu_w, sgu_b,
           proj_a, proj_b, w_out, norm2_w, ffn_w_gate, ffn_w_up, ffn_conv_w, ffn_conv_b, ffn_w_down,
           final_norm_w):
    batch, seq, d = x.shape
    depth = ada_w.shape[0]
    assert seq % TOKEN_TILE == 0 and TOKEN_TILE % ATTN_BLOCK == 0 and d == D_MODEL

    rows = -(-batch // SUBLANES) * SUBLANES
    c_pad = jnp.pad(c, ((0, rows - batch), (0, 0)))
    mod = _ada_modulation(c_pad, ada_w, ada_b)[:, :batch]
    cos_tab, sin_tab = _rope_tables(positions)

    for l in range(depth):
        mod_l = mod[l].reshape(batch, 6, d)
        sgu_b_full = jnp.repeat(sgu_b[l].T, SGU_WIDTH // SGU_GROUPS, axis=1)
        x = _mix_layer(
            x, mod_l, norm1_w[l].reshape(1, d), cos_tab, sin_tab, w_in[l].astype(BF16),
            attn_sinks[l], sgu_ln_w[l].reshape(1, -1), sgu_ln_b[l].reshape(1, -1),
            sgu_w[l].astype(BF16), sgu_b_full,
            proj_a[l].astype(BF16), proj_b[l].astype(BF16), w_out[l].astype(BF16))
        x = _ffn_layer(
            x, mod_l, norm2_w[l].reshape(1, d), ffn_w_gate[l].astype(BF16), ffn_w_up[l].astype(BF16),
            ffn_conv_w[l], ffn_conv_b[l].reshape(1, -1), ffn_w_down[l].astype(BF16),
            final_norm_w.reshape(1, d), final_norm=(l == depth - 1))
    return x
```

```python
import functools

import jax
import jax.numpy as jnp
from jax import lax
from jax.experimental import pallas as pl
from jax.experimental.pallas import tpu as pltpu

D_MODEL = 1024
N_Q_HEADS = 16
N_KV_HEADS = 2
HEAD_DIM = 64
Q_PER_KV = N_Q_HEADS // N_KV_HEADS
ATTN_BLOCK = 128
ROPE_THETA = 500000.0
ROT_DIM = HEAD_DIM // 4
SGU_WIDTH = 1024
SGU_GROUPS = 8
SGU_CHUNK = 128
FFN_DIM = 2816
CONV_WIDTH = 3
NORM_EPS = 1e-6

Q_END = N_Q_HEADS * HEAD_DIM
K_END = Q_END + N_KV_HEADS * HEAD_DIM
V_END = K_END + N_KV_HEADS * HEAD_DIM
Z_END = V_END + 2 * SGU_WIDTH
IN_COLS = Z_END + 2 * D_MODEL

LANES = 128
SUBLANES = 8
MXU_COLS = 256
TOKEN_TILE = 512
VMEM_LIMIT_BYTES = 56 * 1024 * 1024
MASK_VALUE = -1e30
LOG2E = 1.4426950408889634

F32 = jnp.float32
BF16 = jnp.bfloat16


def _rms_norm(x, w):
    ms = jnp.mean(x * x, axis=-1, keepdims=True)
    return x * lax.rsqrt(ms + NORM_EPS) * w


def _dot(a, b):
    return jnp.dot(a, b, preferred_element_type=F32)


ADA_COL_TILE = 1536


def _ada_kernel(c_ref, w_ref, b_ref, o_ref):
    c = c_ref[...]
    c_act = c * jax.nn.sigmoid(c)
    o_ref[0] = jnp.dot(c_act, w_ref[0], preferred_element_type=F32,
                       precision=lax.Precision.HIGHEST) + b_ref[0]


def _ada_modulation(c, ada_w, ada_b):
    depth, d, n = ada_w.shape
    rows = c.shape[0]
    return pl.pallas_call(
        _ada_kernel,
        grid=(depth, n // ADA_COL_TILE),
        in_specs=[
            pl.BlockSpec((rows, d), lambda l, j: (0, 0)),
            pl.BlockSpec((1, d, ADA_COL_TILE), lambda l, j: (l, 0, j)),
            pl.BlockSpec((1, 1, ADA_COL_TILE), lambda l, j: (l, 0, j)),
        ],
        out_specs=pl.BlockSpec((1, rows, ADA_COL_TILE), lambda l, j: (l, 0, j)),
        out_shape=jax.ShapeDtypeStruct((depth, rows, n), F32),
        compiler_params=pltpu.CompilerParams(
            dimension_semantics=("arbitrary", "arbitrary"),
            vmem_limit_bytes=VMEM_LIMIT_BYTES),
        name="ada_modulation",
    )(c, ada_w, ada_b.reshape(depth, 1, n))


ROPE_ROW_TILE = 1024


def _rope_kernel(pos_ref, invf_ref, cos_ref, sin_ref):
    pos = pos_ref[0].astype(F32)
    ang = pos * invf_ref[...]
    lane = lax.broadcasted_iota(jnp.int32, ang.shape, 1) & (HEAD_DIM - 1)
    c = jnp.cos(ang)
    s = jnp.sin(ang)
    cos_ref[0] = jnp.where(lane < ROT_DIM, c, 1.0)
    sin_ref[0] = jnp.where(lane < ROT_DIM // 2, -s, jnp.where(lane < ROT_DIM, s, 0.0))


def _rope_tables(positions):
    b, s = positions.shape
    half = ROT_DIM // 2
    inv_freq = ROPE_THETA ** (-jnp.arange(0, ROT_DIM, 2, dtype=F32) / ROT_DIM)
    lane = jnp.arange(LANES) % HEAD_DIM
    invf_lane = jnp.where(lane < ROT_DIM, inv_freq[lane % half], 0.0).reshape(1, LANES)
    tab = jax.ShapeDtypeStruct((b, s, LANES), F32)
    return pl.pallas_call(
        _rope_kernel,
        grid=(b, s // ROPE_ROW_TILE),
        in_specs=[
            pl.BlockSpec((1, ROPE_ROW_TILE, 1), lambda i, j: (i, j, 0)),
            pl.BlockSpec((1, LANES), lambda i, j: (0, 0)),
        ],
        out_specs=[pl.BlockSpec((1, ROPE_ROW_TILE, LANES), lambda i, j: (i, j, 0))] * 2,
        out_shape=[tab, tab],
        compiler_params=pltpu.CompilerParams(
            dimension_semantics=("arbitrary", "arbitrary"),
            vmem_limit_bytes=VMEM_LIMIT_BYTES),
        name="rope_tables",
    )(positions.reshape(b, s, 1), invf_lane)


def _mix_kernel(x_ref, mod_ref, n1w_ref, cos_ref, sin_ref, win_ref, sink_ref, lnw_ref, lnb_ref,
                sguw_ref, sgub_ref, pa_ref, pb_ref, wo_ref, o_ref,
                q_s, k4_s, v4_s, bias_s, yattn_s, ysgu_s):
    tm = TOKEN_TILE
    nblk = tm // ATTN_BLOCK
    band_keys = 2 * ATTN_BLOCK
    j = pl.program_id(1)

    @pl.when(j == 0)
    def _():
        k4_s[:, 0:ATTN_BLOCK, :] = jnp.zeros((4, ATTN_BLOCK, LANES), BF16)
        v4_s[:, 0:ATTN_BLOCK, :] = jnp.zeros((4, ATTN_BLOCK, LANES), BF16)

    x = x_ref[0]
    h = _rms_norm(x, n1w_ref[...]) * (1.0 + mod_ref[0, 1:2, :]) + mod_ref[0, 0:1, :]
    hb = h.astype(BF16)

    q = _dot(hb, win_ref[:, 0:Q_END])
    kv = _dot(hb, win_ref[:, Q_END:V_END])
    z = _dot(hb, win_ref[:, V_END:Z_END])

    cos_t = cos_ref[0]
    sin_t = sin_ref[0]
    lane = lax.broadcasted_iota(jnp.int32, (tm, LANES), 1)
    first_half = (lane & (HEAD_DIM - 1)) < (ROT_DIM // 2)
    left_head = lane < HEAD_DIM

    def rope(t):
        partner = jnp.where(first_half,
                            pltpu.roll(t, LANES - ROT_DIM // 2, 1),
                            pltpu.roll(t, ROT_DIM // 2, 1))
        return t * cos_t + partner * sin_t

    for c in range(Q_END // LANES):
        qc = rope(q[:, c * LANES:(c + 1) * LANES]) * (HEAD_DIM ** -0.5 * LOG2E)
        q_s[:, c * LANES:(c + 1) * LANES] = qc.astype(BF16)

    k = rope(kv[:, 0:LANES])
    v = kv[:, LANES:2 * LANES]
    for src, dst in ((k, k4_s), (v, v4_s)):
        swapped = pltpu.roll(src, HEAD_DIM, 1)
        zero = jnp.zeros_like(src)
        rows = slice(ATTN_BLOCK, ATTN_BLOCK + tm)
        dst[0, rows, :] = jnp.where(left_head, src, zero).astype(BF16)
        dst[1, rows, :] = jnp.where(left_head, zero, swapped).astype(BF16)
        dst[2, rows, :] = jnp.where(left_head, swapped, zero).astype(BF16)
        dst[3, rows, :] = jnp.where(left_head, zero, src).astype(BF16)

    gates = _dot(hb, win_ref[:, Z_END:IN_COLS])

    z = 0.5 * z * (1.0 + lax.erf(z * (2.0 ** -0.5)))
    u = z[:, 0:SGU_WIDTH]
    v_sgu = z[:, SGU_WIDTH:]
    mu = jnp.mean(v_sgu, axis=-1, keepdims=True)
    dv = v_sgu - mu
    var = jnp.mean(dv * dv, axis=-1, keepdims=True)
    vn = (dv * lax.rsqrt(var + NORM_EPS) * lnw_ref[...] + lnb_ref[...]).astype(BF16)
    ti = lax.broadcasted_iota(jnp.int32, (SGU_CHUNK, SGU_CHUNK), 0)
    si = lax.broadcasted_iota(jnp.int32, (SGU_CHUNK, SGU_CHUNK), 1)
    causal = si <= ti
    gdim = SGU_WIDTH // SGU_GROUPS
    nchunk = tm // SGU_CHUNK
    for g in range(SGU_GROUPS):
        cols = slice(g * gdim, (g + 1) * gdim)
        wm = jnp.where(causal, sguw_ref[g], jnp.zeros((SGU_CHUNK, SGU_CHUNK), BF16))
        rhs = jnp.concatenate(
            [vn[c * SGU_CHUNK:(c + 1) * SGU_CHUNK, cols] for c in range(nchunk)], axis=1)
        f = _dot(wm, rhs)
        for c in range(nchunk):
            rows = slice(c * SGU_CHUNK, (c + 1) * SGU_CHUNK)
            fc = f[:, c * gdim:(c + 1) * gdim] + sgub_ref[:, cols]
            ysgu_s[rows, cols] = (u[rows, cols] * fc).astype(BF16)

    sgu_part = jax.nn.sigmoid(gates[:, 0:D_MODEL]) * _dot(ysgu_s[...], pa_ref[...])
    gate_b = jax.nn.sigmoid(gates[:, D_MODEL:])

    er = lax.broadcasted_iota(jnp.int32, (ATTN_BLOCK, LANES), 0)
    ec = lax.broadcasted_iota(jnp.int32, (ATTN_BLOCK, LANES), 1)
    eye = jnp.where(er == ec, 1.0, 0.0).astype(BF16)
    out_lane_left = ec < HEAD_DIM
    kr = lax.broadcasted_iota(jnp.int32, (band_keys, ATTN_BLOCK), 0)
    qc_i = lax.broadcasted_iota(jnp.int32, (band_keys, ATTN_BLOCK), 1)
    band_t = (kr > qc_i) & (kr <= qc_i + ATTN_BLOCK)
    for idx, ok in ((0, band_t & (kr >= ATTN_BLOCK)), (1, band_t)):
        bias_t = jnp.where(ok, 0.0, MASK_VALUE).astype(BF16)
        bias_s[idx, 0:band_keys, :] = bias_t
        bias_s[idx, band_keys:2 * band_keys, :] = bias_t
    sr = lax.broadcasted_iota(jnp.int32, (2 * band_keys, LANES), 0)
    sc = lax.broadcasted_iota(jnp.int32, (2 * band_keys, LANES), 1)
    ones_sel = jnp.where((sr < band_keys) == (sc < HEAD_DIM), 1.0, 0.0).astype(BF16)
    contract_last = (((1,), (1,)), ((), ()))

    def score_matmuls(b):
        r0 = b * ATTN_BLOCK
        bias_t = bias_s[jnp.where(j > 0, 1, 0)] if b == 0 else bias_s[1]
        scores = []
        value_ops = []
        for g in range(N_KV_HEADS):
            kbd = jnp.concatenate([k4_s[2 * g, pl.ds(r0, band_keys), :],
                                   k4_s[2 * g + 1, pl.ds(r0, band_keys), :]], axis=0)
            vbd = jnp.concatenate([v4_s[2 * g, pl.ds(r0, band_keys), :],
                                   v4_s[2 * g + 1, pl.ds(r0, band_keys), :]], axis=0)
            kbd = jnp.concatenate([kbd, bias_t], axis=1)
            value_ops.append(jnp.concatenate([vbd, ones_sel], axis=1))
            qrows = []
            for pr in range(Q_PER_KV // 2):
                col = (g * Q_PER_KV + 2 * pr) * HEAD_DIM
                qrows.append(jnp.concatenate([q_s[pl.ds(r0, ATTN_BLOCK), col:col + LANES], eye], axis=1))
            scores.append(lax.dot_general(jnp.concatenate(qrows, axis=0), kbd, contract_last,
                                          preferred_element_type=F32))
        return scores, value_ops

    def softmax_value_matmuls(b, scores, value_ops):
        r0 = b * ATTN_BLOCK
        for g in range(N_KV_HEADS):
            probs, sink_terms = [], []
            for pr in range(Q_PER_KV // 2):
                rows = slice(pr * ATTN_BLOCK, (pr + 1) * ATTN_BLOCK)
                pair_probs, pair_sinks = [], []
                for hh in range(2):
                    sh = scores[g][rows, hh * band_keys:(hh + 1) * band_keys]
                    sink = sink_ref[g * Q_PER_KV + 2 * pr + hh] * LOG2E
                    m = jnp.maximum(jnp.max(sh, axis=-1, keepdims=True), sink)
                    pair_probs.append(jnp.exp2(sh - m).astype(BF16))
                    pair_sinks.append(jnp.exp2(sink - m))
                probs.append(jnp.concatenate(pair_probs, axis=1))
                sink_terms.append(jnp.where(out_lane_left, pair_sinks[0], pair_sinks[1]))
            o = _dot(jnp.concatenate(probs, axis=0), value_ops[g])
            for pr in range(Q_PER_KV // 2):
                rows = slice(pr * ATTN_BLOCK, (pr + 1) * ATTN_BLOCK)
                col = (g * Q_PER_KV + 2 * pr) * HEAD_DIM
                denom = o[rows, LANES:] + sink_terms[pr]
                yattn_s[pl.ds(r0, ATTN_BLOCK), col:col + LANES] = (o[rows, 0:LANES] / denom).astype(BF16)

    pending = score_matmuls(0)
    for b in range(nblk):
        upcoming = score_matmuls(b + 1) if b + 1 < nblk else None
        softmax_value_matmuls(b, *pending)
        pending = upcoming

    k4_s[:, 0:ATTN_BLOCK, :] = k4_s[:, tm:tm + ATTN_BLOCK, :]
    v4_s[:, 0:ATTN_BLOCK, :] = v4_s[:, tm:tm + ATTN_BLOCK, :]

    merged = sgu_part + gate_b * _dot(yattn_s[...], pb_ref[...])
    o_ref[0] = x + mod_ref[0, 2:3, :] * _dot(merged.astype(BF16), wo_ref[...])


def _full(shape):
    zeros = (0,) * len(shape)
    return pl.BlockSpec(shape, lambda i, j: zeros)


def _mix_layer(x, mod, n1w, cos_tab, sin_tab, w_in, sinks, ln_w, ln_b, sgu_w, sgu_b_full,
               proj_a, proj_b, w_out):
    b, s, d = x.shape
    tm = TOKEN_TILE
    tile = pl.BlockSpec((1, tm, d), lambda i, j: (i, j, 0))
    tab = pl.BlockSpec((1, tm, LANES), lambda i, j: (i, j, 0))
    return pl.pallas_call(
        _mix_kernel,
        grid=(b, s // tm),
        in_specs=[
            tile,
            pl.BlockSpec((1, 6, d), lambda i, j: (i, 0, 0)),
            _full((1, d)),
            tab, tab,
            _full((d, IN_COLS)),
            pl.BlockSpec(memory_space=pltpu.SMEM),
            _full((1, SGU_WIDTH)), _full((1, SGU_WIDTH)),
            _full((SGU_GROUPS, SGU_CHUNK, SGU_CHUNK)),
            _full((SGU_CHUNK, SGU_WIDTH)),
            _full((SGU_WIDTH, d)), _full((Q_END, d)), _full((d, d)),
        ],
        out_specs=tile,
        out_shape=jax.ShapeDtypeStruct(x.shape, x.dtype),
        scratch_shapes=[
            pltpu.VMEM((tm, Q_END), BF16),
            pltpu.VMEM((4, tm + ATTN_BLOCK, LANES), BF16),
            pltpu.VMEM((4, tm + ATTN_BLOCK, LANES), BF16),
            pltpu.VMEM((2, 4 * ATTN_BLOCK, LANES), BF16),
            pltpu.VMEM((tm, Q_END), BF16),
            pltpu.VMEM((tm, SGU_WIDTH), BF16),
        ],
        compiler_params=pltpu.CompilerParams(
            dimension_semantics=("arbitrary", "arbitrary"),
            vmem_limit_bytes=VMEM_LIMIT_BYTES),
        name="token_mixer",
    )(x, mod, n1w, cos_tab, sin_tab, w_in, sinks, ln_w, ln_b, sgu_w, sgu_b_full,
      proj_a, proj_b, w_out)


CONV_PAD_ROWS = SUBLANES


def _ffn_kernel(x_ref, mod_ref, n2w_ref, wg_ref, wu_ref, cw_ref, cb_ref, wd_ref, fnw_ref, o_ref,
                a_s, act_s, *, final_norm):
    tm = TOKEN_TILE
    j = pl.program_id(1)

    @pl.when(j == 0)
    def _():
        a_s[0:CONV_PAD_ROWS, :] = jnp.zeros((CONV_PAD_ROWS, FFN_DIM), F32)

    x = x_ref[0]
    h = _rms_norm(x, n2w_ref[...]) * (1.0 + mod_ref[0, 4:5, :]) + mod_ref[0, 3:4, :]
    hb = h.astype(BF16)

    for c in range(FFN_DIM // MXU_COLS):
        cols = slice(c * MXU_COLS, (c + 1) * MXU_COLS)
        a = _dot(hb, wg_ref[:, cols])
        a_s[CONV_PAD_ROWS:CONV_PAD_ROWS + tm, cols] = a
        conv = cb_ref[:, cols] + cw_ref[2:3, cols] * a
        for tap in range(CONV_WIDTH - 1):
            back = CONV_WIDTH - 1 - tap
            conv = conv + cw_ref[tap:tap + 1, cols] * a_s[CONV_PAD_ROWS - back:CONV_PAD_ROWS - back + tm, cols]
        up = _dot(hb, wu_ref[:, cols])
        act_s[:, cols] = (conv * jax.nn.sigmoid(conv) * up).astype(BF16)

    a_s[0:CONV_PAD_ROWS, :] = a_s[tm:tm + CONV_PAD_ROWS, :]

    y = x + mod_ref[0, 5:6, :] * _dot(act_s[...], wd_ref[...])
    if final_norm:
        y = _rms_norm(y, fnw_ref[...])
    o_ref[0] = y


def _ffn_layer(x, mod, n2w, w_gate, w_up, conv_w, conv_b, w_down, final_w, final_norm):
    b, s, d = x.shape
    tm = TOKEN_TILE
    tile = pl.BlockSpec((1, tm, d), lambda i, j: (i, j, 0))
    return pl.pallas_call(
        functools.partial(_ffn_kernel, final_norm=final_norm),
        grid=(b, s // tm),
        in_specs=[
            tile,
            pl.BlockSpec((1, 6, d), lambda i, j: (i, 0, 0)),
            _full((1, d)),
            _full((d, FFN_DIM)), _full((d, FFN_DIM)),
            _full((CONV_WIDTH, FFN_DIM)), _full((1, FFN_DIM)),
            _full((FFN_DIM, d)),
            _full((1, d)),
        ],
        out_specs=tile,
        out_shape=jax.ShapeDtypeStruct(x.shape, x.dtype),
        scratch_shapes=[
            pltpu.VMEM((CONV_PAD_ROWS + tm, FFN_DIM), F32),
            pltpu.VMEM((tm, FFN_DIM), BF16),
        ],
        compiler_params=pltpu.CompilerParams(
            dimension_semantics=("arbitrary", "arbitrary"),
            vmem_limit_bytes=VMEM_LIMIT_BYTES),
        name="conv_ffn",
    )(x, mod, n2w, w_gate, w_up, conv_w, conv_b, w_down, final_w)


def kernel(x, c, positions, ada_w, ada_b, norm1_w, w_in, attn_sinks, sgu_ln_w, sgu_ln_b, sgu_w, sgu_b,
           proj_a, proj_b, w_out, norm2_w, ffn_w_gate, ffn_w_up, ffn_conv_w, ffn_conv_b, ffn_w_down,
           final_norm_w):
    batch, seq, d = x.shape
    depth = ada_w.shape[0]
    assert seq % TOKEN_TILE == 0 and TOKEN_TILE % ATTN_BLOCK == 0 and d == D_MODEL

    rows = -(-batch // SUBLANES) * SUBLANES
    c_pad = jnp.pad(c, ((0, rows - batch), (0, 0)))
    mod = _ada_modulation(c_pad, ada_w, ada_b)[:, :batch]
    cos_tab, sin_tab = _rope_tables(positions)

    for l in range(depth):
        mod_l = mod[l].reshape(batch, 6, d)
        sgu_b_full = jnp.repeat(sgu_b[l].T, SGU_WIDTH // SGU_GROUPS, axis=1)
        x = _mix_layer(
            x, mod_l, norm1_w[l].reshape(1, d), cos_tab, sin_tab, w_in[l].astype(BF16),
            attn_sinks[l], sgu_ln_w[l].reshape(1, -1), sgu_ln_b[l].reshape(1, -1),
            sgu_w[l].astype(BF16), sgu_b_full,
            proj_a[l].astype(BF16), proj_b[l].astype(BF16), w_out[l].astype(BF16))
        x = _ffn_layer(
            x, mod_l, norm2_w[l].reshape(1, d), ffn_w_gate[l].astype(BF16), ffn_w_up[l].astype(BF16),
            ffn_conv_w[l], ffn_conv_b[l].reshape(1, -1), ffn_w_down[l].astype(BF16),
            final_norm_w.reshape(1, d), final_norm=(l == depth - 1))
    return x
```

```python
import functools

import jax
import jax.numpy as jnp
from jax import lax
from jax.experimental import pallas as pl
from jax.experimental.pallas import tpu as pltpu

D_MODEL = 1024
N_Q_HEADS = 16
N_KV_HEADS = 2
HEAD_DIM = 64
Q_PER_KV = N_Q_HEADS // N_KV_HEADS
ATTN_BLOCK = 128
ROPE_THETA = 500000.0
ROT_DIM = HEAD_DIM // 4
SGU_WIDTH = 1024
SGU_GROUPS = 8
SGU_CHUNK = 128
FFN_DIM = 2816
CONV_WIDTH = 3
NORM_EPS = 1e-6

Q_END = N_Q_HEADS * HEAD_DIM
K_END = Q_END + N_KV_HEADS * HEAD_DIM
V_END = K_END + N_KV_HEADS * HEAD_DIM
Z_END = V_END + 2 * SGU_WIDTH
IN_COLS = Z_END + 2 * D_MODEL

LANES = 128
SUBLANES = 8
MXU_COLS = 256
TOKEN_TILE = 512
VMEM_LIMIT_BYTES = 56 * 1024 * 1024
MASK_VALUE = -1e30
LOG2E = 1.4426950408889634

F32 = jnp.float32
BF16 = jnp.bfloat16


def _rms_norm(x, w):
    ms = jnp.mean(x * x, axis=-1, keepdims=True)
    return x * lax.rsqrt(ms + NORM_EPS) * w


def _dot(a, b):
    return jnp.dot(a, b, preferred_element_type=F32)


BF16_ROW_TILE = 16


def _cast_specs(stacked_weights, layer, grid):
    steps = grid[0] * grid[1]
    specs, shapes = [], []
    for w in stacked_weights:
        _, rows, cols = w.shape
        slab, share = rows // steps, 1
        while slab % BF16_ROW_TILE:
            slab, share = slab * 2, share * 2
        assert rows % slab == 0 and steps % share == 0
        idx = lambda i, j, share=share: ((i * grid[1] + j) // share, 0)
        specs.append((pl.BlockSpec((None, slab, cols), lambda i, j, idx=idx: (layer,) + idx(i, j)),
                      pl.BlockSpec((slab, cols), idx)))
        shapes.append(jax.ShapeDtypeStruct((rows, cols), BF16))
    return [s[0] for s in specs], [s[1] for s in specs], shapes


def _cast_slabs(src_refs, dst_refs):
    for src, dst in zip(src_refs, dst_refs):
        dst[...] = src[...].astype(BF16)


ADA_COL_TILE = 1536


def _ada_kernel(c_ref, w_ref, b_ref, o_ref):
    c = c_ref[...]
    c_act = c * jax.nn.sigmoid(c)
    o_ref[0] = jnp.dot(c_act, w_ref[0], preferred_element_type=F32,
                       precision=lax.Precision.HIGHEST) + b_ref[0]


def _ada_modulation(c, ada_w, ada_b):
    depth, d, n = ada_w.shape
    rows = c.shape[0]
    return pl.pallas_call(
        _ada_kernel,
        grid=(depth, n // ADA_COL_TILE),
        in_specs=[
            pl.BlockSpec((rows, d), lambda l, j: (0, 0)),
            pl.BlockSpec((1, d, ADA_COL_TILE), lambda l, j: (l, 0, j)),
            pl.BlockSpec((1, 1, ADA_COL_TILE), lambda l, j: (l, 0, j)),
        ],
        out_specs=pl.BlockSpec((1, rows, ADA_COL_TILE), lambda l, j: (l, 0, j)),
        out_shape=jax.ShapeDtypeStruct((depth, rows, n), F32),
        compiler_params=pltpu.CompilerParams(
            dimension_semantics=("arbitrary", "arbitrary"),
            vmem_limit_bytes=VMEM_LIMIT_BYTES),
        name="ada_modulation",
    )(c, ada_w, ada_b.reshape(depth, 1, n))


ROPE_ROW_TILE = 1024


def _rope_kernel(pos_ref, invf_ref, *refs):
    n_cast = (len(refs) - 2) // 2
    cos_ref, sin_ref = refs[n_cast:n_cast + 2]
    _cast_slabs(refs[:n_cast], refs[n_cast + 2:])
    pos = pos_ref[0].astype(F32)
    ang = pos * invf_ref[...]
    lane = lax.broadcasted_iota(jnp.int32, ang.shape, 1) & (HEAD_DIM - 1)
    c = jnp.cos(ang)
    s = jnp.sin(ang)
    cos_ref[0] = jnp.where(lane < ROT_DIM, c, 1.0)
    sin_ref[0] = jnp.where(lane < ROT_DIM // 2, -s, jnp.where(lane < ROT_DIM, s, 0.0))


def _rope_tables(positions, cast_weights):
    b, s = positions.shape
    half = ROT_DIM // 2
    inv_freq = ROPE_THETA ** (-jnp.arange(0, ROT_DIM, 2, dtype=F32) / ROT_DIM)
    lane = jnp.arange(LANES) % HEAD_DIM
    invf_lane = jnp.where(lane < ROT_DIM, inv_freq[lane % half], 0.0).reshape(1, LANES)
    tab = jax.ShapeDtypeStruct((b, s, LANES), F32)
    grid = (b, s // ROPE_ROW_TILE)
    cast_in, cast_out, cast_shapes = _cast_specs(cast_weights, 0, grid)
    return pl.pallas_call(
        _rope_kernel,
        grid=grid,
        in_specs=[
            pl.BlockSpec((1, ROPE_ROW_TILE, 1), lambda i, j: (i, j, 0)),
            pl.BlockSpec((1, LANES), lambda i, j: (0, 0)),
        ] + cast_in,
        out_specs=[pl.BlockSpec((1, ROPE_ROW_TILE, LANES), lambda i, j: (i, j, 0))] * 2 + cast_out,
        out_shape=[tab, tab] + cast_shapes,
        compiler_params=pltpu.CompilerParams(
            dimension_semantics=("arbitrary", "arbitrary"),
            vmem_limit_bytes=VMEM_LIMIT_BYTES),
        name="rope_tables",
    )(positions.reshape(b, s, 1), invf_lane, *cast_weights)


MIX_N_INPUTS = 14


def _mix_kernel(*refs, n_cast):
    (x_ref, mod_ref, n1w_ref, cos_ref, sin_ref, win_ref, sink_ref, lnw_ref, lnb_ref,
     sguw_ref, sgub_ref, pa_ref, pb_ref, wo_ref) = refs[:MIX_N_INPUTS]
    o_ref = refs[MIX_N_INPUTS + n_cast]
    q_s, k4_s, v4_s, bias_s, yattn_s, ysgu_s = refs[MIX_N_INPUTS + 2 * n_cast + 1:]
    _cast_slabs(refs[MIX_N_INPUTS:MIX_N_INPUTS + n_cast],
                refs[MIX_N_INPUTS + n_cast + 1:MIX_N_INPUTS + 2 * n_cast + 1])
    tm = TOKEN_TILE
    nblk = tm // ATTN_BLOCK
    band_keys = 2 * ATTN_BLOCK
    j = pl.program_id(1)

    @pl.when(j == 0)
    def _():
        k4_s[:, 0:ATTN_BLOCK, :] = jnp.zeros((4, ATTN_BLOCK, LANES), BF16)
        v4_s[:, 0:ATTN_BLOCK, :] = jnp.zeros((4, ATTN_BLOCK, LANES), BF16)

    x = x_ref[0]
    h = _rms_norm(x, n1w_ref[...]) * (1.0 + mod_ref[0, 1:2, :]) + mod_ref[0, 0:1, :]
    hb = h.astype(BF16)

    q = _dot(hb, win_ref[:, 0:Q_END])
    kv = _dot(hb, win_ref[:, Q_END:V_END])
    z = _dot(hb, win_ref[:, V_END:Z_END])

    cos_t = cos_ref[0]
    sin_t = sin_ref[0]
    lane = lax.broadcasted_iota(jnp.int32, (tm, LANES), 1)
    first_half = (lane & (HEAD_DIM - 1)) < (ROT_DIM // 2)
    left_head = lane < HEAD_DIM

    def rope(t):
        partner = jnp.where(first_half,
                            pltpu.roll(t, LANES - ROT_DIM // 2, 1),
                            pltpu.roll(t, ROT_DIM // 2, 1))
        return t * cos_t + partner * sin_t

    for c in range(Q_END // LANES):
        qc = rope(q[:, c * LANES:(c + 1) * LANES]) * (HEAD_DIM ** -0.5 * LOG2E)
        q_s[:, c * LANES:(c + 1) * LANES] = qc.astype(BF16)

    k = rope(kv[:, 0:LANES])
    v = kv[:, LANES:2 * LANES]
    for src, dst in ((k, k4_s), (v, v4_s)):
        swapped = pltpu.roll(src, HEAD_DIM, 1)
        zero = jnp.zeros_like(src)
        rows = slice(ATTN_BLOCK, ATTN_BLOCK + tm)
        dst[0, rows, :] = jnp.where(left_head, src, zero).astype(BF16)
        dst[1, rows, :] = jnp.where(left_head, zero, swapped).astype(BF16)
        dst[2, rows, :] = jnp.where(left_head, swapped, zero).astype(BF16)
        dst[3, rows, :] = jnp.where(left_head, zero, src).astype(BF16)

    gates = _dot(hb, win_ref[:, Z_END:IN_COLS])

    z = 0.5 * z * (1.0 + lax.erf(z * (2.0 ** -0.5)))
    u = z[:, 0:SGU_WIDTH]
    v_sgu = z[:, SGU_WIDTH:]
    mu = jnp.mean(v_sgu, axis=-1, keepdims=True)
    dv = v_sgu - mu
    var = jnp.mean(dv * dv, axis=-1, keepdims=True)
    vn = (dv * lax.rsqrt(var + NORM_EPS) * lnw_ref[...] + lnb_ref[...]).astype(BF16)
    ti = lax.broadcasted_iota(jnp.int32, (SGU_CHUNK, SGU_CHUNK), 0)
    si = lax.broadcasted_iota(jnp.int32, (SGU_CHUNK, SGU_CHUNK), 1)
    causal = si <= ti
    gdim = SGU_WIDTH // SGU_GROUPS
    nchunk = tm // SGU_CHUNK
    for g in range(SGU_GROUPS):
        cols = slice(g * gdim, (g + 1) * gdim)
        wm = jnp.where(causal, sguw_ref[g], 0.0).astype(BF16)
        rhs = jnp.concatenate(
            [vn[c * SGU_CHUNK:(c + 1) * SGU_CHUNK, cols] for c in range(nchunk)], axis=1)
        f = _dot(wm, rhs)
        for c in range(nchunk):
            rows = slice(c * SGU_CHUNK, (c + 1) * SGU_CHUNK)
            fc = f[:, c * gdim:(c + 1) * gdim] + sgub_ref[:, cols]
            ysgu_s[rows, cols] = (u[rows, cols] * fc).astype(BF16)

    sgu_part = jax.nn.sigmoid(gates[:, 0:D_MODEL]) * _dot(ysgu_s[...], pa_ref[...])
    gate_b = jax.nn.sigmoid(gates[:, D_MODEL:])

    er = lax.broadcasted_iota(jnp.int32, (ATTN_BLOCK, LANES), 0)
    ec = lax.broadcasted_iota(jnp.int32, (ATTN_BLOCK, LANES), 1)
    eye = jnp.where(er == ec, 1.0, 0.0).astype(BF16)
    out_lane_left = ec < HEAD_DIM
    kr = lax.broadcasted_iota(jnp.int32, (band_keys, ATTN_BLOCK), 0)
    qc_i = lax.broadcasted_iota(jnp.int32, (band_keys, ATTN_BLOCK), 1)
    band_t = (kr > qc_i) & (kr <= qc_i + ATTN_BLOCK)
    for idx, ok in ((0, band_t & (kr >= ATTN_BLOCK)), (1, band_t)):
        bias_t = jnp.where(ok, 0.0, MASK_VALUE).astype(BF16)
        bias_s[idx, 0:band_keys, :] = bias_t
        bias_s[idx, band_keys:2 * band_keys, :] = bias_t
    sr = lax.broadcasted_iota(jnp.int32, (2 * band_keys, LANES), 0)
    sc = lax.broadcasted_iota(jnp.int32, (2 * band_keys, LANES), 1)
    ones_sel = jnp.where((sr < band_keys) == (sc < HEAD_DIM), 1.0, 0.0).astype(BF16)
    contract_last = (((1,), (1,)), ((), ()))

    def score_matmuls(b):
        r0 = b * ATTN_BLOCK
        bias_t = bias_s[jnp.where(j > 0, 1, 0)] if b == 0 else bias_s[1]
        scores = []
        value_ops = []
        for g in range(N_KV_HEADS):
            kbd = jnp.concatenate([k4_s[2 * g, pl.ds(r0, band_keys), :],
                                   k4_s[2 * g + 1, pl.ds(r0, band_keys), :]], axis=0)
            vbd = jnp.concatenate([v4_s[2 * g, pl.ds(r0, band_keys), :],
                                   v4_s[2 * g + 1, pl.ds(r0, band_keys), :]], axis=0)
            kbd = jnp.concatenate([kbd, bias_t], axis=1)
            value_ops.append(jnp.concatenate([vbd, ones_sel], axis=1))
            qrows = []
            for pr in range(Q_PER_KV // 2):
                col = (g * Q_PER_KV + 2 * pr) * HEAD_DIM
                qrows.append(jnp.concatenate([q_s[pl.ds(r0, ATTN_BLOCK), col:col + LANES], eye], axis=1))
            scores.append(lax.dot_general(jnp.concatenate(qrows, axis=0), kbd, contract_last,
                                          preferred_element_type=F32))
        return scores, value_ops

    def softmax_value_matmuls(b, scores, value_ops):
        r0 = b * ATTN_BLOCK
        for g in range(N_KV_HEADS):
            probs, sink_terms = [], []
            for pr in range(Q_PER_KV // 2):
                rows = slice(pr * ATTN_BLOCK, (pr + 1) * ATTN_BLOCK)
                pair_probs, pair_sinks = [], []
                for hh in range(2):
                    sh = scores[g][rows, hh * band_keys:(hh + 1) * band_keys]
                    sink = sink_ref[g * Q_PER_KV + 2 * pr + hh] * LOG2E
                    m = jnp.maximum(jnp.max(sh, axis=-1, keepdims=True), sink)
                    pair_probs.append(jnp.exp2(sh - m).astype(BF16))
                    pair_sinks.append(jnp.exp2(sink - m))
                probs.append(jnp.concatenate(pair_probs, axis=1))
                sink_terms.append(jnp.where(out_lane_left, pair_sinks[0], pair_sinks[1]))
            o = _dot(jnp.concatenate(probs, axis=0), value_ops[g])
            for pr in range(Q_PER_KV // 2):
                rows = slice(pr * ATTN_BLOCK, (pr + 1) * ATTN_BLOCK)
                col = (g * Q_PER_KV + 2 * pr) * HEAD_DIM
                denom = o[rows, LANES:] + sink_terms[pr]
                yattn_s[pl.ds(r0, ATTN_BLOCK), col:col + LANES] = (o[rows, 0:LANES] / denom).astype(BF16)

    pending = score_matmuls(0)
    for b in range(nblk):
        upcoming = score_matmuls(b + 1) if b + 1 < nblk else None
        softmax_value_matmuls(b, *pending)
        pending = upcoming

    k4_s[:, 0:ATTN_BLOCK, :] = k4_s[:, tm:tm + ATTN_BLOCK, :]
    v4_s[:, 0:ATTN_BLOCK, :] = v4_s[:, tm:tm + ATTN_BLOCK, :]

    merged = sgu_part + gate_b * _dot(yattn_s[...], pb_ref[...])
    o_ref[0] = x + mod_ref[0, 2:3, :] * _dot(merged.astype(BF16), wo_ref[...])


def _full(shape):
    zeros = (0,) * len(shape)
    return pl.BlockSpec(shape, lambda i, j: zeros)


def _mix_layer(x, mod, n1w, cos_tab, sin_tab, w_in, sinks, ln_w, ln_b, sgu_w, sgu_b_full,
               proj_a, proj_b, w_out, cast_weights, layer):
    b, s, d = x.shape
    tm = TOKEN_TILE
    tile = pl.BlockSpec((1, tm, d), lambda i, j: (i, j, 0))
    tab = pl.BlockSpec((1, tm, LANES), lambda i, j: (i, j, 0))
    grid = (b, s // tm)
    cast_in, cast_out, cast_shapes = _cast_specs(cast_weights, layer, grid)
    return pl.pallas_call(
        functools.partial(_mix_kernel, n_cast=len(cast_weights)),
        grid=grid,
        in_specs=[
            tile,
            pl.BlockSpec((1, 6, d), lambda i, j: (i, 0, 0)),
            _full((1, d)),
            tab, tab,
            _full((d, IN_COLS)),
            pl.BlockSpec(memory_space=pltpu.SMEM),
            _full((1, SGU_WIDTH)), _full((1, SGU_WIDTH)),
            _full((SGU_GROUPS, SGU_CHUNK, SGU_CHUNK)),
            _full((SGU_CHUNK, SGU_WIDTH)),
            _full((SGU_WIDTH, d)), _full((Q_END, d)), _full((d, d)),
        ] + cast_in,
        out_specs=[tile] + cast_out,
        out_shape=[jax.ShapeDtypeStruct(x.shape, x.dtype)] + cast_shapes,
        scratch_shapes=[
            pltpu.VMEM((tm, Q_END), BF16),
            pltpu.VMEM((4, tm + ATTN_BLOCK, LANES), BF16),
            pltpu.VMEM((4, tm + ATTN_BLOCK, LANES), BF16),
            pltpu.VMEM((2, 4 * ATTN_BLOCK, LANES), BF16),
            pltpu.VMEM((tm, Q_END), BF16),
            pltpu.VMEM((tm, SGU_WIDTH), BF16),
        ],
        compiler_params=pltpu.CompilerParams(
            dimension_semantics=("arbitrary", "arbitrary"),
            vmem_limit_bytes=VMEM_LIMIT_BYTES),
        name="token_mixer",
    )(x, mod, n1w, cos_tab, sin_tab, w_in, sinks, ln_w, ln_b, sgu_w, sgu_b_full,
      proj_a, proj_b, w_out, *cast_weights)


CONV_PAD_ROWS = SUBLANES


FFN_N_INPUTS = 9


def _ffn_kernel(*refs, n_cast, final_norm):
    (x_ref, mod_ref, n2w_ref, wg_ref, wu_ref, cw_ref, cb_ref, wd_ref, fnw_ref) = refs[:FFN_N_INPUTS]
    o_ref = refs[FFN_N_INPUTS + n_cast]
    a_s, act_s = refs[FFN_N_INPUTS + 2 * n_cast + 1:]
    _cast_slabs(refs[FFN_N_INPUTS:FFN_N_INPUTS + n_cast],
                refs[FFN_N_INPUTS + n_cast + 1:FFN_N_INPUTS + 2 * n_cast + 1])
    tm = TOKEN_TILE
    j = pl.program_id(1)

    @pl.when(j == 0)
    def _():
        a_s[0:CONV_PAD_ROWS, :] = jnp.zeros((CONV_PAD_ROWS, FFN_DIM), F32)

    x = x_ref[0]
    h = _rms_norm(x, n2w_ref[...]) * (1.0 + mod_ref[0, 4:5, :]) + mod_ref[0, 3:4, :]
    hb = h.astype(BF16)

    for c in range(FFN_DIM // MXU_COLS):
        cols = slice(c * MXU_COLS, (c + 1) * MXU_COLS)
        a = _dot(hb, wg_ref[:, cols])
        a_s[CONV_PAD_ROWS:CONV_PAD_ROWS + tm, cols] = a
        conv = cb_ref[:, cols] + cw_ref[2:3, cols] * a
        for tap in range(CONV_WIDTH - 1):
            back = CONV_WIDTH - 1 - tap
            conv = conv + cw_ref[tap:tap + 1, cols] * a_s[CONV_PAD_ROWS - back:CONV_PAD_ROWS - back + tm, cols]
        up = _dot(hb, wu_ref[:, cols])
        act_s[:, cols] = (conv * jax.nn.sigmoid(conv) * up).astype(BF16)

    a_s[0:CONV_PAD_ROWS, :] = a_s[tm:tm + CONV_PAD_ROWS, :]

    y = x + mod_ref[0, 5:6, :] * _dot(act_s[...], wd_ref[...])
    if final_norm:
        y = _rms_norm(y, fnw_ref[...])
    o_ref[0] = y


def _ffn_layer(x, mod, n2w, w_gate, w_up, conv_w, conv_b, w_down, final_w, final_norm,
               cast_weights, layer):
    b, s, d = x.shape
    tm = TOKEN_TILE
    tile = pl.BlockSpec((1, tm, d), lambda i, j: (i, j, 0))
    grid = (b, s // tm)
    cast_in, cast_out, cast_shapes = _cast_specs(cast_weights, layer, grid)
    return pl.pallas_call(
        functools.partial(_ffn_kernel, n_cast=len(cast_weights), final_norm=final_norm),
        grid=grid,
        in_specs=[
            tile,
            pl.BlockSpec((1, 6, d), lambda i, j: (i, 0, 0)),
            _full((1, d)),
            _full((d, FFN_DIM)), _full((d, FFN_DIM)),
            _full((CONV_WIDTH, FFN_DIM)), _full((1, FFN_DIM)),
            _full((FFN_DIM, d)),
            _full((1, d)),
        ] + cast_in,
        out_specs=[tile] + cast_out,
        out_shape=[jax.ShapeDtypeStruct(x.shape, x.dtype)] + cast_shapes,
        scratch_shapes=[
            pltpu.VMEM((CONV_PAD_ROWS + tm, FFN_DIM), F32),
            pltpu.VMEM((tm, FFN_DIM), BF16),
        ],
        compiler_params=pltpu.CompilerParams(
            dimension_semantics=("arbitrary", "arbitrary"),
            vmem_limit_bytes=VMEM_LIMIT_BYTES),
        name="conv_ffn",
    )(x, mod, n2w, w_gate, w_up, conv_w, conv_b, w_down, final_w, *cast_weights)


def kernel(x, c, positions, ada_w, ada_b, norm1_w, w_in, attn_sinks, sgu_ln_w, sgu_ln_b, sgu_w, sgu_b,
           proj_a, proj_b, w_out, norm2_w, ffn_w_gate, ffn_w_up, ffn_conv_w, ffn_conv_b, ffn_w_down,
           final_norm_w):
    batch, seq, d = x.shape
    depth = ada_w.shape[0]
    assert seq % TOKEN_TILE == 0 and TOKEN_TILE % ATTN_BLOCK == 0 and d == D_MODEL

    rows = -(-batch // SUBLANES) * SUBLANES
    c_pad = jnp.pad(c, ((0, rows - batch), (0, 0)))
    mod = _ada_modulation(c_pad, ada_w, ada_b)[:, :batch]
    mix_f32 = (w_in, proj_a, proj_b, w_out)
    ffn_f32 = (ffn_w_gate, ffn_w_up, ffn_w_down)
    cos_tab, sin_tab, *mix_w = _rope_tables(positions, mix_f32)

    for l in range(depth):
        mod_l = mod[l].reshape(batch, 6, d)
        sgu_b_full = jnp.repeat(sgu_b[l].T, SGU_WIDTH // SGU_GROUPS, axis=1)
        x, *ffn_w = _mix_layer(
            x, mod_l, norm1_w[l].reshape(1, d), cos_tab, sin_tab, mix_w[0],
            attn_sinks[l], sgu_ln_w[l].reshape(1, -1), sgu_ln_b[l].reshape(1, -1),
            sgu_w[l], sgu_b_full, mix_w[1], mix_w[2], mix_w[3], ffn_f32, l)
        last = l == depth - 1
        x, *mix_w = _ffn_layer(
            x, mod_l, norm2_w[l].reshape(1, d), ffn_w[0], ffn_w[1],
            ffn_conv_w[l], ffn_conv_b[l].reshape(1, -1), ffn_w[2],
            final_norm_w.reshape(1, d), last, () if last else mix_f32, l + 1)
    return x
```

```python
import functools

import jax
import jax.numpy as jnp
from jax import lax
from jax.experimental import pallas as pl
from jax.experimental.pallas import tpu as pltpu

D_MODEL = 1024
N_Q_HEADS = 16
N_KV_HEADS = 2
HEAD_DIM = 64
Q_PER_KV = N_Q_HEADS // N_KV_HEADS
ATTN_BLOCK = 128
ROPE_THETA = 500000.0
ROT_DIM = HEAD_DIM // 4
SGU_WIDTH = 1024
SGU_GROUPS = 8
SGU_CHUNK = 128
FFN_DIM = 2816
CONV_WIDTH = 3
NORM_EPS = 1e-6

Q_END = N_Q_HEADS * HEAD_DIM
K_END = Q_END + N_KV_HEADS * HEAD_DIM
V_END = K_END + N_KV_HEADS * HEAD_DIM
Z_END = V_END + 2 * SGU_WIDTH
IN_COLS = Z_END + 2 * D_MODEL

LANES = 128
SUBLANES = 8
MXU_COLS = 256
TOKEN_TILE = 512
NORM_ROW_CHUNK = 128
OUT_ROW_CHUNK = 256
VMEM_LIMIT_BYTES = 56 * 1024 * 1024
MASK_VALUE = -1e30
LOG2E = 1.4426950408889634

F32 = jnp.float32
BF16 = jnp.bfloat16


def _rms_norm(x, w):
    ms = jnp.mean(x * x, axis=-1, keepdims=True)
    return x * lax.rsqrt(ms + NORM_EPS) * w


def _dot(a, b):
    return jnp.dot(a, b, preferred_element_type=F32)


BF16_ROW_TILE = 16


def _cast_specs(stacked_weights, layer, grid):
    steps = grid[0] * grid[1]
    specs, shapes = [], []
    for w in stacked_weights:
        _, rows, cols = w.shape
        slab, share = rows // steps, 1
        while slab % BF16_ROW_TILE:
            slab, share = slab * 2, share * 2
        assert rows % slab == 0 and steps % share == 0
        idx = lambda i, j, share=share: ((i * grid[1] + j) // share, 0)
        specs.append((pl.BlockSpec((None, slab, cols), lambda i, j, idx=idx: (layer,) + idx(i, j)),
                      pl.BlockSpec((slab, cols), idx)))
        shapes.append(jax.ShapeDtypeStruct((rows, cols), BF16))
    return [s[0] for s in specs], [s[1] for s in specs], shapes


def _cast_slabs(src_refs, dst_refs):
    for src, dst in zip(src_refs, dst_refs):
        dst[...] = src[...].astype(BF16)


ADA_COL_TILE = 1536


def _ada_kernel(c_ref, w_ref, b_ref, o_ref):
    c = c_ref[...]
    c_act = c * jax.nn.sigmoid(c)
    o_ref[0] = jnp.dot(c_act, w_ref[0], preferred_element_type=F32,
                       precision=lax.Precision.HIGHEST) + b_ref[0]


def _ada_modulation(c, ada_w, ada_b):
    depth, d, n = ada_w.shape
    rows = c.shape[0]
    return pl.pallas_call(
        _ada_kernel,
        grid=(depth, n // ADA_COL_TILE),
        in_specs=[
            pl.BlockSpec((rows, d), lambda l, j: (0, 0)),
            pl.BlockSpec((1, d, ADA_COL_TILE), lambda l, j: (l, 0, j)),
            pl.BlockSpec((1, 1, ADA_COL_TILE), lambda l, j: (l, 0, j)),
        ],
        out_specs=pl.BlockSpec((1, rows, ADA_COL_TILE), lambda l, j: (l, 0, j)),
        out_shape=jax.ShapeDtypeStruct((depth, rows, n), F32),
        compiler_params=pltpu.CompilerParams(
            dimension_semantics=("arbitrary", "arbitrary"),
            vmem_limit_bytes=VMEM_LIMIT_BYTES),
        name="ada_modulation",
    )(c, ada_w, ada_b.reshape(depth, 1, n))


ROPE_ROW_TILE = 1024


def _rope_kernel(pos_ref, invf_ref, *refs):
    n_cast = (len(refs) - 2) // 2
    cos_ref, sin_ref = refs[n_cast:n_cast + 2]
    _cast_slabs(refs[:n_cast], refs[n_cast + 2:])
    pos = pos_ref[0].astype(F32)
    ang = pos * invf_ref[...]
    lane = lax.broadcasted_iota(jnp.int32, ang.shape, 1) & (HEAD_DIM - 1)
    c = jnp.cos(ang)
    s = jnp.sin(ang)
    cos_ref[0] = jnp.where(lane < ROT_DIM, c, 1.0)
    sin_ref[0] = jnp.where(lane < ROT_DIM // 2, -s, jnp.where(lane < ROT_DIM, s, 0.0))


def _rope_tables(positions, cast_weights):
    b, s = positions.shape
    half = ROT_DIM // 2
    inv_freq = ROPE_THETA ** (-jnp.arange(0, ROT_DIM, 2, dtype=F32) / ROT_DIM)
    lane = jnp.arange(LANES) % HEAD_DIM
    invf_lane = jnp.where(lane < ROT_DIM, inv_freq[lane % half], 0.0).reshape(1, LANES)
    tab = jax.ShapeDtypeStruct((b, s, LANES), F32)
    grid = (b, s // ROPE_ROW_TILE)
    cast_in, cast_out, cast_shapes = _cast_specs(cast_weights, 0, grid)
    return pl.pallas_call(
        _rope_kernel,
        grid=grid,
        in_specs=[
            pl.BlockSpec((1, ROPE_ROW_TILE, 1), lambda i, j: (i, j, 0)),
            pl.BlockSpec((1, LANES), lambda i, j: (0, 0)),
        ] + cast_in,
        out_specs=[pl.BlockSpec((1, ROPE_ROW_TILE, LANES), lambda i, j: (i, j, 0))] * 2 + cast_out,
        out_shape=[tab, tab] + cast_shapes,
        compiler_params=pltpu.CompilerParams(
            dimension_semantics=("arbitrary", "arbitrary"),
            vmem_limit_bytes=VMEM_LIMIT_BYTES),
        name="rope_tables",
    )(positions.reshape(b, s, 1), invf_lane, *cast_weights)


MIX_N_INPUTS = 14


def _mix_kernel(*refs, n_cast):
    (x_ref, mod_ref, n1w_ref, cos_ref, sin_ref, win_ref, sink_ref, lnw_ref, lnb_ref,
     sguw_ref, sgub_ref, pa_ref, pb_ref, wo_ref) = refs[:MIX_N_INPUTS]
    o_ref = refs[MIX_N_INPUTS + n_cast]
    q_s, k4_s, v4_s, bias_s, yattn_s, ysgu_s = refs[MIX_N_INPUTS + 2 * n_cast + 1:]
    _cast_slabs(refs[MIX_N_INPUTS:MIX_N_INPUTS + n_cast],
                refs[MIX_N_INPUTS + n_cast + 1:MIX_N_INPUTS + 2 * n_cast + 1])
    tm = TOKEN_TILE
    nblk = tm // ATTN_BLOCK
    band_keys = 2 * ATTN_BLOCK
    j = pl.program_id(1)

    @pl.when(j == 0)
    def _():
        k4_s[:, 0:ATTN_BLOCK, :] = jnp.zeros((4, ATTN_BLOCK, LANES), BF16)
        v4_s[:, 0:ATTN_BLOCK, :] = jnp.zeros((4, ATTN_BLOCK, LANES), BF16)

    scale1 = 1.0 + mod_ref[0, 1:2, :]
    hb_chunks, q_chunks = [], []
    for r in range(tm // NORM_ROW_CHUNK):
        xr = x_ref[0, r * NORM_ROW_CHUNK:(r + 1) * NORM_ROW_CHUNK, :]
        hr = (_rms_norm(xr, n1w_ref[...]) * scale1 + mod_ref[0, 0:1, :]).astype(BF16)
        hb_chunks.append(hr)
        q_chunks.append(_dot(hr, win_ref[:, 0:Q_END]))
    hb = jnp.concatenate(hb_chunks, axis=0)
    q = jnp.concatenate(q_chunks, axis=0)
    kv = _dot(hb, win_ref[:, Q_END:V_END])
    z = _dot(hb, win_ref[:, V_END:Z_END])

    cos_t = cos_ref[0]
    sin_t = sin_ref[0]
    lane = lax.broadcasted_iota(jnp.int32, (tm, LANES), 1)
    first_half = (lane & (HEAD_DIM - 1)) < (ROT_DIM // 2)
    left_head = lane < HEAD_DIM

    def rope(t):
        partner = jnp.where(first_half,
                            pltpu.roll(t, LANES - ROT_DIM // 2, 1),
                            pltpu.roll(t, ROT_DIM // 2, 1))
        return t * cos_t + partner * sin_t

    for c in range(Q_END // LANES):
        qc = rope(q[:, c * LANES:(c + 1) * LANES]) * (HEAD_DIM ** -0.5 * LOG2E)
        q_s[:, c * LANES:(c + 1) * LANES] = qc.astype(BF16)

    k = rope(kv[:, 0:LANES])
    v = kv[:, LANES:2 * LANES]
    for src, dst in ((k, k4_s), (v, v4_s)):
        swapped = pltpu.roll(src, HEAD_DIM, 1)
        zero = jnp.zeros_like(src)
        rows = slice(ATTN_BLOCK, ATTN_BLOCK + tm)
        dst[0, rows, :] = jnp.where(left_head, src, zero).astype(BF16)
        dst[1, rows, :] = jnp.where(left_head, zero, swapped).astype(BF16)
        dst[2, rows, :] = jnp.where(left_head, swapped, zero).astype(BF16)
        dst[3, rows, :] = jnp.where(left_head, zero, src).astype(BF16)

    gates = _dot(hb, win_ref[:, Z_END:IN_COLS])

    z = 0.5 * z * (1.0 + lax.erf(z * (2.0 ** -0.5)))
    u = z[:, 0:SGU_WIDTH]
    v_sgu = z[:, SGU_WIDTH:]
    mu = jnp.mean(v_sgu, axis=-1, keepdims=True)
    dv = v_sgu - mu
    var = jnp.mean(dv * dv, axis=-1, keepdims=True)
    vn = (dv * lax.rsqrt(var + NORM_EPS) * lnw_ref[...] + lnb_ref[...]).astype(BF16)
    ti = lax.broadcasted_iota(jnp.int32, (SGU_CHUNK, SGU_CHUNK), 0)
    si = lax.broadcasted_iota(jnp.int32, (SGU_CHUNK, SGU_CHUNK), 1)
    causal = si <= ti
    gdim = SGU_WIDTH // SGU_GROUPS
    nchunk = tm // SGU_CHUNK
    for g in range(SGU_GROUPS):
        cols = slice(g * gdim, (g + 1) * gdim)
        wm = jnp.where(causal, sguw_ref[g], 0.0).astype(BF16)
        rhs = jnp.concatenate(
            [vn[c * SGU_CHUNK:(c + 1) * SGU_CHUNK, cols] for c in range(nchunk)], axis=1)
        f = _dot(wm, rhs)
        for c in range(nchunk):
            rows = slice(c * SGU_CHUNK, (c + 1) * SGU_CHUNK)
            fc = f[:, c * gdim:(c + 1) * gdim] + sgub_ref[:, cols]
            ysgu_s[rows, cols] = (u[rows, cols] * fc).astype(BF16)

    sgu_part = jax.nn.sigmoid(gates[:, 0:D_MODEL]) * _dot(ysgu_s[...], pa_ref[...])
    gate_b = jax.nn.sigmoid(gates[:, D_MODEL:])

    er = lax.broadcasted_iota(jnp.int32, (ATTN_BLOCK, LANES), 0)
    ec = lax.broadcasted_iota(jnp.int32, (ATTN_BLOCK, LANES), 1)
    eye = jnp.where(er == ec, 1.0, 0.0).astype(BF16)
    out_lane_left = ec < HEAD_DIM
    kr = lax.broadcasted_iota(jnp.int32, (band_keys, ATTN_BLOCK), 0)
    qc_i = lax.broadcasted_iota(jnp.int32, (band_keys, ATTN_BLOCK), 1)
    band_t = (kr > qc_i) & (kr <= qc_i + ATTN_BLOCK)
    for idx, ok in ((0, band_t & (kr >= ATTN_BLOCK)), (1, band_t)):
        bias_t = jnp.where(ok, 0.0, MASK_VALUE).astype(BF16)
        bias_s[idx, 0:band_keys, :] = bias_t
        bias_s[idx, band_keys:2 * band_keys, :] = bias_t
    sr = lax.broadcasted_iota(jnp.int32, (2 * band_keys, LANES), 0)
    sc = lax.broadcasted_iota(jnp.int32, (2 * band_keys, LANES), 1)
    ones_sel = jnp.where((sr < band_keys) == (sc < HEAD_DIM), 1.0, 0.0).astype(BF16)
    contract_last = (((1,), (1,)), ((), ()))

    def score_matmuls(b):
        r0 = b * ATTN_BLOCK
        bias_t = bias_s[jnp.where(j > 0, 1, 0)] if b == 0 else bias_s[1]
        scores = []
        value_ops = []
        for g in range(N_KV_HEADS):
            kbd = jnp.concatenate([k4_s[2 * g, pl.ds(r0, band_keys), :],
                                   k4_s[2 * g + 1, pl.ds(r0, band_keys), :]], axis=0)
            vbd = jnp.concatenate([v4_s[2 * g, pl.ds(r0, band_keys), :],
                                   v4_s[2 * g + 1, pl.ds(r0, band_keys), :]], axis=0)
            kbd = jnp.concatenate([kbd, bias_t], axis=1)
            value_ops.append(jnp.concatenate([vbd, ones_sel], axis=1))
            qrows = []
            for pr in range(Q_PER_KV // 2):
                col = (g * Q_PER_KV + 2 * pr) * HEAD_DIM
                qrows.append(jnp.concatenate([q_s[pl.ds(r0, ATTN_BLOCK), col:col + LANES], eye], axis=1))
            scores.append(lax.dot_general(jnp.concatenate(qrows, axis=0), kbd, contract_last,
                                          preferred_element_type=F32))
        return scores, value_ops

    def softmax_value_matmuls(b, scores, value_ops):
        r0 = b * ATTN_BLOCK
        for g in range(N_KV_HEADS):
            probs, sink_terms = [], []
            for pr in range(Q_PER_KV // 2):
                rows = slice(pr * ATTN_BLOCK, (pr + 1) * ATTN_BLOCK)
                pair_probs, pair_sinks = [], []
                for hh in range(2):
                    sh = scores[g][rows, hh * band_keys:(hh + 1) * band_keys]
                    sink = sink_ref[g * Q_PER_KV + 2 * pr + hh] * LOG2E
                    m = jnp.maximum(jnp.max(sh, axis=-1, keepdims=True), sink)
                    pair_probs.append(jnp.exp2(sh - m).astype(BF16))
                    pair_sinks.append(jnp.exp2(sink - m))
                probs.append(jnp.concatenate(pair_probs, axis=1))
                sink_terms.append(jnp.where(out_lane_left, pair_sinks[0], pair_sinks[1]))
            o = _dot(jnp.concatenate(probs, axis=0), value_ops[g])
            for pr in range(Q_PER_KV // 2):
                rows = slice(pr * ATTN_BLOCK, (pr + 1) * ATTN_BLOCK)
                col = (g * Q_PER_KV + 2 * pr) * HEAD_DIM
                denom = o[rows, LANES:] + sink_terms[pr]
                yattn_s[pl.ds(r0, ATTN_BLOCK), col:col + LANES] = (o[rows, 0:LANES] / denom).astype(BF16)

    pending = score_matmuls(0)
    for b in range(nblk):
        upcoming = score_matmuls(b + 1) if b + 1 < nblk else None
        softmax_value_matmuls(b, *pending)
        pending = upcoming

    k4_s[:, 0:ATTN_BLOCK, :] = k4_s[:, tm:tm + ATTN_BLOCK, :]
    v4_s[:, 0:ATTN_BLOCK, :] = v4_s[:, tm:tm + ATTN_BLOCK, :]

    merged = (sgu_part + gate_b * _dot(yattn_s[...], pb_ref[...])).astype(BF16)
    for r in range(tm // OUT_ROW_CHUNK):
        rows = slice(r * OUT_ROW_CHUNK, (r + 1) * OUT_ROW_CHUNK)
        o_ref[0, rows, :] = x_ref[0, rows, :] + mod_ref[0, 2:3, :] * _dot(merged[rows, :], wo_ref[...])


def _full(shape):
    zeros = (0,) * len(shape)
    return pl.BlockSpec(shape, lambda i, j: zeros)


def _mix_layer(x, mod, n1w, cos_tab, sin_tab, w_in, sinks, ln_w, ln_b, sgu_w, sgu_b_full,
               proj_a, proj_b, w_out, cast_weights, layer):
    b, s, d = x.shape
    tm = TOKEN_TILE
    tile = pl.BlockSpec((1, tm, d), lambda i, j: (i, j, 0))
    tab = pl.BlockSpec((1, tm, LANES), lambda i, j: (i, j, 0))
    grid = (b, s // tm)
    cast_in, cast_out, cast_shapes = _cast_specs(cast_weights, layer, grid)
    return pl.pallas_call(
        functools.partial(_mix_kernel, n_cast=len(cast_weights)),
        grid=grid,
        in_specs=[
            tile,
            pl.BlockSpec((1, 6, d), lambda i, j: (i, 0, 0)),
            _full((1, d)),
            tab, tab,
            _full((d, IN_COLS)),
            pl.BlockSpec(memory_space=pltpu.SMEM),
            _full((1, SGU_WIDTH)), _full((1, SGU_WIDTH)),
            _full((SGU_GROUPS, SGU_CHUNK, SGU_CHUNK)),
            _full((SGU_CHUNK, SGU_WIDTH)),
            _full((SGU_WIDTH, d)), _full((Q_END, d)), _full((d, d)),
        ] + cast_in,
        out_specs=[tile] + cast_out,
        out_shape=[jax.ShapeDtypeStruct(x.shape, x.dtype)] + cast_shapes,
        scratch_shapes=[
            pltpu.VMEM((tm, Q_END), BF16),
            pltpu.VMEM((4, tm + ATTN_BLOCK, LANES), BF16),
            pltpu.VMEM((4, tm + ATTN_BLOCK, LANES), BF16),
            pltpu.VMEM((2, 4 * ATTN_BLOCK, LANES), BF16),
            pltpu.VMEM((tm, Q_END), BF16),
            pltpu.VMEM((tm, SGU_WIDTH), BF16),
        ],
        compiler_params=pltpu.CompilerParams(
            dimension_semantics=("arbitrary", "arbitrary"),
            vmem_limit_bytes=VMEM_LIMIT_BYTES),
        name="token_mixer",
    )(x, mod, n1w, cos_tab, sin_tab, w_in, sinks, ln_w, ln_b, sgu_w, sgu_b_full,
      proj_a, proj_b, w_out, *cast_weights)


CONV_PAD_ROWS = SUBLANES


FFN_N_INPUTS = 9


def _ffn_kernel(*refs, n_cast, final_norm):
    (x_ref, mod_ref, n2w_ref, wg_ref, wu_ref, cw_ref, cb_ref, wd_ref, fnw_ref) = refs[:FFN_N_INPUTS]
    o_ref = refs[FFN_N_INPUTS + n_cast]
    a_s, act_s = refs[FFN_N_INPUTS + 2 * n_cast + 1:]
    _cast_slabs(refs[FFN_N_INPUTS:FFN_N_INPUTS + n_cast],
                refs[FFN_N_INPUTS + n_cast + 1:FFN_N_INPUTS + 2 * n_cast + 1])
    tm = TOKEN_TILE
    j = pl.program_id(1)

    @pl.when(j == 0)
    def _():
        a_s[0:CONV_PAD_ROWS, :] = jnp.zeros((CONV_PAD_ROWS, FFN_DIM), F32)

    scale2 = 1.0 + mod_ref[0, 4:5, :]
    hb_chunks, a_chunks, up_chunks = [], [], []
    for r in range(tm // NORM_ROW_CHUNK):
        xr = x_ref[0, r * NORM_ROW_CHUNK:(r + 1) * NORM_ROW_CHUNK, :]
        hr = (_rms_norm(xr, n2w_ref[...]) * scale2 + mod_ref[0, 3:4, :]).astype(BF16)
        hb_chunks.append(hr)
        a_chunks.append(_dot(hr, wg_ref[:, 0:MXU_COLS]))
        up_chunks.append(_dot(hr, wu_ref[:, 0:MXU_COLS]))
    hb = jnp.concatenate(hb_chunks, axis=0)
    first_a = jnp.concatenate(a_chunks, axis=0)
    first_up = jnp.concatenate(up_chunks, axis=0)

    for c in range(FFN_DIM // MXU_COLS):
        cols = slice(c * MXU_COLS, (c + 1) * MXU_COLS)
        a = first_a if c == 0 else _dot(hb, wg_ref[:, cols])
        a_s[CONV_PAD_ROWS:CONV_PAD_ROWS + tm, cols] = a
        conv = cb_ref[:, cols] + cw_ref[2:3, cols] * a
        for tap in range(CONV_WIDTH - 1):
            back = CONV_WIDTH - 1 - tap
            conv = conv + cw_ref[tap:tap + 1, cols] * a_s[CONV_PAD_ROWS - back:CONV_PAD_ROWS - back + tm, cols]
        up = first_up if c == 0 else _dot(hb, wu_ref[:, cols])
        act_s[:, cols] = (conv * jax.nn.sigmoid(conv) * up).astype(BF16)

    a_s[0:CONV_PAD_ROWS, :] = a_s[tm:tm + CONV_PAD_ROWS, :]

    for r in range(tm // OUT_ROW_CHUNK):
        rows = slice(r * OUT_ROW_CHUNK, (r + 1) * OUT_ROW_CHUNK)
        y = x_ref[0, rows, :] + mod_ref[0, 5:6, :] * _dot(act_s[rows, :], wd_ref[...])
        if final_norm:
            y = _rms_norm(y, fnw_ref[...])
        o_ref[0, rows, :] = y


def _ffn_layer(x, mod, n2w, w_gate, w_up, conv_w, conv_b, w_down, final_w, final_norm,
               cast_weights, layer):
    b, s, d = x.shape
    tm = TOKEN_TILE
    tile = pl.BlockSpec((1, tm, d), lambda i, j: (i, j, 0))
    grid = (b, s // tm)
    cast_in, cast_out, cast_shapes = _cast_specs(cast_weights, layer, grid)
    return pl.pallas_call(
        functools.partial(_ffn_kernel, n_cast=len(cast_weights), final_norm=final_norm),
        grid=grid,
        in_specs=[
            tile,
            pl.BlockSpec((1, 6, d), lambda i, j: (i, 0, 0)),
            _full((1, d)),
            _full((d, FFN_DIM)), _full((d, FFN_DIM)),
            _full((CONV_WIDTH, FFN_DIM)), _full((1, FFN_DIM)),
            _full((FFN_DIM, d)),
            _full((1, d)),
        ] + cast_in,
        out_specs=[tile] + cast_out,
        out_shape=[jax.ShapeDtypeStruct(x.shape, x.dtype)] + cast_shapes,
        scratch_shapes=[
            pltpu.VMEM((CONV_PAD_ROWS + tm, FFN_DIM), F32),
            pltpu.VMEM((tm, FFN_DIM), BF16),
        ],
        compiler_params=pltpu.CompilerParams(
            dimension_semantics=("arbitrary", "arbitrary"),
            vmem_limit_bytes=VMEM_LIMIT_BYTES),
        name="conv_ffn",
    )(x, mod, n2w, w_gate, w_up, conv_w, conv_b, w_down, final_w, *cast_weights)


def kernel(x, c, positions, ada_w, ada_b, norm1_w, w_in, attn_sinks, sgu_ln_w, sgu_ln_b, sgu_w, sgu_b,
           proj_a, proj_b, w_out, norm2_w, ffn_w_gate, ffn_w_up, ffn_conv_w, ffn_conv_b, ffn_w_down,
           final_norm_w):
    batch, seq, d = x.shape
    depth = ada_w.shape[0]
    assert seq % TOKEN_TILE == 0 and TOKEN_TILE % ATTN_BLOCK == 0 and d == D_MODEL

    rows = -(-batch // SUBLANES) * SUBLANES
    c_pad = jnp.pad(c, ((0, rows - batch), (0, 0)))
    mod = _ada_modulation(c_pad, ada_w, ada_b)[:, :batch]
    mix_f32 = (w_in, proj_a, proj_b, w_out)
    ffn_f32 = (ffn_w_gate, ffn_w_up, ffn_w_down)
    cos_tab, sin_tab, *mix_w = _rope_tables(positions, mix_f32)

    for l in range(depth):
        mod_l = mod[l].reshape(batch, 6, d)
        sgu_b_full = jnp.repeat(sgu_b[l].T, SGU_WIDTH // SGU_GROUPS, axis=1)
        x, *ffn_w = _mix_layer(
            x, mod_l, norm1_w[l].reshape(1, d), cos_tab, sin_tab, mix_w[0],
            attn_sinks[l], sgu_ln_w[l].reshape(1, -1), sgu_ln_b[l].reshape(1, -1),
            sgu_w[l], sgu_b_full, mix_w[1], mix_w[2], mix_w[3], ffn_f32, l)
        last = l == depth - 1
        x, *mix_w = _ffn_layer(
            x, mod_l, norm2_w[l].reshape(1, d), ffn_w[0], ffn_w[1],
            ffn_conv_w[l], ffn_conv_b[l].reshape(1, -1), ffn_w[2],
            final_norm_w.reshape(1, d), last, () if last else mix_f32, l + 1)
    return x
```

```python
import functools

import jax
import jax.numpy as jnp
from jax import lax
from jax.experimental import pallas as pl
from jax.experimental.pallas import tpu as pltpu

D_MODEL = 1024
N_Q_HEADS = 16
N_KV_HEADS = 2
HEAD_DIM = 64
Q_PER_KV = N_Q_HEADS // N_KV_HEADS
ATTN_BLOCK = 128
ROPE_THETA = 500000.0
ROT_DIM = HEAD_DIM // 4
SGU_WIDTH = 1024
SGU_GROUPS = 8
SGU_CHUNK = 128
FFN_DIM = 2816
CONV_WIDTH = 3
NORM_EPS = 1e-6

Q_END = N_Q_HEADS * HEAD_DIM
K_END = Q_END + N_KV_HEADS * HEAD_DIM
V_END = K_END + N_KV_HEADS * HEAD_DIM
Z_END = V_END + 2 * SGU_WIDTH
IN_COLS = Z_END + 2 * D_MODEL

LANES = 128
SUBLANES = 8
MXU_COLS = 256
TOKEN_TILE = 512
NORM_ROW_CHUNK = 128
OUT_ROW_CHUNK = 256
VMEM_LIMIT_BYTES = 56 * 1024 * 1024
MASK_VALUE = -1e30
LOG2E = 1.4426950408889634

F32 = jnp.float32
BF16 = jnp.bfloat16


def _rms_norm(x, w):
    ms = jnp.mean(x * x, axis=-1, keepdims=True)
    return x * lax.rsqrt(ms + NORM_EPS) * w


def _dot(a, b):
    return jnp.dot(a, b, preferred_element_type=F32)


BF16_ROW_TILE = 16


def _cast_specs(stacked_weights, layer, grid):
    steps = grid[0] * grid[1]
    specs, shapes = [], []
    for w in stacked_weights:
        _, rows, cols = w.shape
        slab, share = rows // steps, 1
        while slab % BF16_ROW_TILE:
            slab, share = slab * 2, share * 2
        assert rows % slab == 0 and steps % share == 0
        idx = lambda i, j, share=share: ((i * grid[1] + j) // share, 0)
        specs.append((pl.BlockSpec((None, slab, cols), lambda i, j, idx=idx: (layer,) + idx(i, j)),
                      pl.BlockSpec((slab, cols), idx)))
        shapes.append(jax.ShapeDtypeStruct((rows, cols), BF16))
    return [s[0] for s in specs], [s[1] for s in specs], shapes


def _cast_slabs(src_refs, dst_refs):
    for src, dst in zip(src_refs, dst_refs):
        dst[...] = src[...].astype(BF16)


ADA_COL_TILE = 1536


def _ada_kernel(c_ref, w_ref, b_ref, o_ref):
    c = c_ref[...]
    c_act = c * jax.nn.sigmoid(c)
    o_ref[0] = jnp.dot(c_act, w_ref[0], preferred_element_type=F32,
                       precision=lax.Precision.HIGHEST) + b_ref[0]


def _ada_modulation(c, ada_w, ada_b):
    depth, d, n = ada_w.shape
    rows = c.shape[0]
    return pl.pallas_call(
        _ada_kernel,
        grid=(depth, n // ADA_COL_TILE),
        in_specs=[
            pl.BlockSpec((rows, d), lambda l, j: (0, 0)),
            pl.BlockSpec((1, d, ADA_COL_TILE), lambda l, j: (l, 0, j)),
            pl.BlockSpec((1, 1, ADA_COL_TILE), lambda l, j: (l, 0, j)),
        ],
        out_specs=pl.BlockSpec((1, rows, ADA_COL_TILE), lambda l, j: (l, 0, j)),
        out_shape=jax.ShapeDtypeStruct((depth, rows, n), F32),
        compiler_params=pltpu.CompilerParams(
            dimension_semantics=("arbitrary", "arbitrary"),
            vmem_limit_bytes=VMEM_LIMIT_BYTES),
        name="ada_modulation",
    )(c, ada_w, ada_b.reshape(depth, 1, n))


ROPE_ROW_TILE = 1024


def _rope_kernel(pos_ref, invf_ref, *refs):
    n_cast = (len(refs) - 2) // 2
    cos_ref, sin_ref = refs[n_cast:n_cast + 2]
    _cast_slabs(refs[:n_cast], refs[n_cast + 2:])
    pos = pos_ref[0].astype(F32)
    ang = pos * invf_ref[...]
    lane = lax.broadcasted_iota(jnp.int32, ang.shape, 1) & (HEAD_DIM - 1)
    c = jnp.cos(ang)
    s = jnp.sin(ang)
    cos_ref[0] = jnp.where(lane < ROT_DIM, c, 1.0)
    sin_ref[0] = jnp.where(lane < ROT_DIM // 2, -s, jnp.where(lane < ROT_DIM, s, 0.0))


def _rope_tables(positions, cast_weights):
    b, s = positions.shape
    half = ROT_DIM // 2
    inv_freq = ROPE_THETA ** (-jnp.arange(0, ROT_DIM, 2, dtype=F32) / ROT_DIM)
    lane = jnp.arange(LANES) % HEAD_DIM
    invf_lane = jnp.where(lane < ROT_DIM, inv_freq[lane % half], 0.0).reshape(1, LANES)
    tab = jax.ShapeDtypeStruct((b, s, LANES), F32)
    grid = (b, s // ROPE_ROW_TILE)
    cast_in, cast_out, cast_shapes = _cast_specs(cast_weights, 0, grid)
    return pl.pallas_call(
        _rope_kernel,
        grid=grid,
        in_specs=[
            pl.BlockSpec((1, ROPE_ROW_TILE, 1), lambda i, j: (i, j, 0)),
            pl.BlockSpec((1, LANES), lambda i, j: (0, 0)),
        ] + cast_in,
        out_specs=[pl.BlockSpec((1, ROPE_ROW_TILE, LANES), lambda i, j: (i, j, 0))] * 2 + cast_out,
        out_shape=[tab, tab] + cast_shapes,
        compiler_params=pltpu.CompilerParams(
            dimension_semantics=("arbitrary", "arbitrary"),
            vmem_limit_bytes=VMEM_LIMIT_BYTES),
        name="rope_tables",
    )(positions.reshape(b, s, 1), invf_lane, *cast_weights)


MIX_N_INPUTS = 14


def _mix_kernel(*refs, n_cast):
    (x_ref, mod_ref, n1w_ref, cos_ref, sin_ref, win_ref, sink_ref, lnw_ref, lnb_ref,
     sguw_ref, sgub_ref, pa_ref, pb_ref, wo_ref) = refs[:MIX_N_INPUTS]
    o_ref = refs[MIX_N_INPUTS + n_cast]
    q_s, k4_s, v4_s, bias_s, yattn_s, ysgu_s = refs[MIX_N_INPUTS + 2 * n_cast + 1:]
    _cast_slabs(refs[MIX_N_INPUTS:MIX_N_INPUTS + n_cast],
                refs[MIX_N_INPUTS + n_cast + 1:MIX_N_INPUTS + 2 * n_cast + 1])
    tm = TOKEN_TILE
    nblk = tm // ATTN_BLOCK
    band_keys = 2 * ATTN_BLOCK
    j = pl.program_id(1)

    @pl.when(j == 0)
    def _():
        k4_s[:, 0:ATTN_BLOCK, :] = jnp.zeros((4, ATTN_BLOCK, LANES), BF16)
        v4_s[:, 0:ATTN_BLOCK, :] = jnp.zeros((4, ATTN_BLOCK, LANES), BF16)

    scale1 = 1.0 + mod_ref[0, 1:2, :]
    hb_chunks, q_chunks = [], []
    for r in range(tm // NORM_ROW_CHUNK):
        xr = x_ref[0, r * NORM_ROW_CHUNK:(r + 1) * NORM_ROW_CHUNK, :]
        hr = (_rms_norm(xr, n1w_ref[...]) * scale1 + mod_ref[0, 0:1, :]).astype(BF16)
        hb_chunks.append(hr)
        q_chunks.append(_dot(hr, win_ref[:, 0:Q_END]))
    hb = jnp.concatenate(hb_chunks, axis=0)
    q = jnp.concatenate(q_chunks, axis=0)
    kv = _dot(hb, win_ref[:, Q_END:V_END])
    z = _dot(hb, win_ref[:, V_END:Z_END])

    cos_t = cos_ref[0]
    sin_t = sin_ref[0]
    lane = lax.broadcasted_iota(jnp.int32, (tm, LANES), 1)
    first_half = (lane & (HEAD_DIM - 1)) < (ROT_DIM // 2)
    left_head = lane < HEAD_DIM

    def rope(t):
        partner = jnp.where(first_half,
                            pltpu.roll(t, LANES - ROT_DIM // 2, 1),
                            pltpu.roll(t, ROT_DIM // 2, 1))
        return t * cos_t + partner * sin_t

    for c in range(Q_END // LANES):
        qc = rope(q[:, c * LANES:(c + 1) * LANES]) * (HEAD_DIM ** -0.5 * LOG2E)
        q_s[:, c * LANES:(c + 1) * LANES] = qc.astype(BF16)

    k = rope(kv[:, 0:LANES])
    v = kv[:, LANES:2 * LANES]
    for src, dst in ((k, k4_s), (v, v4_s)):
        swapped = pltpu.roll(src, HEAD_DIM, 1)
        zero = jnp.zeros_like(src)
        rows = slice(ATTN_BLOCK, ATTN_BLOCK + tm)
        dst[0, rows, :] = jnp.where(left_head, src, zero).astype(BF16)
        dst[1, rows, :] = jnp.where(left_head, zero, swapped).astype(BF16)
        dst[2, rows, :] = jnp.where(left_head, swapped, zero).astype(BF16)
        dst[3, rows, :] = jnp.where(left_head, zero, src).astype(BF16)

    gates = _dot(hb, win_ref[:, Z_END:IN_COLS])

    z = 0.5 * z * (1.0 + lax.erf(z * (2.0 ** -0.5)))
    u = z[:, 0:SGU_WIDTH]
    v_sgu = z[:, SGU_WIDTH:]
    mu = jnp.mean(v_sgu, axis=-1, keepdims=True)
    dv = v_sgu - mu
    var = jnp.mean(dv * dv, axis=-1, keepdims=True)
    vn = (dv * lax.rsqrt(var + NORM_EPS) * lnw_ref[...] + lnb_ref[...]).astype(BF16)
    ti = lax.broadcasted_iota(jnp.int32, (SGU_CHUNK, SGU_CHUNK), 0)
    si = lax.broadcasted_iota(jnp.int32, (SGU_CHUNK, SGU_CHUNK), 1)
    causal = si <= ti
    gdim = SGU_WIDTH // SGU_GROUPS
    nchunk = tm // SGU_CHUNK
    for g in range(SGU_GROUPS):
        cols = slice(g * gdim, (g + 1) * gdim)
        wm = jnp.where(causal, sguw_ref[g], 0.0).astype(BF16)
        rhs = jnp.concatenate(
            [vn[c * SGU_CHUNK:(c + 1) * SGU_CHUNK, cols] for c in range(nchunk)], axis=1)
        f = _dot(wm, rhs)
        for c in range(nchunk):
            rows = slice(c * SGU_CHUNK, (c + 1) * SGU_CHUNK)
            fc = f[:, c * gdim:(c + 1) * gdim] + sgub_ref[:, cols]
            ysgu_s[rows, cols] = (u[rows, cols] * fc).astype(BF16)

    sgu_part = jax.nn.sigmoid(gates[:, 0:D_MODEL]) * _dot(ysgu_s[...], pa_ref[...])
    gate_b = jax.nn.sigmoid(gates[:, D_MODEL:])

    er = lax.broadcasted_iota(jnp.int32, (ATTN_BLOCK, LANES), 0)
    ec = lax.broadcasted_iota(jnp.int32, (ATTN_BLOCK, LANES), 1)
    eye = jnp.where(er == ec, 1.0, 0.0).astype(BF16)
    kr = lax.broadcasted_iota(jnp.int32, (band_keys, ATTN_BLOCK), 0)
    qc_i = lax.broadcasted_iota(jnp.int32, (band_keys, ATTN_BLOCK), 1)
    band_t = (kr > qc_i) & (kr <= qc_i + ATTN_BLOCK)
    for idx, ok in ((0, band_t & (kr >= ATTN_BLOCK)), (1, band_t)):
        bias_t = jnp.where(ok, 0.0, MASK_VALUE).astype(BF16)
        bias_s[idx, 0:band_keys, :] = bias_t
        bias_s[idx, band_keys:2 * band_keys, :] = bias_t
    sr = lax.broadcasted_iota(jnp.int32, (2 * band_keys, LANES), 0)
    sc = lax.broadcasted_iota(jnp.int32, (2 * band_keys, LANES), 1)
    ones_sel = jnp.where((sr < band_keys) == (sc < HEAD_DIM), 1.0, 0.0).astype(BF16)
    sink_value_row = (sr == 0) | (sr == band_keys)
    sink_lane = ec == 0
    contract_last = (((1,), (1,)), ((), ()))

    def score_matmuls(b):
        r0 = b * ATTN_BLOCK
        bias_t = bias_s[jnp.where(j > 0, 1, 0)] if b == 0 else bias_s[1]
        scores = []
        value_ops = []
        for g in range(N_KV_HEADS):
            kbd = jnp.concatenate([k4_s[2 * g, pl.ds(r0, band_keys), :],
                                   k4_s[2 * g + 1, pl.ds(r0, band_keys), :]], axis=0)
            vbd = jnp.concatenate([v4_s[2 * g, pl.ds(r0, band_keys), :],
                                   v4_s[2 * g + 1, pl.ds(r0, band_keys), :]], axis=0)
            kbd = jnp.concatenate([kbd, bias_t], axis=1)
            vbd = jnp.where(sink_value_row, jnp.zeros_like(vbd), vbd)
            value_ops.append(jnp.concatenate([vbd, ones_sel], axis=1))
            qrows = []
            for pr in range(Q_PER_KV // 2):
                col = (g * Q_PER_KV + 2 * pr) * HEAD_DIM
                qrows.append(jnp.concatenate([q_s[pl.ds(r0, ATTN_BLOCK), col:col + LANES], eye], axis=1))
            scores.append(lax.dot_general(jnp.concatenate(qrows, axis=0), kbd, contract_last,
                                          preferred_element_type=F32))
        return scores, value_ops

    def softmax_value_matmuls(b, scores, value_ops):
        r0 = b * ATTN_BLOCK
        for g in range(N_KV_HEADS):
            probs = []
            for pr in range(Q_PER_KV // 2):
                rows = slice(pr * ATTN_BLOCK, (pr + 1) * ATTN_BLOCK)
                pair_probs = []
                for hh in range(2):
                    c0 = hh * band_keys
                    sink = sink_ref[g * Q_PER_KV + 2 * pr + hh] * LOG2E
                    first = jnp.where(sink_lane, sink, scores[g][rows, c0:c0 + LANES])
                    rest = scores[g][rows, c0 + LANES:c0 + band_keys]
                    m = jnp.max(jnp.maximum(first, rest), axis=-1, keepdims=True)
                    pair_probs += [jnp.exp2(first - m).astype(BF16), jnp.exp2(rest - m).astype(BF16)]
                probs.append(jnp.concatenate(pair_probs, axis=1))
            o = _dot(jnp.concatenate(probs, axis=0), value_ops[g])
            for pr in range(Q_PER_KV // 2):
                rows = slice(pr * ATTN_BLOCK, (pr + 1) * ATTN_BLOCK)
                col = (g * Q_PER_KV + 2 * pr) * HEAD_DIM
                yattn_s[pl.ds(r0, ATTN_BLOCK), col:col + LANES] = (
                    o[rows, 0:LANES] / o[rows, LANES:]).astype(BF16)

    pending = score_matmuls(0)
    for b in range(nblk):
        upcoming = score_matmuls(b + 1) if b + 1 < nblk else None
        softmax_value_matmuls(b, *pending)
        pending = upcoming

    k4_s[:, 0:ATTN_BLOCK, :] = k4_s[:, tm:tm + ATTN_BLOCK, :]
    v4_s[:, 0:ATTN_BLOCK, :] = v4_s[:, tm:tm + ATTN_BLOCK, :]

    merged = (sgu_part + gate_b * _dot(yattn_s[...], pb_ref[...])).astype(BF16)
    for r in range(tm // OUT_ROW_CHUNK):
        rows = slice(r * OUT_ROW_CHUNK, (r + 1) * OUT_ROW_CHUNK)
        o_ref[0, rows, :] = x_ref[0, rows, :] + mod_ref[0, 2:3, :] * _dot(merged[rows, :], wo_ref[...])


def _full(shape):
    zeros = (0,) * len(shape)
    return pl.BlockSpec(shape, lambda i, j: zeros)


def _mix_layer(x, mod, n1w, cos_tab, sin_tab, w_in, sinks, ln_w, ln_b, sgu_w, sgu_b_full,
               proj_a, proj_b, w_out, cast_weights, layer):
    b, s, d = x.shape
    tm = TOKEN_TILE
    tile = pl.BlockSpec((1, tm, d), lambda i, j: (i, j, 0))
    tab = pl.BlockSpec((1, tm, LANES), lambda i, j: (i, j, 0))
    grid = (b, s // tm)
    cast_in, cast_out, cast_shapes = _cast_specs(cast_weights, layer, grid)
    return pl.pallas_call(
        functools.partial(_mix_kernel, n_cast=len(cast_weights)),
        grid=grid,
        in_specs=[
            tile,
            pl.BlockSpec((1, 6, d), lambda i, j: (i, 0, 0)),
            _full((1, d)),
            tab, tab,
            _full((d, IN_COLS)),
            pl.BlockSpec(memory_space=pltpu.SMEM),
            _full((1, SGU_WIDTH)), _full((1, SGU_WIDTH)),
            _full((SGU_GROUPS, SGU_CHUNK, SGU_CHUNK)),
            _full((SGU_CHUNK, SGU_WIDTH)),
            _full((SGU_WIDTH, d)), _full((Q_END, d)), _full((d, d)),
        ] + cast_in,
        out_specs=[tile] + cast_out,
        out_shape=[jax.ShapeDtypeStruct(x.shape, x.dtype)] + cast_shapes,
        scratch_shapes=[
            pltpu.VMEM((tm, Q_END), BF16),
            pltpu.VMEM((4, tm + ATTN_BLOCK, LANES), BF16),
            pltpu.VMEM((4, tm + ATTN_BLOCK, LANES), BF16),
            pltpu.VMEM((2, 4 * ATTN_BLOCK, LANES), BF16),
            pltpu.VMEM((tm, Q_END), BF16),
            pltpu.VMEM((tm, SGU_WIDTH), BF16),
        ],
        compiler_params=pltpu.CompilerParams(
            dimension_semantics=("arbitrary", "arbitrary"),
            vmem_limit_bytes=VMEM_LIMIT_BYTES),
        name="token_mixer",
    )(x, mod, n1w, cos_tab, sin_tab, w_in, sinks, ln_w, ln_b, sgu_w, sgu_b_full,
      proj_a, proj_b, w_out, *cast_weights)


CONV_PAD_ROWS = SUBLANES


FFN_N_INPUTS = 9


def _ffn_kernel(*refs, n_cast, final_norm):
    (x_ref, mod_ref, n2w_ref, wg_ref, wu_ref, cw_ref, cb_ref, wd_ref, fnw_ref) = refs[:FFN_N_INPUTS]
    o_ref = refs[FFN_N_INPUTS + n_cast]
    a_s, act_s = refs[FFN_N_INPUTS + 2 * n_cast + 1:]
    _cast_slabs(refs[FFN_N_INPUTS:FFN_N_INPUTS + n_cast],
                refs[FFN_N_INPUTS + n_cast + 1:FFN_N_INPUTS + 2 * n_cast + 1])
    tm = TOKEN_TILE
    j = pl.program_id(1)

    @pl.when(j == 0)
    def _():
        a_s[0:CONV_PAD_ROWS, :] = jnp.zeros((CONV_PAD_ROWS, FFN_DIM), F32)

    scale2 = 1.0 + mod_ref[0, 4:5, :]
    hb_chunks, a_chunks, up_chunks = [], [], []
    for r in range(tm // NORM_ROW_CHUNK):
        xr = x_ref[0, r * NORM_ROW_CHUNK:(r + 1) * NORM_ROW_CHUNK, :]
        hr = (_rms_norm(xr, n2w_ref[...]) * scale2 + mod_ref[0, 3:4, :]).astype(BF16)
        hb_chunks.append(hr)
        a_chunks.append(_dot(hr, wg_ref[:, 0:MXU_COLS]))
        up_chunks.append(_dot(hr, wu_ref[:, 0:MXU_COLS]))
    hb = jnp.concatenate(hb_chunks, axis=0)
    first_a = jnp.concatenate(a_chunks, axis=0)
    first_up = jnp.concatenate(up_chunks, axis=0)

    for c in range(FFN_DIM // MXU_COLS):
        cols = slice(c * MXU_COLS, (c + 1) * MXU_COLS)
        a = first_a if c == 0 else _dot(hb, wg_ref[:, cols])
        a_s[CONV_PAD_ROWS:CONV_PAD_ROWS + tm, cols] = a
        conv = cb_ref[:, cols] + cw_ref[2:3, cols] * a
        for tap in range(CONV_WIDTH - 1):
            back = CONV_WIDTH - 1 - tap
            conv = conv + cw_ref[tap:tap + 1, cols] * a_s[CONV_PAD_ROWS - back:CONV_PAD_ROWS - back + tm, cols]
        up = first_up if c == 0 else _dot(hb, wu_ref[:, cols])
        act_s[:, cols] = (conv * jax.nn.sigmoid(conv) * up).astype(BF16)

    a_s[0:CONV_PAD_ROWS, :] = a_s[tm:tm + CONV_PAD_ROWS, :]

    for r in range(tm // OUT_ROW_CHUNK):
        rows = slice(r * OUT_ROW_CHUNK, (r + 1) * OUT_ROW_CHUNK)
        y = x_ref[0, rows, :] + mod_ref[0, 5:6, :] * _dot(act_s[rows, :], wd_ref[...])
        if final_norm:
            y = _rms_norm(y, fnw_ref[...])
        o_ref[0, rows, :] = y


def _ffn_layer(x, mod, n2w, w_gate, w_up, conv_w, conv_b, w_down, final_w, final_norm,
               cast_weights, layer):
    b, s, d = x.shape
    tm = TOKEN_TILE
    tile = pl.BlockSpec((1, tm, d), lambda i, j: (i, j, 0))
    grid = (b, s // tm)
    cast_in, cast_out, cast_shapes = _cast_specs(cast_weights, layer, grid)
    return pl.pallas_call(
        functools.partial(_ffn_kernel, n_cast=len(cast_weights), final_norm=final_norm),
        grid=grid,
        in_specs=[
            tile,
            pl.BlockSpec((1, 6, d), lambda i, j: (i, 0, 0)),
            _full((1, d)),
            _full((d, FFN_DIM)), _full((d, FFN_DIM)),
            _full((CONV_WIDTH, FFN_DIM)), _full((1, FFN_DIM)),
            _full((FFN_DIM, d)),
            _full((1, d)),
        ] + cast_in,
        out_specs=[tile] + cast_out,
        out_shape=[jax.ShapeDtypeStruct(x.shape, x.dtype)] + cast_shapes,
        scratch_shapes=[
            pltpu.VMEM((CONV_PAD_ROWS + tm, FFN_DIM), F32),
            pltpu.VMEM((tm, FFN_DIM), BF16),
        ],
        compiler_params=pltpu.CompilerParams(
            dimension_semantics=("arbitrary", "arbitrary"),
            vmem_limit_bytes=VMEM_LIMIT_BYTES),
        name="conv_ffn",
    )(x, mod, n2w, w_gate, w_up, conv_w, conv_b, w_down, final_w, *cast_weights)


def kernel(x, c, positions, ada_w, ada_b, norm1_w, w_in, attn_sinks, sgu_ln_w, sgu_ln_b, sgu_w, sgu_b,
           proj_a, proj_b, w_out, norm2_w, ffn_w_gate, ffn_w_up, ffn_conv_w, ffn_conv_b, ffn_w_down,
           final_norm_w):
    batch, seq, d = x.shape
    depth = ada_w.shape[0]
    assert seq % TOKEN_TILE == 0 and TOKEN_TILE % ATTN_BLOCK == 0 and d == D_MODEL

    rows = -(-batch // SUBLANES) * SUBLANES
    c_pad = jnp.pad(c, ((0, rows - batch), (0, 0)))
    mod = _ada_modulation(c_pad, ada_w, ada_b)[:, :batch]
    mix_f32 = (w_in, proj_a, proj_b, w_out)
    ffn_f32 = (ffn_w_gate, ffn_w_up, ffn_w_down)
    cos_tab, sin_tab, *mix_w = _rope_tables(positions, mix_f32)

    for l in range(depth):
        mod_l = mod[l].reshape(batch, 6, d)
        sgu_b_full = jnp.repeat(sgu_b[l].T, SGU_WIDTH // SGU_GROUPS, axis=1)
        x, *ffn_w = _mix_layer(
            x, mod_l, norm1_w[l].reshape(1, d), cos_tab, sin_tab, mix_w[0],
            attn_sinks[l], sgu_ln_w[l].reshape(1, -1), sgu_ln_b[l].reshape(1, -1),
            sgu_w[l], sgu_b_full, mix_w[1], mix_w[2], mix_w[3], ffn_f32, l)
        last = l == depth - 1
        x, *mix_w = _ffn_layer(
            x, mod_l, norm2_w[l].reshape(1, d), ffn_w[0], ffn_w[1],
            ffn_conv_w[l], ffn_conv_b[l].reshape(1, -1), ffn_w[2],
            final_norm_w.reshape(1, d), last, () if last else mix_f32, l + 1)
    return x
```

```python
import functools

import jax
import jax.numpy as jnp
from jax import lax
from jax.experimental import pallas as pl
from jax.experimental.pallas import tpu as pltpu

D_MODEL = 1024
N_Q_HEADS = 16
N_KV_HEADS = 2
HEAD_DIM = 64
Q_PER_KV = N_Q_HEADS // N_KV_HEADS
ATTN_BLOCK = 128
ROPE_THETA = 500000.0
ROT_DIM = HEAD_DIM // 4
SGU_WIDTH = 1024
SGU_GROUPS = 8
SGU_CHUNK = 128
FFN_DIM = 2816
CONV_WIDTH = 3
NORM_EPS = 1e-6

Q_END = N_Q_HEADS * HEAD_DIM
K_END = Q_END + N_KV_HEADS * HEAD_DIM
V_END = K_END + N_KV_HEADS * HEAD_DIM
Z_END = V_END + 2 * SGU_WIDTH
IN_COLS = Z_END + 2 * D_MODEL

LANES = 128
SUBLANES = 8
MXU_COLS = 256
TOKEN_TILE = 512
NORM_ROW_CHUNK = 128
OUT_ROW_CHUNK = 256
VMEM_LIMIT_BYTES = 56 * 1024 * 1024
MASK_VALUE = -1e30
LOG2E = 1.4426950408889634

F32 = jnp.float32
BF16 = jnp.bfloat16


def _rms_norm(x, w):
    ms = jnp.mean(x * x, axis=-1, keepdims=True)
    return x * lax.rsqrt(ms + NORM_EPS) * w


def _dot(a, b):
    return jnp.dot(a, b, preferred_element_type=F32)


BF16_ROW_TILE = 16


def _cast_specs(stacked_weights, layer, grid):
    steps = grid[0] * grid[1]
    specs, shapes = [], []
    for w in stacked_weights:
        _, rows, cols = w.shape
        slab, share = rows // steps, 1
        while slab % BF16_ROW_TILE:
            slab, share = slab * 2, share * 2
        assert rows % slab == 0 and steps % share == 0
        idx = lambda i, j, share=share: ((i * grid[1] + j) // share, 0)
        specs.append((pl.BlockSpec((None, slab, cols), lambda i, j, idx=idx: (layer,) + idx(i, j)),
                      pl.BlockSpec((slab, cols), idx)))
        shapes.append(jax.ShapeDtypeStruct((rows, cols), BF16))
    return [s[0] for s in specs], [s[1] for s in specs], shapes


def _cast_slabs(src_refs, dst_refs):
    for src, dst in zip(src_refs, dst_refs):
        dst[...] = src[...].astype(BF16)


ROPE_ROW_TILE = 1024


def _split_bf16(v):
    hi = v.astype(BF16)
    return hi, (v - hi.astype(F32)).astype(BF16)


def _prep_kernel(c_ref, adaw_ref, adab_ref, pos_ref, invf_ref, *refs):
    n_cast = (len(refs) - 3) // 2
    mod_ref, cos_ref, sin_ref = refs[n_cast:n_cast + 3]
    _cast_slabs(refs[:n_cast], refs[n_cast + 3:])

    c = c_ref[...]
    rows = c.shape[0]
    c_hi, c_lo = _split_bf16(c * jax.nn.sigmoid(c))
    lhs = jnp.concatenate([c_hi, c_lo], axis=0)
    w_hi, w_lo = _split_bf16(adaw_ref[...])
    parts = _dot(lhs, w_hi) + _dot(lhs, w_lo)
    mod_ref[...] = parts[0:rows] + parts[rows:2 * rows] + adab_ref[...]

    pos = pos_ref[0].astype(F32)
    ang = pos * invf_ref[...]
    lane = lax.broadcasted_iota(jnp.int32, ang.shape, 1) & (HEAD_DIM - 1)
    cos_ref[0] = jnp.where(lane < ROT_DIM, jnp.cos(ang), 1.0)
    s = jnp.sin(ang)
    sin_ref[0] = jnp.where(lane < ROT_DIM // 2, -s, jnp.where(lane < ROT_DIM, s, 0.0))


def _prepare(c, ada_w, ada_b, positions, cast_weights):
    b, s = positions.shape
    depth, d, n = ada_w.shape
    rows = c.shape[0]
    half = ROT_DIM // 2
    inv_freq = ROPE_THETA ** (-jnp.arange(0, ROT_DIM, 2, dtype=F32) / ROT_DIM)
    lane = jnp.arange(LANES) % HEAD_DIM
    invf_lane = jnp.where(lane < ROT_DIM, inv_freq[lane % half], 0.0).reshape(1, LANES)
    tab = jax.ShapeDtypeStruct((b, s, LANES), F32)
    grid = (b, s // ROPE_ROW_TILE)
    steps = grid[0] * grid[1]
    assert steps % depth == 0 and n % (steps // depth) == 0
    col_tiles = steps // depth
    ada_cols = n // col_tiles
    ada_idx = lambda i, j: ((i * grid[1] + j) // col_tiles, 0, (i * grid[1] + j) % col_tiles)
    cast_in, cast_out, cast_shapes = _cast_specs(cast_weights, 0, grid)
    return pl.pallas_call(
        _prep_kernel,
        grid=grid,
        in_specs=[
            pl.BlockSpec((rows, d), lambda i, j: (0, 0)),
            pl.BlockSpec((None, d, ada_cols), ada_idx),
            pl.BlockSpec((None, 1, ada_cols), ada_idx),
            pl.BlockSpec((1, ROPE_ROW_TILE, 1), lambda i, j: (i, j, 0)),
            pl.BlockSpec((1, LANES), lambda i, j: (0, 0)),
        ] + cast_in,
        out_specs=[pl.BlockSpec((None, rows, ada_cols), ada_idx)]
        + [pl.BlockSpec((1, ROPE_ROW_TILE, LANES), lambda i, j: (i, j, 0))] * 2 + cast_out,
        out_shape=[jax.ShapeDtypeStruct((depth, rows, n), F32), tab, tab] + cast_shapes,
        compiler_params=pltpu.CompilerParams(
            dimension_semantics=("arbitrary", "arbitrary"),
            vmem_limit_bytes=VMEM_LIMIT_BYTES),
        name="prepare",
    )(c, ada_w, ada_b.reshape(depth, 1, n), positions.reshape(b, s, 1), invf_lane, *cast_weights)


MIX_N_INPUTS = 14


def _mix_kernel(*refs, n_cast):
    (x_ref, mod_ref, n1w_ref, cos_ref, sin_ref, win_ref, sink_ref, lnw_ref, lnb_ref,
     sguw_ref, sgub_ref, pa_ref, pb_ref, wo_ref) = refs[:MIX_N_INPUTS]
    o_ref = refs[MIX_N_INPUTS + n_cast]
    q_s, k4_s, v4_s, bias_s, yattn_s, ysgu_s = refs[MIX_N_INPUTS + 2 * n_cast + 1:]
    _cast_slabs(refs[MIX_N_INPUTS:MIX_N_INPUTS + n_cast],
                refs[MIX_N_INPUTS + n_cast + 1:MIX_N_INPUTS + 2 * n_cast + 1])
    tm = TOKEN_TILE
    nblk = tm // ATTN_BLOCK
    band_keys = 2 * ATTN_BLOCK
    j = pl.program_id(1)

    @pl.when(j == 0)
    def _():
        k4_s[:, 0:ATTN_BLOCK, :] = jnp.zeros((4, ATTN_BLOCK, LANES), BF16)
        v4_s[:, 0:ATTN_BLOCK, :] = jnp.zeros((4, ATTN_BLOCK, LANES), BF16)

    scale1 = 1.0 + mod_ref[0, 1:2, :]
    hb_chunks, q_chunks = [], []
    for r in range(tm // NORM_ROW_CHUNK):
        xr = x_ref[0, r * NORM_ROW_CHUNK:(r + 1) * NORM_ROW_CHUNK, :]
        hr = (_rms_norm(xr, n1w_ref[...]) * scale1 + mod_ref[0, 0:1, :]).astype(BF16)
        hb_chunks.append(hr)
        q_chunks.append(_dot(hr, win_ref[:, 0:Q_END]))
    hb = jnp.concatenate(hb_chunks, axis=0)
    q = jnp.concatenate(q_chunks, axis=0)
    kv = _dot(hb, win_ref[:, Q_END:V_END])
    z = _dot(hb, win_ref[:, V_END:Z_END])

    cos_t = cos_ref[0]
    sin_t = sin_ref[0]
    lane = lax.broadcasted_iota(jnp.int32, (tm, LANES), 1)
    first_half = (lane & (HEAD_DIM - 1)) < (ROT_DIM // 2)
    left_head = lane < HEAD_DIM

    def rope(t):
        partner = jnp.where(first_half,
                            pltpu.roll(t, LANES - ROT_DIM // 2, 1),
                            pltpu.roll(t, ROT_DIM // 2, 1))
        return t * cos_t + partner * sin_t

    for c in range(Q_END // LANES):
        qc = rope(q[:, c * LANES:(c + 1) * LANES]) * (HEAD_DIM ** -0.5 * LOG2E)
        q_s[:, c * LANES:(c + 1) * LANES] = qc.astype(BF16)

    k = rope(kv[:, 0:LANES])
    v = kv[:, LANES:2 * LANES]
    for src, dst in ((k, k4_s), (v, v4_s)):
        swapped = pltpu.roll(src, HEAD_DIM, 1)
        zero = jnp.zeros_like(src)
        rows = slice(ATTN_BLOCK, ATTN_BLOCK + tm)
        dst[0, rows, :] = jnp.where(left_head, src, zero).astype(BF16)
        dst[1, rows, :] = jnp.where(left_head, zero, swapped).astype(BF16)
        dst[2, rows, :] = jnp.where(left_head, swapped, zero).astype(BF16)
        dst[3, rows, :] = jnp.where(left_head, zero, src).astype(BF16)

    gates = _dot(hb, win_ref[:, Z_END:IN_COLS])

    z = 0.5 * z * (1.0 + lax.erf(z * (2.0 ** -0.5)))
    u = z[:, 0:SGU_WIDTH]
    v_sgu = z[:, SGU_WIDTH:]
    mu = jnp.mean(v_sgu, axis=-1, keepdims=True)
    dv = v_sgu - mu
    var = jnp.mean(dv * dv, axis=-1, keepdims=True)
    vn = (dv * lax.rsqrt(var + NORM_EPS) * lnw_ref[...] + lnb_ref[...]).astype(BF16)
    ti = lax.broadcasted_iota(jnp.int32, (SGU_CHUNK, SGU_CHUNK), 0)
    si = lax.broadcasted_iota(jnp.int32, (SGU_CHUNK, SGU_CHUNK), 1)
    causal = si <= ti
    gdim = SGU_WIDTH // SGU_GROUPS
    nchunk = tm // SGU_CHUNK
    for g in range(SGU_GROUPS):
        cols = slice(g * gdim, (g + 1) * gdim)
        wm = jnp.where(causal, sguw_ref[g], 0.0).astype(BF16)
        rhs = jnp.concatenate(
            [vn[c * SGU_CHUNK:(c + 1) * SGU_CHUNK, cols] for c in range(nchunk)], axis=1)
        f = _dot(wm, rhs)
        for c in range(nchunk):
            rows = slice(c * SGU_CHUNK, (c + 1) * SGU_CHUNK)
            fc = f[:, c * gdim:(c + 1) * gdim] + sgub_ref[:, cols]
            ysgu_s[rows, cols] = (u[rows, cols] * fc).astype(BF16)

    sgu_part = jax.nn.sigmoid(gates[:, 0:D_MODEL]) * _dot(ysgu_s[...], pa_ref[...])
    gate_b = jax.nn.sigmoid(gates[:, D_MODEL:])

    er = lax.broadcasted_iota(jnp.int32, (ATTN_BLOCK, LANES), 0)
    ec = lax.broadcasted_iota(jnp.int32, (ATTN_BLOCK, LANES), 1)
    eye = jnp.where(er == ec, 1.0, 0.0).astype(BF16)
    kr = lax.broadcasted_iota(jnp.int32, (band_keys, ATTN_BLOCK), 0)
    qc_i = lax.broadcasted_iota(jnp.int32, (band_keys, ATTN_BLOCK), 1)
    band_t = (kr > qc_i) & (kr <= qc_i + ATTN_BLOCK)
    for idx, ok in ((0, band_t & (kr >= ATTN_BLOCK)), (1, band_t)):
        bias_t = jnp.where(ok, 0.0, MASK_VALUE).astype(BF16)
        bias_s[idx, 0:band_keys, :] = bias_t
        bias_s[idx, band_keys:2 * band_keys, :] = bias_t
    sr = lax.broadcasted_iota(jnp.int32, (2 * band_keys, LANES), 0)
    sc = lax.broadcasted_iota(jnp.int32, (2 * band_keys, LANES), 1)
    ones_sel = jnp.where((sr < band_keys) == (sc < HEAD_DIM), 1.0, 0.0).astype(BF16)
    sink_value_row = (sr == 0) | (sr == band_keys)
    sink_lane = ec == 0
    contract_last = (((1,), (1,)), ((), ()))

    def score_matmuls(b):
        r0 = b * ATTN_BLOCK
        bias_t = bias_s[jnp.where(j > 0, 1, 0)] if b == 0 else bias_s[1]
        scores = []
        value_ops = []
        for g in range(N_KV_HEADS):
            kbd = jnp.concatenate([k4_s[2 * g, pl.ds(r0, band_keys), :],
                                   k4_s[2 * g + 1, pl.ds(r0, band_keys), :]], axis=0)
            vbd = jnp.concatenate([v4_s[2 * g, pl.ds(r0, band_keys), :],
                                   v4_s[2 * g + 1, pl.ds(r0, band_keys), :]], axis=0)
            kbd = jnp.concatenate([kbd, bias_t], axis=1)
            vbd = jnp.where(sink_value_row, jnp.zeros_like(vbd), vbd)
            value_ops.append(jnp.concatenate([vbd, ones_sel], axis=1))
            qrows = []
            for pr in range(Q_PER_KV // 2):
                col = (g * Q_PER_KV + 2 * pr) * HEAD_DIM
                qrows.append(jnp.concatenate([q_s[pl.ds(r0, ATTN_BLOCK), col:col + LANES], eye], axis=1))
            scores.append(lax.dot_general(jnp.concatenate(qrows, axis=0), kbd, contract_last,
                                          preferred_element_type=F32))
        return scores, value_ops

    def softmax_value_matmuls(b, scores, value_ops):
        r0 = b * ATTN_BLOCK
        for g in range(N_KV_HEADS):
            probs = []
            for pr in range(Q_PER_KV // 2):
                rows = slice(pr * ATTN_BLOCK, (pr + 1) * ATTN_BLOCK)
                pair_probs = []
                for hh in range(2):
                    c0 = hh * band_keys
                    sink = sink_ref[g * Q_PER_KV + 2 * pr + hh] * LOG2E
                    first = jnp.where(sink_lane, sink, scores[g][rows, c0:c0 + LANES])
                    rest = scores[g][rows, c0 + LANES:c0 + band_keys]
                    m = jnp.max(jnp.maximum(first, rest), axis=-1, keepdims=True)
                    pair_probs += [jnp.exp2(first - m).astype(BF16), jnp.exp2(rest - m).astype(BF16)]
                probs.append(jnp.concatenate(pair_probs, axis=1))
            o = _dot(jnp.concatenate(probs, axis=0), value_ops[g])
            for pr in range(Q_PER_KV // 2):
                rows = slice(pr * ATTN_BLOCK, (pr + 1) * ATTN_BLOCK)
                col = (g * Q_PER_KV + 2 * pr) * HEAD_DIM
                yattn_s[pl.ds(r0, ATTN_BLOCK), col:col + LANES] = (
                    o[rows, 0:LANES] / o[rows, LANES:]).astype(BF16)

    pending = score_matmuls(0)
    for b in range(nblk):
        upcoming = score_matmuls(b + 1) if b + 1 < nblk else None
        softmax_value_matmuls(b, *pending)
        pending = upcoming

    k4_s[:, 0:ATTN_BLOCK, :] = k4_s[:, tm:tm + ATTN_BLOCK, :]
    v4_s[:, 0:ATTN_BLOCK, :] = v4_s[:, tm:tm + ATTN_BLOCK, :]

    merged = (sgu_part + gate_b * _dot(yattn_s[...], pb_ref[...])).astype(BF16)
    for r in range(tm // OUT_ROW_CHUNK):
        rows = slice(r * OUT_ROW_CHUNK, (r + 1) * OUT_ROW_CHUNK)
        o_ref[0, rows, :] = x_ref[0, rows, :] + mod_ref[0, 2:3, :] * _dot(merged[rows, :], wo_ref[...])


def _full(shape):
    zeros = (0,) * len(shape)
    return pl.BlockSpec(shape, lambda i, j: zeros)


def _mix_layer(x, mod, n1w, cos_tab, sin_tab, w_in, sinks, ln_w, ln_b, sgu_w, sgu_b_full,
               proj_a, proj_b, w_out, cast_weights, layer):
    b, s, d = x.shape
    tm = TOKEN_TILE
    tile = pl.BlockSpec((1, tm, d), lambda i, j: (i, j, 0))
    tab = pl.BlockSpec((1, tm, LANES), lambda i, j: (i, j, 0))
    grid = (b, s // tm)
    cast_in, cast_out, cast_shapes = _cast_specs(cast_weights, layer, grid)
    return pl.pallas_call(
        functools.partial(_mix_kernel, n_cast=len(cast_weights)),
        grid=grid,
        in_specs=[
            tile,
            pl.BlockSpec((1, 6, d), lambda i, j: (i, 0, 0)),
            _full((1, d)),
            tab, tab,
            _full((d, IN_COLS)),
            pl.BlockSpec(memory_space=pltpu.SMEM),
            _full((1, SGU_WIDTH)), _full((1, SGU_WIDTH)),
            _full((SGU_GROUPS, SGU_CHUNK, SGU_CHUNK)),
            _full((SGU_CHUNK, SGU_WIDTH)),
            _full((SGU_WIDTH, d)), _full((Q_END, d)), _full((d, d)),
        ] + cast_in,
        out_specs=[tile] + cast_out,
        out_shape=[jax.ShapeDtypeStruct(x.shape, x.dtype)] + cast_shapes,
        scratch_shapes=[
            pltpu.VMEM((tm, Q_END), BF16),
            pltpu.VMEM((4, tm + ATTN_BLOCK, LANES), BF16),
            pltpu.VMEM((4, tm + ATTN_BLOCK, LANES), BF16),
            pltpu.VMEM((2, 4 * ATTN_BLOCK, LANES), BF16),
            pltpu.VMEM((tm, Q_END), BF16),
            pltpu.VMEM((tm, SGU_WIDTH), BF16),
        ],
        compiler_params=pltpu.CompilerParams(
            dimension_semantics=("arbitrary", "arbitrary"),
            vmem_limit_bytes=VMEM_LIMIT_BYTES),
        name="token_mixer",
    )(x, mod, n1w, cos_tab, sin_tab, w_in, sinks, ln_w, ln_b, sgu_w, sgu_b_full,
      proj_a, proj_b, w_out, *cast_weights)


CONV_PAD_ROWS = SUBLANES


FFN_N_INPUTS = 9


def _ffn_kernel(*refs, n_cast, final_norm):
    (x_ref, mod_ref, n2w_ref, wg_ref, wu_ref, cw_ref, cb_ref, wd_ref, fnw_ref) = refs[:FFN_N_INPUTS]
    o_ref = refs[FFN_N_INPUTS + n_cast]
    a_s, act_s = refs[FFN_N_INPUTS + 2 * n_cast + 1:]
    _cast_slabs(refs[FFN_N_INPUTS:FFN_N_INPUTS + n_cast],
                refs[FFN_N_INPUTS + n_cast + 1:FFN_N_INPUTS + 2 * n_cast + 1])
    tm = TOKEN_TILE
    j = pl.program_id(1)

    @pl.when(j == 0)
    def _():
        a_s[0:CONV_PAD_ROWS, :] = jnp.zeros((CONV_PAD_ROWS, FFN_DIM), F32)

    scale2 = 1.0 + mod_ref[0, 4:5, :]
    hb_chunks, a_chunks, up_chunks = [], [], []
    for r in range(tm // NORM_ROW_CHUNK):
        xr = x_ref[0, r * NORM_ROW_CHUNK:(r + 1) * NORM_ROW_CHUNK, :]
        hr = (_rms_norm(xr, n2w_ref[...]) * scale2 + mod_ref[0, 3:4, :]).astype(BF16)
        hb_chunks.append(hr)
        a_chunks.append(_dot(hr, wg_ref[:, 0:MXU_COLS]))
        up_chunks.append(_dot(hr, wu_ref[:, 0:MXU_COLS]))
    hb = jnp.concatenate(hb_chunks, axis=0)
    first_a = jnp.concatenate(a_chunks, axis=0)
    first_up = jnp.concatenate(up_chunks, axis=0)

    for c in range(FFN_DIM // MXU_COLS):
        cols = slice(c * MXU_COLS, (c + 1) * MXU_COLS)
        a = first_a if c == 0 else _dot(hb, wg_ref[:, cols])
        a_s[CONV_PAD_ROWS:CONV_PAD_ROWS + tm, cols] = a
        conv = cb_ref[:, cols] + cw_ref[2:3, cols] * a
        for tap in range(CONV_WIDTH - 1):
            back = CONV_WIDTH - 1 - tap
            conv = conv + cw_ref[tap:tap + 1, cols] * a_s[CONV_PAD_ROWS - back:CONV_PAD_ROWS - back + tm, cols]
        up = first_up if c == 0 else _dot(hb, wu_ref[:, cols])
        act_s[:, cols] = (conv * jax.nn.sigmoid(conv) * up).astype(BF16)

    a_s[0:CONV_PAD_ROWS, :] = a_s[tm:tm + CONV_PAD_ROWS, :]

    for r in range(tm // OUT_ROW_CHUNK):
        rows = slice(r * OUT_ROW_CHUNK, (r + 1) * OUT_ROW_CHUNK)
        y = x_ref[0, rows, :] + mod_ref[0, 5:6, :] * _dot(act_s[rows, :], wd_ref[...])
        if final_norm:
            y = _rms_norm(y, fnw_ref[...])
        o_ref[0, rows, :] = y


def _ffn_layer(x, mod, n2w, w_gate, w_up, conv_w, conv_b, w_down, final_w, final_norm,
               cast_weights, layer):
    b, s, d = x.shape
    tm = TOKEN_TILE
    tile = pl.BlockSpec((1, tm, d), lambda i, j: (i, j, 0))
    grid = (b, s // tm)
    cast_in, cast_out, cast_shapes = _cast_specs(cast_weights, layer, grid)
    return pl.pallas_call(
        functools.partial(_ffn_kernel, n_cast=len(cast_weights), final_norm=final_norm),
        grid=grid,
        in_specs=[
            tile,
            pl.BlockSpec((1, 6, d), lambda i, j: (i, 0, 0)),
            _full((1, d)),
            _full((d, FFN_DIM)), _full((d, FFN_DIM)),
            _full((CONV_WIDTH, FFN_DIM)), _full((1, FFN_DIM)),
            _full((FFN_DIM, d)),
            _full((1, d)),
        ] + cast_in,
        out_specs=[tile] + cast_out,
        out_shape=[jax.ShapeDtypeStruct(x.shape, x.dtype)] + cast_shapes,
        scratch_shapes=[
            pltpu.VMEM((CONV_PAD_ROWS + tm, FFN_DIM), F32),
            pltpu.VMEM((tm, FFN_DIM), BF16),
        ],
        compiler_params=pltpu.CompilerParams(
            dimension_semantics=("arbitrary", "arbitrary"),
            vmem_limit_bytes=VMEM_LIMIT_BYTES),
        name="conv_ffn",
    )(x, mod, n2w, w_gate, w_up, conv_w, conv_b, w_down, final_w, *cast_weights)


def kernel(x, c, positions, ada_w, ada_b, norm1_w, w_in, attn_sinks, sgu_ln_w, sgu_ln_b, sgu_w, sgu_b,
           proj_a, proj_b, w_out, norm2_w, ffn_w_gate, ffn_w_up, ffn_conv_w, ffn_conv_b, ffn_w_down,
           final_norm_w):
    batch, seq, d = x.shape
    depth = ada_w.shape[0]
    assert seq % TOKEN_TILE == 0 and TOKEN_TILE % ATTN_BLOCK == 0 and d == D_MODEL

    mix_f32 = (w_in, proj_a, proj_b, w_out)
    ffn_f32 = (ffn_w_gate, ffn_w_up, ffn_w_down)
    rows = -(-batch // SUBLANES) * SUBLANES
    c_pad = jnp.pad(c, ((0, rows - batch), (0, 0)))
    mod, cos_tab, sin_tab, *mix_w = _prepare(c_pad, ada_w, ada_b, positions, mix_f32)
    mod = mod[:, :batch]

    for l in range(depth):
        mod_l = mod[l].reshape(batch, 6, d)
        sgu_b_full = jnp.repeat(sgu_b[l].T, SGU_WIDTH // SGU_GROUPS, axis=1)
        x, *ffn_w = _mix_layer(
            x, mod_l, norm1_w[l].reshape(1, d), cos_tab, sin_tab, mix_w[0],
            attn_sinks[l], sgu_ln_w[l].reshape(1, -1), sgu_ln_b[l].reshape(1, -1),
            sgu_w[l], sgu_b_full, mix_w[1], mix_w[2], mix_w[3], ffn_f32, l)
        last = l == depth - 1
        x, *mix_w = _ffn_layer(
            x, mod_l, norm2_w[l].reshape(1, d), ffn_w[0], ffn_w[1],
            ffn_conv_w[l], ffn_conv_b[l].reshape(1, -1), ffn_w[2],
            final_norm_w.reshape(1, d), last, () if last else mix_f32, l + 1)
    return x
```

```python
import functools

import jax
import jax.numpy as jnp
from jax import lax
from jax.experimental import pallas as pl
from jax.experimental.pallas import tpu as pltpu

D_MODEL = 1024
N_Q_HEADS = 16
N_KV_HEADS = 2
HEAD_DIM = 64
Q_PER_KV = N_Q_HEADS // N_KV_HEADS
ATTN_BLOCK = 128
ROPE_THETA = 500000.0
ROT_DIM = HEAD_DIM // 4
SGU_WIDTH = 1024
SGU_GROUPS = 8
SGU_CHUNK = 128
FFN_DIM = 2816
CONV_WIDTH = 3
NORM_EPS = 1e-6

Q_END = N_Q_HEADS * HEAD_DIM
K_END = Q_END + N_KV_HEADS * HEAD_DIM
V_END = K_END + N_KV_HEADS * HEAD_DIM
Z_END = V_END + 2 * SGU_WIDTH
IN_COLS = Z_END + 2 * D_MODEL

LANES = 128
SUBLANES = 8
MXU_COLS = 256
TOKEN_TILE = 512
FFN_TOKEN_TILE = 1024
NORM_ROW_CHUNK = 128
OUT_ROW_CHUNK = 256
VMEM_LIMIT_BYTES = 56 * 1024 * 1024
MASK_VALUE = -1e30
LOG2E = 1.4426950408889634

F32 = jnp.float32
BF16 = jnp.bfloat16


def _rms_norm(x, w):
    ms = jnp.mean(x * x, axis=-1, keepdims=True)
    return x * lax.rsqrt(ms + NORM_EPS) * w


def _dot(a, b):
    return jnp.dot(a, b, preferred_element_type=F32)


BF16_ROW_TILE = 16


def _cast_specs(stacked_weights, layer, grid):
    steps = grid[0] * grid[1]
    specs, shapes = [], []
    for w in stacked_weights:
        _, rows, cols = w.shape
        slab, share = rows // steps, 1
        while slab % BF16_ROW_TILE:
            slab, share = slab * 2, share * 2
        assert rows % slab == 0 and steps % share == 0
        idx = lambda i, j, share=share: ((i * grid[1] + j) // share, 0)
        specs.append((pl.BlockSpec((None, slab, cols), lambda i, j, idx=idx: (layer,) + idx(i, j)),
                      pl.BlockSpec((slab, cols), idx)))
        shapes.append(jax.ShapeDtypeStruct((rows, cols), BF16))
    return [s[0] for s in specs], [s[1] for s in specs], shapes


def _cast_slabs(src_refs, dst_refs):
    for src, dst in zip(src_refs, dst_refs):
        dst[...] = src[...].astype(BF16)


ROPE_ROW_TILE = 1024


def _split_bf16(v):
    hi = v.astype(BF16)
    return hi, (v - hi.astype(F32)).astype(BF16)


def _prep_kernel(c_ref, adaw_ref, adab_ref, pos_ref, invf_ref, *refs):
    n_cast = (len(refs) - 3) // 2
    mod_ref, cos_ref, sin_ref = refs[n_cast:n_cast + 3]
    _cast_slabs(refs[:n_cast], refs[n_cast + 3:])

    c = c_ref[...]
    rows = c.shape[0]
    c_hi, c_lo = _split_bf16(c * jax.nn.sigmoid(c))
    lhs = jnp.concatenate([c_hi, c_lo], axis=0)
    w_hi, w_lo = _split_bf16(adaw_ref[...])
    parts = _dot(lhs, w_hi) + _dot(lhs, w_lo)
    mod_ref[...] = parts[0:rows] + parts[rows:2 * rows] + adab_ref[...]

    pos = pos_ref[0].astype(F32)
    ang = pos * invf_ref[...]
    lane = lax.broadcasted_iota(jnp.int32, ang.shape, 1) & (HEAD_DIM - 1)
    cos_ref[0] = jnp.where(lane < ROT_DIM, jnp.cos(ang), 1.0)
    s = jnp.sin(ang)
    sin_ref[0] = jnp.where(lane < ROT_DIM // 2, -s, jnp.where(lane < ROT_DIM, s, 0.0))


def _prepare(c, ada_w, ada_b, positions, cast_weights):
    b, s = positions.shape
    depth, d, n = ada_w.shape
    rows = c.shape[0]
    half = ROT_DIM // 2
    inv_freq = ROPE_THETA ** (-jnp.arange(0, ROT_DIM, 2, dtype=F32) / ROT_DIM)
    lane = jnp.arange(LANES) % HEAD_DIM
    invf_lane = jnp.where(lane < ROT_DIM, inv_freq[lane % half], 0.0).reshape(1, LANES)
    tab = jax.ShapeDtypeStruct((b, s, LANES), F32)
    grid = (b, s // ROPE_ROW_TILE)
    steps = grid[0] * grid[1]
    assert steps % depth == 0 and n % (steps // depth) == 0
    col_tiles = steps // depth
    ada_cols = n // col_tiles
    ada_idx = lambda i, j: ((i * grid[1] + j) // col_tiles, 0, (i * grid[1] + j) % col_tiles)
    cast_in, cast_out, cast_shapes = _cast_specs(cast_weights, 0, grid)
    return pl.pallas_call(
        _prep_kernel,
        grid=grid,
        in_specs=[
            pl.BlockSpec((rows, d), lambda i, j: (0, 0)),
            pl.BlockSpec((None, d, ada_cols), ada_idx),
            pl.BlockSpec((None, 1, ada_cols), ada_idx),
            pl.BlockSpec((1, ROPE_ROW_TILE, 1), lambda i, j: (i, j, 0)),
            pl.BlockSpec((1, LANES), lambda i, j: (0, 0)),
        ] + cast_in,
        out_specs=[pl.BlockSpec((None, rows, ada_cols), ada_idx)]
        + [pl.BlockSpec((1, ROPE_ROW_TILE, LANES), lambda i, j: (i, j, 0))] * 2 + cast_out,
        out_shape=[jax.ShapeDtypeStruct((depth, rows, n), F32), tab, tab] + cast_shapes,
        compiler_params=pltpu.CompilerParams(
            dimension_semantics=("arbitrary", "arbitrary"),
            vmem_limit_bytes=VMEM_LIMIT_BYTES),
        name="prepare",
    )(c, ada_w, ada_b.reshape(depth, 1, n), positions.reshape(b, s, 1), invf_lane, *cast_weights)


MIX_N_INPUTS = 14


def _mix_kernel(*refs, n_cast):
    (x_ref, mod_ref, n1w_ref, cos_ref, sin_ref, win_ref, sink_ref, lnw_ref, lnb_ref,
     sguw_ref, sgub_ref, pa_ref, pb_ref, wo_ref) = refs[:MIX_N_INPUTS]
    o_ref = refs[MIX_N_INPUTS + n_cast]
    q_s, k4_s, v4_s, bias_s, yattn_s, ysgu_s = refs[MIX_N_INPUTS + 2 * n_cast + 1:]
    _cast_slabs(refs[MIX_N_INPUTS:MIX_N_INPUTS + n_cast],
                refs[MIX_N_INPUTS + n_cast + 1:MIX_N_INPUTS + 2 * n_cast + 1])
    tm = TOKEN_TILE
    nblk = tm // ATTN_BLOCK
    band_keys = 2 * ATTN_BLOCK
    j = pl.program_id(1)

    @pl.when(j == 0)
    def _():
        k4_s[:, 0:ATTN_BLOCK, :] = jnp.zeros((4, ATTN_BLOCK, LANES), BF16)
        v4_s[:, 0:ATTN_BLOCK, :] = jnp.zeros((4, ATTN_BLOCK, LANES), BF16)

    scale1 = 1.0 + mod_ref[0, 1:2, :]
    hb_chunks, q_chunks = [], []
    for r in range(tm // NORM_ROW_CHUNK):
        xr = x_ref[0, r * NORM_ROW_CHUNK:(r + 1) * NORM_ROW_CHUNK, :]
        hr = (_rms_norm(xr, n1w_ref[...]) * scale1 + mod_ref[0, 0:1, :]).astype(BF16)
        hb_chunks.append(hr)
        q_chunks.append(_dot(hr, win_ref[:, 0:Q_END]))
    hb = jnp.concatenate(hb_chunks, axis=0)
    q = jnp.concatenate(q_chunks, axis=0)
    kv = _dot(hb, win_ref[:, Q_END:V_END])
    z = _dot(hb, win_ref[:, V_END:Z_END])

    cos_t = cos_ref[0]
    sin_t = sin_ref[0]
    lane = lax.broadcasted_iota(jnp.int32, (tm, LANES), 1)
    first_half = (lane & (HEAD_DIM - 1)) < (ROT_DIM // 2)
    left_head = lane < HEAD_DIM

    def rope(t):
        partner = jnp.where(first_half,
                            pltpu.roll(t, LANES - ROT_DIM // 2, 1),
                            pltpu.roll(t, ROT_DIM // 2, 1))
        return t * cos_t + partner * sin_t

    for c in range(Q_END // LANES):
        qc = rope(q[:, c * LANES:(c + 1) * LANES]) * (HEAD_DIM ** -0.5 * LOG2E)
        q_s[:, c * LANES:(c + 1) * LANES] = qc.astype(BF16)

    k = rope(kv[:, 0:LANES])
    v = kv[:, LANES:2 * LANES]
    for src, dst in ((k, k4_s), (v, v4_s)):
        swapped = pltpu.roll(src, HEAD_DIM, 1)
        zero = jnp.zeros_like(src)
        rows = slice(ATTN_BLOCK, ATTN_BLOCK + tm)
        dst[0, rows, :] = jnp.where(left_head, src, zero).astype(BF16)
        dst[1, rows, :] = jnp.where(left_head, zero, swapped).astype(BF16)
        dst[2, rows, :] = jnp.where(left_head, swapped, zero).astype(BF16)
        dst[3, rows, :] = jnp.where(left_head, zero, src).astype(BF16)

    gates = _dot(hb, win_ref[:, Z_END:IN_COLS])

    z = 0.5 * z * (1.0 + lax.erf(z * (2.0 ** -0.5)))
    u = z[:, 0:SGU_WIDTH]
    v_sgu = z[:, SGU_WIDTH:]
    mu = jnp.mean(v_sgu, axis=-1, keepdims=True)
    dv = v_sgu - mu
    var = jnp.mean(dv * dv, axis=-1, keepdims=True)
    vn = (dv * lax.rsqrt(var + NORM_EPS) * lnw_ref[...] + lnb_ref[...]).astype(BF16)
    ti = lax.broadcasted_iota(jnp.int32, (SGU_CHUNK, SGU_CHUNK), 0)
    si = lax.broadcasted_iota(jnp.int32, (SGU_CHUNK, SGU_CHUNK), 1)
    causal = si <= ti
    gdim = SGU_WIDTH // SGU_GROUPS
    nchunk = tm // SGU_CHUNK
    for g in range(SGU_GROUPS):
        cols = slice(g * gdim, (g + 1) * gdim)
        wm = jnp.where(causal, sguw_ref[g], 0.0).astype(BF16)
        rhs = jnp.concatenate(
            [vn[c * SGU_CHUNK:(c + 1) * SGU_CHUNK, cols] for c in range(nchunk)], axis=1)
        f = _dot(wm, rhs)
        for c in range(nchunk):
            rows = slice(c * SGU_CHUNK, (c + 1) * SGU_CHUNK)
            fc = f[:, c * gdim:(c + 1) * gdim] + sgub_ref[:, cols]
            ysgu_s[rows, cols] = (u[rows, cols] * fc).astype(BF16)

    sgu_part = jax.nn.sigmoid(gates[:, 0:D_MODEL]) * _dot(ysgu_s[...], pa_ref[...])
    gate_b = jax.nn.sigmoid(gates[:, D_MODEL:])

    er = lax.broadcasted_iota(jnp.int32, (ATTN_BLOCK, LANES), 0)
    ec = lax.broadcasted_iota(jnp.int32, (ATTN_BLOCK, LANES), 1)
    eye = jnp.where(er == ec, 1.0, 0.0).astype(BF16)
    kr = lax.broadcasted_iota(jnp.int32, (band_keys, ATTN_BLOCK), 0)
    qc_i = lax.broadcasted_iota(jnp.int32, (band_keys, ATTN_BLOCK), 1)
    band_t = (kr > qc_i) & (kr <= qc_i + ATTN_BLOCK)
    for idx, ok in ((0, band_t & (kr >= ATTN_BLOCK)), (1, band_t)):
        bias_t = jnp.where(ok, 0.0, MASK_VALUE).astype(BF16)
        bias_s[idx, 0:band_keys, :] = bias_t
        bias_s[idx, band_keys:2 * band_keys, :] = bias_t
    sr = lax.broadcasted_iota(jnp.int32, (2 * band_keys, LANES), 0)
    sc = lax.broadcasted_iota(jnp.int32, (2 * band_keys, LANES), 1)
    ones_sel = jnp.where((sr < band_keys) == (sc < HEAD_DIM), 1.0, 0.0).astype(BF16)
    sink_value_row = (sr == 0) | (sr == band_keys)
    sink_lane = ec == 0
    contract_last = (((1,), (1,)), ((), ()))

    def score_matmuls(b):
        r0 = b * ATTN_BLOCK
        bias_t = bias_s[jnp.where(j > 0, 1, 0)] if b == 0 else bias_s[1]
        scores = []
        value_ops = []
        for g in range(N_KV_HEADS):
            kbd = jnp.concatenate([k4_s[2 * g, pl.ds(r0, band_keys), :],
                                   k4_s[2 * g + 1, pl.ds(r0, band_keys), :]], axis=0)
            vbd = jnp.concatenate([v4_s[2 * g, pl.ds(r0, band_keys), :],
                                   v4_s[2 * g + 1, pl.ds(r0, band_keys), :]], axis=0)
            kbd = jnp.concatenate([kbd, bias_t], axis=1)
            vbd = jnp.where(sink_value_row, jnp.zeros_like(vbd), vbd)
            value_ops.append(jnp.concatenate([vbd, ones_sel], axis=1))
            qrows = []
            for pr in range(Q_PER_KV // 2):
                col = (g * Q_PER_KV + 2 * pr) * HEAD_DIM
                qrows.append(jnp.concatenate([q_s[pl.ds(r0, ATTN_BLOCK), col:col + LANES], eye], axis=1))
            scores.append(lax.dot_general(jnp.concatenate(qrows, axis=0), kbd, contract_last,
                                          preferred_element_type=F32))
        return scores, value_ops

    def softmax_value_matmuls(b, scores, value_ops):
        r0 = b * ATTN_BLOCK
        for g in range(N_KV_HEADS):
            probs = []
            for pr in range(Q_PER_KV // 2):
                rows = slice(pr * ATTN_BLOCK, (pr + 1) * ATTN_BLOCK)
                pair_probs = []
                for hh in range(2):
                    c0 = hh * band_keys
                    sink = sink_ref[g * Q_PER_KV + 2 * pr + hh] * LOG2E
                    first = jnp.where(sink_lane, sink, scores[g][rows, c0:c0 + LANES])
                    rest = scores[g][rows, c0 + LANES:c0 + band_keys]
                    m = jnp.max(jnp.maximum(first, rest), axis=-1, keepdims=True)
                    pair_probs += [jnp.exp2(first - m).astype(BF16), jnp.exp2(rest - m).astype(BF16)]
                probs.append(jnp.concatenate(pair_probs, axis=1))
            o = _dot(jnp.concatenate(probs, axis=0), value_ops[g])
            for pr in range(Q_PER_KV // 2):
                rows = slice(pr * ATTN_BLOCK, (pr + 1) * ATTN_BLOCK)
                col = (g * Q_PER_KV + 2 * pr) * HEAD_DIM
                yattn_s[pl.ds(r0, ATTN_BLOCK), col:col + LANES] = (
                    o[rows, 0:LANES] / o[rows, LANES:]).astype(BF16)

    pending = score_matmuls(0)
    for b in range(nblk):
        upcoming = score_matmuls(b + 1) if b + 1 < nblk else None
        softmax_value_matmuls(b, *pending)
        pending = upcoming

    k4_s[:, 0:ATTN_BLOCK, :] = k4_s[:, tm:tm + ATTN_BLOCK, :]
    v4_s[:, 0:ATTN_BLOCK, :] = v4_s[:, tm:tm + ATTN_BLOCK, :]

    merged = (sgu_part + gate_b * _dot(yattn_s[...], pb_ref[...])).astype(BF16)
    for r in range(tm // OUT_ROW_CHUNK):
        rows = slice(r * OUT_ROW_CHUNK, (r + 1) * OUT_ROW_CHUNK)
        o_ref[0, rows, :] = x_ref[0, rows, :] + mod_ref[0, 2:3, :] * _dot(merged[rows, :], wo_ref[...])


def _full(shape):
    zeros = (0,) * len(shape)
    return pl.BlockSpec(shape, lambda i, j: zeros)


def _mix_layer(x, mod, n1w, cos_tab, sin_tab, w_in, sinks, ln_w, ln_b, sgu_w, sgu_b_full,
               proj_a, proj_b, w_out, cast_weights, layer):
    b, s, d = x.shape
    tm = TOKEN_TILE
    tile = pl.BlockSpec((1, tm, d), lambda i, j: (i, j, 0))
    tab = pl.BlockSpec((1, tm, LANES), lambda i, j: (i, j, 0))
    grid = (b, s // tm)
    cast_in, cast_out, cast_shapes = _cast_specs(cast_weights, layer, grid)
    return pl.pallas_call(
        functools.partial(_mix_kernel, n_cast=len(cast_weights)),
        grid=grid,
        in_specs=[
            tile,
            pl.BlockSpec((1, 6, d), lambda i, j: (i, 0, 0)),
            _full((1, d)),
            tab, tab,
            _full((d, IN_COLS)),
            pl.BlockSpec(memory_space=pltpu.SMEM),
            _full((1, SGU_WIDTH)), _full((1, SGU_WIDTH)),
            _full((SGU_GROUPS, SGU_CHUNK, SGU_CHUNK)),
            _full((SGU_CHUNK, SGU_WIDTH)),
            _full((SGU_WIDTH, d)), _full((Q_END, d)), _full((d, d)),
        ] + cast_in,
        out_specs=[tile] + cast_out,
        out_shape=[jax.ShapeDtypeStruct(x.shape, x.dtype)] + cast_shapes,
        scratch_shapes=[
            pltpu.VMEM((tm, Q_END), BF16),
            pltpu.VMEM((4, tm + ATTN_BLOCK, LANES), BF16),
            pltpu.VMEM((4, tm + ATTN_BLOCK, LANES), BF16),
            pltpu.VMEM((2, 4 * ATTN_BLOCK, LANES), BF16),
            pltpu.VMEM((tm, Q_END), BF16),
            pltpu.VMEM((tm, SGU_WIDTH), BF16),
        ],
        compiler_params=pltpu.CompilerParams(
            dimension_semantics=("arbitrary", "arbitrary"),
            vmem_limit_bytes=VMEM_LIMIT_BYTES),
        name="token_mixer",
    )(x, mod, n1w, cos_tab, sin_tab, w_in, sinks, ln_w, ln_b, sgu_w, sgu_b_full,
      proj_a, proj_b, w_out, *cast_weights)


CONV_PAD_ROWS = SUBLANES


FFN_N_INPUTS = 9


def _ffn_kernel(*refs, n_cast, final_norm):
    (x_ref, mod_ref, n2w_ref, wg_ref, wu_ref, cw_ref, cb_ref, wd_ref, fnw_ref) = refs[:FFN_N_INPUTS]
    o_ref = refs[FFN_N_INPUTS + n_cast]
    a_s, carry_s, act_s = refs[FFN_N_INPUTS + 2 * n_cast + 1:]
    _cast_slabs(refs[FFN_N_INPUTS:FFN_N_INPUTS + n_cast],
                refs[FFN_N_INPUTS + n_cast + 1:FFN_N_INPUTS + 2 * n_cast + 1])
    tm = FFN_TOKEN_TILE
    j = pl.program_id(1)

    @pl.when(j == 0)
    def _():
        carry_s[...] = jnp.zeros((CONV_PAD_ROWS, FFN_DIM), F32)

    scale2 = 1.0 + mod_ref[0, 4:5, :]
    hb_chunks, a_chunks, up_chunks = [], [], []
    for r in range(tm // NORM_ROW_CHUNK):
        xr = x_ref[0, r * NORM_ROW_CHUNK:(r + 1) * NORM_ROW_CHUNK, :]
        hr = (_rms_norm(xr, n2w_ref[...]) * scale2 + mod_ref[0, 3:4, :]).astype(BF16)
        hb_chunks.append(hr)
        a_chunks.append(_dot(hr, wg_ref[:, 0:MXU_COLS]))
        up_chunks.append(_dot(hr, wu_ref[:, 0:MXU_COLS]))
    hb = jnp.concatenate(hb_chunks, axis=0)
    first_a = jnp.concatenate(a_chunks, axis=0)
    first_up = jnp.concatenate(up_chunks, axis=0)

    for c in range(FFN_DIM // MXU_COLS):
        cols = slice(c * MXU_COLS, (c + 1) * MXU_COLS)
        a = first_a if c == 0 else _dot(hb, wg_ref[:, cols])
        stage = a_s.at[c % 2]
        stage[0:CONV_PAD_ROWS, :] = carry_s[:, cols]
        stage[CONV_PAD_ROWS:CONV_PAD_ROWS + tm, :] = a
        conv = cb_ref[:, cols] + cw_ref[2:3, cols] * a
        for tap in range(CONV_WIDTH - 1):
            back = CONV_WIDTH - 1 - tap
            conv = conv + cw_ref[tap:tap + 1, cols] * stage[CONV_PAD_ROWS - back:CONV_PAD_ROWS - back + tm, :]
        carry_s[:, cols] = stage[tm:tm + CONV_PAD_ROWS, :]
        up = first_up if c == 0 else _dot(hb, wu_ref[:, cols])
        act_s[:, cols] = (conv * jax.nn.sigmoid(conv) * up).astype(BF16)

    for r in range(tm // OUT_ROW_CHUNK):
        rows = slice(r * OUT_ROW_CHUNK, (r + 1) * OUT_ROW_CHUNK)
        y = x_ref[0, rows, :] + mod_ref[0, 5:6, :] * _dot(act_s[rows, :], wd_ref[...])
        if final_norm:
            y = _rms_norm(y, fnw_ref[...])
        o_ref[0, rows, :] = y


def _ffn_layer(x, mod, n2w, w_gate, w_up, conv_w, conv_b, w_down, final_w, final_norm,
               cast_weights, layer):
    b, s, d = x.shape
    tm = FFN_TOKEN_TILE
    tile = pl.BlockSpec((1, tm, d), lambda i, j: (i, j, 0))
    grid = (b, s // tm)
    cast_in, cast_out, cast_shapes = _cast_specs(cast_weights, layer, grid)
    return pl.pallas_call(
        functools.partial(_ffn_kernel, n_cast=len(cast_weights), final_norm=final_norm),
        grid=grid,
        in_specs=[
            tile,
            pl.BlockSpec((1, 6, d), lambda i, j: (i, 0, 0)),
            _full((1, d)),
            _full((d, FFN_DIM)), _full((d, FFN_DIM)),
            _full((CONV_WIDTH, FFN_DIM)), _full((1, FFN_DIM)),
            _full((FFN_DIM, d)),
            _full((1, d)),
        ] + cast_in,
        out_specs=[tile] + cast_out,
        out_shape=[jax.ShapeDtypeStruct(x.shape, x.dtype)] + cast_shapes,
        scratch_shapes=[
            pltpu.VMEM((2, CONV_PAD_ROWS + tm, MXU_COLS), F32),
            pltpu.VMEM((CONV_PAD_ROWS, FFN_DIM), F32),
            pltpu.VMEM((tm, FFN_DIM), BF16),
        ],
        compiler_params=pltpu.CompilerParams(
            dimension_semantics=("arbitrary", "arbitrary"),
            vmem_limit_bytes=VMEM_LIMIT_BYTES),
        name="conv_ffn",
    )(x, mod, n2w, w_gate, w_up, conv_w, conv_b, w_down, final_w, *cast_weights)


def kernel(x, c, positions, ada_w, ada_b, norm1_w, w_in, attn_sinks, sgu_ln_w, sgu_ln_b, sgu_w, sgu_b,
           proj_a, proj_b, w_out, norm2_w, ffn_w_gate, ffn_w_up, ffn_conv_w, ffn_conv_b, ffn_w_down,
           final_norm_w):
    batch, seq, d = x.shape
    depth = ada_w.shape[0]
    assert seq % TOKEN_TILE == 0 and seq % FFN_TOKEN_TILE == 0 and TOKEN_TILE % ATTN_BLOCK == 0
    assert d == D_MODEL

    mix_f32 = (w_in, proj_a, proj_b, w_out)
    ffn_f32 = (ffn_w_gate, ffn_w_up, ffn_w_down)
    rows = -(-batch // SUBLANES) * SUBLANES
    c_pad = jnp.pad(c, ((0, rows - batch), (0, 0)))
    mod, cos_tab, sin_tab, *mix_w = _prepare(c_pad, ada_w, ada_b, positions, mix_f32)
    mod = mod[:, :batch]

    for l in range(depth):
        mod_l = mod[l].reshape(batch, 6, d)
        sgu_b_full = jnp.repeat(sgu_b[l].T, SGU_WIDTH // SGU_GROUPS, axis=1)
        x, *ffn_w = _mix_layer(
            x, mod_l, norm1_w[l].reshape(1, d), cos_tab, sin_tab, mix_w[0],
            attn_sinks[l], sgu_ln_w[l].reshape(1, -1), sgu_ln_b[l].reshape(1, -1),
            sgu_w[l], sgu_b_full, mix_w[1], mix_w[2], mix_w[3], ffn_f32, l)
        last = l == depth - 1
        x, *mix_w = _ffn_layer(
            x, mod_l, norm2_w[l].reshape(1, d), ffn_w[0], ffn_w[1],
            ffn_conv_w[l], ffn_conv_b[l].reshape(1, -1), ffn_w[2],
            final_norm_w.reshape(1, d), last, () if last else mix_f32, l + 1)
    return x
```

```python
import functools

import jax
import jax.numpy as jnp
from jax import lax
from jax.experimental import pallas as pl
from jax.experimental.pallas import tpu as pltpu

D_MODEL = 1024
N_Q_HEADS = 16
N_KV_HEADS = 2
HEAD_DIM = 64
Q_PER_KV = N_Q_HEADS // N_KV_HEADS
ATTN_BLOCK = 128
ROPE_THETA = 500000.0
ROT_DIM = HEAD_DIM // 4
SGU_WIDTH = 1024
SGU_GROUPS = 8
SGU_CHUNK = 128
FFN_DIM = 2816
CONV_WIDTH = 3
NORM_EPS = 1e-6

Q_END = N_Q_HEADS * HEAD_DIM
K_END = Q_END + N_KV_HEADS * HEAD_DIM
V_END = K_END + N_KV_HEADS * HEAD_DIM
Z_END = V_END + 2 * SGU_WIDTH
IN_COLS = Z_END + 2 * D_MODEL

LANES = 128
SUBLANES = 8
MXU_COLS = 256
TOKEN_TILE = 512
FFN_TOKEN_TILE = 1024
NORM_ROW_CHUNK = 128
OUT_ROW_CHUNK = 256
ATTN_LOOKAHEAD = 2
VMEM_LIMIT_BYTES = 56 * 1024 * 1024
MASK_VALUE = -1e30
LOG2E = 1.4426950408889634

F32 = jnp.float32
BF16 = jnp.bfloat16


def _rms_norm(x, w):
    ms = jnp.mean(x * x, axis=-1, keepdims=True)
    return x * lax.rsqrt(ms + NORM_EPS) * w


def _dot(a, b):
    return jnp.dot(a, b, preferred_element_type=F32)


BF16_ROW_TILE = 16


def _cast_specs(stacked_weights, layer, grid):
    steps = grid[0] * grid[1]
    specs, shapes = [], []
    for w in stacked_weights:
        _, rows, cols = w.shape
        slab, share = rows // steps, 1
        while slab % BF16_ROW_TILE:
            slab, share = slab * 2, share * 2
        assert rows % slab == 0 and steps % share == 0
        idx = lambda i, j, share=share: ((i * grid[1] + j) // share, 0)
        specs.append((pl.BlockSpec((None, slab, cols), lambda i, j, idx=idx: (layer,) + idx(i, j)),
                      pl.BlockSpec((slab, cols), idx)))
        shapes.append(jax.ShapeDtypeStruct((rows, cols), BF16))
    return [s[0] for s in specs], [s[1] for s in specs], shapes


def _cast_slabs(src_refs, dst_refs):
    for src, dst in zip(src_refs, dst_refs):
        dst[...] = src[...].astype(BF16)


ROPE_ROW_TILE = 1024


def _split_bf16(v):
    hi = v.astype(BF16)
    return hi, (v - hi.astype(F32)).astype(BF16)


def _prep_kernel(c_ref, adaw_ref, adab_ref, pos_ref, invf_ref, *refs):
    n_cast = (len(refs) - 3) // 2
    mod_ref, cos_ref, sin_ref = refs[n_cast:n_cast + 3]
    _cast_slabs(refs[:n_cast], refs[n_cast + 3:])

    c = c_ref[...]
    rows = c.shape[0]
    c_hi, c_lo = _split_bf16(c * jax.nn.sigmoid(c))
    lhs = jnp.concatenate([c_hi, c_lo], axis=0)
    w_hi, w_lo = _split_bf16(adaw_ref[...])
    parts = _dot(lhs, w_hi) + _dot(lhs, w_lo)
    mod_ref[...] = parts[0:rows] + parts[rows:2 * rows] + adab_ref[...]

    pos = pos_ref[0].astype(F32)
    ang = pos * invf_ref[...]
    lane = lax.broadcasted_iota(jnp.int32, ang.shape, 1) & (HEAD_DIM - 1)
    cos_ref[0] = jnp.where(lane < ROT_DIM, jnp.cos(ang), 1.0)
    s = jnp.sin(ang)
    sin_ref[0] = jnp.where(lane < ROT_DIM // 2, -s, jnp.where(lane < ROT_DIM, s, 0.0))


def _prepare(c, ada_w, ada_b, positions, cast_weights):
    b, s = positions.shape
    depth, d, n = ada_w.shape
    rows = c.shape[0]
    half = ROT_DIM // 2
    inv_freq = ROPE_THETA ** (-jnp.arange(0, ROT_DIM, 2, dtype=F32) / ROT_DIM)
    lane = jnp.arange(LANES) % HEAD_DIM
    invf_lane = jnp.where(lane < ROT_DIM, inv_freq[lane % half], 0.0).reshape(1, LANES)
    tab = jax.ShapeDtypeStruct((b, s, LANES), F32)
    grid = (b, s // ROPE_ROW_TILE)
    steps = grid[0] * grid[1]
    assert steps % depth == 0 and n % (steps // depth) == 0
    col_tiles = steps // depth
    ada_cols = n // col_tiles
    ada_idx = lambda i, j: ((i * grid[1] + j) // col_tiles, 0, (i * grid[1] + j) % col_tiles)
    cast_in, cast_out, cast_shapes = _cast_specs(cast_weights, 0, grid)
    return pl.pallas_call(
        _prep_kernel,
        grid=grid,
        in_specs=[
            pl.BlockSpec((rows, d), lambda i, j: (0, 0)),
            pl.BlockSpec((None, d, ada_cols), ada_idx),
            pl.BlockSpec((None, 1, ada_cols), ada_idx),
            pl.BlockSpec((1, ROPE_ROW_TILE, 1), lambda i, j: (i, j, 0)),
            pl.BlockSpec((1, LANES), lambda i, j: (0, 0)),
        ] + cast_in,
        out_specs=[pl.BlockSpec((None, rows, ada_cols), ada_idx)]
        + [pl.BlockSpec((1, ROPE_ROW_TILE, LANES), lambda i, j: (i, j, 0))] * 2 + cast_out,
        out_shape=[jax.ShapeDtypeStruct((depth, rows, n), F32), tab, tab] + cast_shapes,
        compiler_params=pltpu.CompilerParams(
            dimension_semantics=("arbitrary", "arbitrary"),
            vmem_limit_bytes=VMEM_LIMIT_BYTES),
        name="prepare",
    )(c, ada_w, ada_b.reshape(depth, 1, n), positions.reshape(b, s, 1), invf_lane, *cast_weights)


MIX_N_INPUTS = 14


def _mix_kernel(*refs, n_cast):
    (x_ref, mod_ref, n1w_ref, cos_ref, sin_ref, win_ref, sink_ref, lnw_ref, lnb_ref,
     sguw_ref, sgub_ref, pa_ref, pb_ref, wo_ref) = refs[:MIX_N_INPUTS]
    o_ref = refs[MIX_N_INPUTS + n_cast]
    q_s, k4_s, v4_s, bias_s, yattn_s, ysgu_s = refs[MIX_N_INPUTS + 2 * n_cast + 1:]
    _cast_slabs(refs[MIX_N_INPUTS:MIX_N_INPUTS + n_cast],
                refs[MIX_N_INPUTS + n_cast + 1:MIX_N_INPUTS + 2 * n_cast + 1])
    tm = TOKEN_TILE
    nblk = tm // ATTN_BLOCK
    band_keys = 2 * ATTN_BLOCK
    j = pl.program_id(1)

    @pl.when(j == 0)
    def _():
        k4_s[:, 0:ATTN_BLOCK, :] = jnp.zeros((4, ATTN_BLOCK, LANES), BF16)
        v4_s[:, 0:ATTN_BLOCK, :] = jnp.zeros((4, ATTN_BLOCK, LANES), BF16)

    scale1 = 1.0 + mod_ref[0, 1:2, :]
    hb_chunks, q_chunks = [], []
    for r in range(tm // NORM_ROW_CHUNK):
        xr = x_ref[0, r * NORM_ROW_CHUNK:(r + 1) * NORM_ROW_CHUNK, :]
        hr = (_rms_norm(xr, n1w_ref[...]) * scale1 + mod_ref[0, 0:1, :]).astype(BF16)
        hb_chunks.append(hr)
        q_chunks.append(_dot(hr, win_ref[:, 0:Q_END]))
    hb = jnp.concatenate(hb_chunks, axis=0)
    q = jnp.concatenate(q_chunks, axis=0)
    kv = _dot(hb, win_ref[:, Q_END:V_END])
    z = _dot(hb, win_ref[:, V_END:Z_END])

    cos_t = cos_ref[0]
    sin_t = sin_ref[0]
    lane = lax.broadcasted_iota(jnp.int32, (tm, LANES), 1)
    first_half = (lane & (HEAD_DIM - 1)) < (ROT_DIM // 2)
    left_head = lane < HEAD_DIM

    def rope(t):
        partner = jnp.where(first_half,
                            pltpu.roll(t, LANES - ROT_DIM // 2, 1),
                            pltpu.roll(t, ROT_DIM // 2, 1))
        return t * cos_t + partner * sin_t

    for c in range(Q_END // LANES):
        qc = rope(q[:, c * LANES:(c + 1) * LANES]) * (HEAD_DIM ** -0.5 * LOG2E)
        q_s[:, c * LANES:(c + 1) * LANES] = qc.astype(BF16)

    k = rope(kv[:, 0:LANES])
    v = kv[:, LANES:2 * LANES]
    for src, dst in ((k, k4_s), (v, v4_s)):
        swapped = pltpu.roll(src, HEAD_DIM, 1)
        zero = jnp.zeros_like(src)
        rows = slice(ATTN_BLOCK, ATTN_BLOCK + tm)
        dst[0, rows, :] = jnp.where(left_head, src, zero).astype(BF16)
        dst[1, rows, :] = jnp.where(left_head, zero, swapped).astype(BF16)
        dst[2, rows, :] = jnp.where(left_head, swapped, zero).astype(BF16)
        dst[3, rows, :] = jnp.where(left_head, zero, src).astype(BF16)

    gates = _dot(hb, win_ref[:, Z_END:IN_COLS])

    z = 0.5 * z * (1.0 + lax.erf(z * (2.0 ** -0.5)))
    u = z[:, 0:SGU_WIDTH]
    v_sgu = z[:, SGU_WIDTH:]
    mu = jnp.mean(v_sgu, axis=-1, keepdims=True)
    dv = v_sgu - mu
    var = jnp.mean(dv * dv, axis=-1, keepdims=True)
    vn = (dv * lax.rsqrt(var + NORM_EPS) * lnw_ref[...] + lnb_ref[...]).astype(BF16)
    ti = lax.broadcasted_iota(jnp.int32, (SGU_CHUNK, SGU_CHUNK), 0)
    si = lax.broadcasted_iota(jnp.int32, (SGU_CHUNK, SGU_CHUNK), 1)
    causal = si <= ti
    gdim = SGU_WIDTH // SGU_GROUPS
    nchunk = tm // SGU_CHUNK
    for g in range(SGU_GROUPS):
        cols = slice(g * gdim, (g + 1) * gdim)
        wm = jnp.where(causal, sguw_ref[g], 0.0).astype(BF16)
        rhs = jnp.concatenate(
            [vn[c * SGU_CHUNK:(c + 1) * SGU_CHUNK, cols] for c in range(nchunk)], axis=1)
        f = _dot(wm, rhs)
        for c in range(nchunk):
            rows = slice(c * SGU_CHUNK, (c + 1) * SGU_CHUNK)
            fc = f[:, c * gdim:(c + 1) * gdim] + sgub_ref[:, cols]
            ysgu_s[rows, cols] = (u[rows, cols] * fc).astype(BF16)

    sgu_part = jax.nn.sigmoid(gates[:, 0:D_MODEL]) * _dot(ysgu_s[...], pa_ref[...])
    gate_b = jax.nn.sigmoid(gates[:, D_MODEL:])

    er = lax.broadcasted_iota(jnp.int32, (ATTN_BLOCK, LANES), 0)
    ec = lax.broadcasted_iota(jnp.int32, (ATTN_BLOCK, LANES), 1)
    eye = jnp.where(er == ec, 1.0, 0.0).astype(BF16)
    kr = lax.broadcasted_iota(jnp.int32, (band_keys, ATTN_BLOCK), 0)
    qc_i = lax.broadcasted_iota(jnp.int32, (band_keys, ATTN_BLOCK), 1)
    band_t = (kr > qc_i) & (kr <= qc_i + ATTN_BLOCK)
    for idx, ok in ((0, band_t & (kr >= ATTN_BLOCK)), (1, band_t)):
        bias_t = jnp.where(ok, 0.0, MASK_VALUE).astype(BF16)
        bias_s[idx, 0:band_keys, :] = bias_t
        bias_s[idx, band_keys:2 * band_keys, :] = bias_t
    sr = lax.broadcasted_iota(jnp.int32, (2 * band_keys, LANES), 0)
    sc = lax.broadcasted_iota(jnp.int32, (2 * band_keys, LANES), 1)
    ones_sel = jnp.where((sr < band_keys) == (sc < HEAD_DIM), 1.0, 0.0).astype(BF16)
    sink_value_row = (sr == 0) | (sr == band_keys)
    sink_lane = ec == 0
    contract_last = (((1,), (1,)), ((), ()))

    def score_matmul(b, g):
        r0 = b * ATTN_BLOCK
        bias_t = bias_s[jnp.where(j > 0, 1, 0)] if b == 0 else bias_s[1]
        kbd = jnp.concatenate([k4_s[2 * g, pl.ds(r0, band_keys), :],
                               k4_s[2 * g + 1, pl.ds(r0, band_keys), :]], axis=0)
        kbd = jnp.concatenate([kbd, bias_t], axis=1)
        qrows = []
        for pr in range(Q_PER_KV // 2):
            col = (g * Q_PER_KV + 2 * pr) * HEAD_DIM
            qrows.append(jnp.concatenate([q_s[pl.ds(r0, ATTN_BLOCK), col:col + LANES], eye], axis=1))
        return lax.dot_general(jnp.concatenate(qrows, axis=0), kbd, contract_last,
                               preferred_element_type=F32)

    def softmax_value_matmul(b, g, scores):
        r0 = b * ATTN_BLOCK
        vbd = jnp.concatenate([v4_s[2 * g, pl.ds(r0, band_keys), :],
                               v4_s[2 * g + 1, pl.ds(r0, band_keys), :]], axis=0)
        vbd = jnp.where(sink_value_row, jnp.zeros_like(vbd), vbd)
        value_op = jnp.concatenate([vbd, ones_sel], axis=1)
        probs = []
        for pr in range(Q_PER_KV // 2):
            rows = slice(pr * ATTN_BLOCK, (pr + 1) * ATTN_BLOCK)
            pair_probs = []
            for hh in range(2):
                c0 = hh * band_keys
                sink = sink_ref[g * Q_PER_KV + 2 * pr + hh] * LOG2E
                first = jnp.where(sink_lane, sink, scores[rows, c0:c0 + LANES])
                rest = scores[rows, c0 + LANES:c0 + band_keys]
                m = jnp.max(jnp.maximum(first, rest), axis=-1, keepdims=True)
                pair_probs += [jnp.exp2(first - m).astype(BF16), jnp.exp2(rest - m).astype(BF16)]
            probs.append(jnp.concatenate(pair_probs, axis=1))
        o = _dot(jnp.concatenate(probs, axis=0), value_op)
        for pr in range(Q_PER_KV // 2):
            rows = slice(pr * ATTN_BLOCK, (pr + 1) * ATTN_BLOCK)
            col = (g * Q_PER_KV + 2 * pr) * HEAD_DIM
            yattn_s[pl.ds(r0, ATTN_BLOCK), col:col + LANES] = (
                o[rows, 0:LANES] / o[rows, LANES:]).astype(BF16)

    units = [(b, g) for b in range(nblk) for g in range(N_KV_HEADS)]
    pending = [score_matmul(*u) for u in units[:ATTN_LOOKAHEAD]]
    for i, (b, g) in enumerate(units):
        if i + ATTN_LOOKAHEAD < len(units):
            pending.append(score_matmul(*units[i + ATTN_LOOKAHEAD]))
        softmax_value_matmul(b, g, pending.pop(0))

    k4_s[:, 0:ATTN_BLOCK, :] = k4_s[:, tm:tm + ATTN_BLOCK, :]
    v4_s[:, 0:ATTN_BLOCK, :] = v4_s[:, tm:tm + ATTN_BLOCK, :]

    merged = (sgu_part + gate_b * _dot(yattn_s[...], pb_ref[...])).astype(BF16)
    for r in range(tm // OUT_ROW_CHUNK):
        rows = slice(r * OUT_ROW_CHUNK, (r + 1) * OUT_ROW_CHUNK)
        o_ref[0, rows, :] = x_ref[0, rows, :] + mod_ref[0, 2:3, :] * _dot(merged[rows, :], wo_ref[...])


def _full(shape):
    zeros = (0,) * len(shape)
    return pl.BlockSpec(shape, lambda i, j: zeros)


def _mix_layer(x, mod, n1w, cos_tab, sin_tab, w_in, sinks, ln_w, ln_b, sgu_w, sgu_b_full,
               proj_a, proj_b, w_out, cast_weights, layer):
    b, s, d = x.shape
    tm = TOKEN_TILE
    tile = pl.BlockSpec((1, tm, d), lambda i, j: (i, j, 0))
    tab = pl.BlockSpec((1, tm, LANES), lambda i, j: (i, j, 0))
    grid = (b, s // tm)
    cast_in, cast_out, cast_shapes = _cast_specs(cast_weights, layer, grid)
    return pl.pallas_call(
        functools.partial(_mix_kernel, n_cast=len(cast_weights)),
        grid=grid,
        in_specs=[
            tile,
            pl.BlockSpec((1, 6, d), lambda i, j: (i, 0, 0)),
            _full((1, d)),
            tab, tab,
            _full((d, IN_COLS)),
            pl.BlockSpec(memory_space=pltpu.SMEM),
            _full((1, SGU_WIDTH)), _full((1, SGU_WIDTH)),
            _full((SGU_GROUPS, SGU_CHUNK, SGU_CHUNK)),
            _full((SGU_CHUNK, SGU_WIDTH)),
            _full((SGU_WIDTH, d)), _full((Q_END, d)), _full((d, d)),
        ] + cast_in,
        out_specs=[tile] + cast_out,
        out_shape=[jax.ShapeDtypeStruct(x.shape, x.dtype)] + cast_shapes,
        scratch_shapes=[
            pltpu.VMEM((tm, Q_END), BF16),
            pltpu.VMEM((4, tm + ATTN_BLOCK, LANES), BF16),
            pltpu.VMEM((4, tm + ATTN_BLOCK, LANES), BF16),
            pltpu.VMEM((2, 4 * ATTN_BLOCK, LANES), BF16),
            pltpu.VMEM((tm, Q_END), BF16),
            pltpu.VMEM((tm, SGU_WIDTH), BF16),
        ],
        compiler_params=pltpu.CompilerParams(
            dimension_semantics=("arbitrary", "arbitrary"),
            vmem_limit_bytes=VMEM_LIMIT_BYTES),
        name="token_mixer",
    )(x, mod, n1w, cos_tab, sin_tab, w_in, sinks, ln_w, ln_b, sgu_w, sgu_b_full,
      proj_a, proj_b, w_out, *cast_weights)


CONV_PAD_ROWS = SUBLANES


FFN_N_INPUTS = 9


def _ffn_kernel(*refs, n_cast, final_norm):
    (x_ref, mod_ref, n2w_ref, wg_ref, wu_ref, cw_ref, cb_ref, wd_ref, fnw_ref) = refs[:FFN_N_INPUTS]
    o_ref = refs[FFN_N_INPUTS + n_cast]
    a_s, carry_s, act_s = refs[FFN_N_INPUTS + 2 * n_cast + 1:]
    _cast_slabs(refs[FFN_N_INPUTS:FFN_N_INPUTS + n_cast],
                refs[FFN_N_INPUTS + n_cast + 1:FFN_N_INPUTS + 2 * n_cast + 1])
    tm = FFN_TOKEN_TILE
    j = pl.program_id(1)

    @pl.when(j == 0)
    def _():
        carry_s[...] = jnp.zeros((CONV_PAD_ROWS, FFN_DIM), F32)

    scale2 = 1.0 + mod_ref[0, 4:5, :]
    hb_chunks, a_chunks, up_chunks = [], [], []
    for r in range(tm // NORM_ROW_CHUNK):
        xr = x_ref[0, r * NORM_ROW_CHUNK:(r + 1) * NORM_ROW_CHUNK, :]
        hr = (_rms_norm(xr, n2w_ref[...]) * scale2 + mod_ref[0, 3:4, :]).astype(BF16)
        hb_chunks.append(hr)
        a_chunks.append(_dot(hr, wg_ref[:, 0:MXU_COLS]))
        up_chunks.append(_dot(hr, wu_ref[:, 0:MXU_COLS]))
    hb = jnp.concatenate(hb_chunks, axis=0)
    first_a = jnp.concatenate(a_chunks, axis=0)
    first_up = jnp.concatenate(up_chunks, axis=0)

    for c in range(FFN_DIM // MXU_COLS):
        cols = slice(c * MXU_COLS, (c + 1) * MXU_COLS)
        a = first_a if c == 0 else _dot(hb, wg_ref[:, cols])
        stage = a_s.at[c % 2]
        stage[0:CONV_PAD_ROWS, :] = carry_s[:, cols]
        stage[CONV_PAD_ROWS:CONV_PAD_ROWS + tm, :] = a
        conv = cb_ref[:, cols] + cw_ref[2:3, cols] * a
        for tap in range(CONV_WIDTH - 1):
            back = CONV_WIDTH - 1 - tap
            conv = conv + cw_ref[tap:tap + 1, cols] * stage[CONV_PAD_ROWS - back:CONV_PAD_ROWS - back + tm, :]
        carry_s[:, cols] = stage[tm:tm + CONV_PAD_ROWS, :]
        up = first_up if c == 0 else _dot(hb, wu_ref[:, cols])
        act_s[:, cols] = (conv * jax.nn.sigmoid(conv) * up).astype(BF16)

    for r in range(tm // OUT_ROW_CHUNK):
        rows = slice(r * OUT_ROW_CHUNK, (r + 1) * OUT_ROW_CHUNK)
        y = x_ref[0, rows, :] + mod_ref[0, 5:6, :] * _dot(act_s[rows, :], wd_ref[...])
        if final_norm:
            y = _rms_norm(y, fnw_ref[...])
        o_ref[0, rows, :] = y


def _ffn_layer(x, mod, n2w, w_gate, w_up, conv_w, conv_b, w_down, final_w, final_norm,
               cast_weights, layer):
    b, s, d = x.shape
    tm = FFN_TOKEN_TILE
    tile = pl.BlockSpec((1, tm, d), lambda i, j: (i, j, 0))
    grid = (b, s // tm)
    cast_in, cast_out, cast_shapes = _cast_specs(cast_weights, layer, grid)
    return pl.pallas_call(
        functools.partial(_ffn_kernel, n_cast=len(cast_weights), final_norm=final_norm),
        grid=grid,
        in_specs=[
            tile,
            pl.BlockSpec((1, 6, d), lambda i, j: (i, 0, 0)),
            _full((1, d)),
            _full((d, FFN_DIM)), _full((d, FFN_DIM)),
            _full((CONV_WIDTH, FFN_DIM)), _full((1, FFN_DIM)),
            _full((FFN_DIM, d)),
            _full((1, d)),
        ] + cast_in,
        out_specs=[tile] + cast_out,
        out_shape=[jax.ShapeDtypeStruct(x.shape, x.dtype)] + cast_shapes,
        scratch_shapes=[
            pltpu.VMEM((2, CONV_PAD_ROWS + tm, MXU_COLS), F32),
            pltpu.VMEM((CONV_PAD_ROWS, FFN_DIM), F32),
            pltpu.VMEM((tm, FFN_DIM), BF16),
        ],
        compiler_params=pltpu.CompilerParams(
            dimension_semantics=("arbitrary", "arbitrary"),
            vmem_limit_bytes=VMEM_LIMIT_BYTES),
        name="conv_ffn",
    )(x, mod, n2w, w_gate, w_up, conv_w, conv_b, w_down, final_w, *cast_weights)


def kernel(x, c, positions, ada_w, ada_b, norm1_w, w_in, attn_sinks, sgu_ln_w, sgu_ln_b, sgu_w, sgu_b,
           proj_a, proj_b, w_out, norm2_w, ffn_w_gate, ffn_w_up, ffn_conv_w, ffn_conv_b, ffn_w_down,
           final_norm_w):
    batch, seq, d = x.shape
    depth = ada_w.shape[0]
    assert seq % TOKEN_TILE == 0 and seq % FFN_TOKEN_TILE == 0 and TOKEN_TILE % ATTN_BLOCK == 0
    assert d == D_MODEL

    mix_f32 = (w_in, proj_a, proj_b, w_out)
    ffn_f32 = (ffn_w_gate, ffn_w_up, ffn_w_down)
    rows = -(-batch // SUBLANES) * SUBLANES
    c_pad = jnp.pad(c, ((0, rows - batch), (0, 0)))
    mod, cos_tab, sin_tab, *mix_w = _prepare(c_pad, ada_w, ada_b, positions, mix_f32)
    mod = mod[:, :batch]

    for l in range(depth):
        mod_l = mod[l].reshape(batch, 6, d)
        sgu_b_full = jnp.repeat(sgu_b[l].T, SGU_WIDTH // SGU_GROUPS, axis=1)
        x, *ffn_w = _mix_layer(
            x, mod_l, norm1_w[l].reshape(1, d), cos_tab, sin_tab, mix_w[0],
            attn_sinks[l], sgu_ln_w[l].reshape(1, -1), sgu_ln_b[l].reshape(1, -1),
            sgu_w[l], sgu_b_full, mix_w[1], mix_w[2], mix_w[3], ffn_f32, l)
        last = l == depth - 1
        x, *mix_w = _ffn_layer(
            x, mod_l, norm2_w[l].reshape(1, d), ffn_w[0], ffn_w[1],
            ffn_conv_w[l], ffn_conv_b[l].reshape(1, -1), ffn_w[2],
            final_norm_w.reshape(1, d), last, () if last else mix_f32, l + 1)
    return x
```

```python
import functools

import jax
import jax.numpy as jnp
from jax import lax
from jax.experimental import pallas as pl
from jax.experimental.pallas import tpu as pltpu

D_MODEL = 1024
N_Q_HEADS = 16
N_KV_HEADS = 2
HEAD_DIM = 64
Q_PER_KV = N_Q_HEADS // N_KV_HEADS
ATTN_BLOCK = 128
ROPE_THETA = 500000.0
ROT_DIM = HEAD_DIM // 4
SGU_WIDTH = 1024
SGU_GROUPS = 8
SGU_CHUNK = 128
FFN_DIM = 2816
CONV_WIDTH = 3
NORM_EPS = 1e-6

Q_END = N_Q_HEADS * HEAD_DIM
K_END = Q_END + N_KV_HEADS * HEAD_DIM
V_END = K_END + N_KV_HEADS * HEAD_DIM
Z_END = V_END + 2 * SGU_WIDTH
IN_COLS = Z_END + 2 * D_MODEL

LANES = 128
SUBLANES = 8
MXU_COLS = 256
TOKEN_TILE = 512
FFN_TOKEN_TILE = 1024
NORM_ROW_CHUNK = 128
OUT_ROW_CHUNK = 256
ATTN_LOOKAHEAD = 2
VMEM_LIMIT_BYTES = 56 * 1024 * 1024
MASK_VALUE = -1e30
LOG2E = 1.4426950408889634

F32 = jnp.float32
BF16 = jnp.bfloat16


def _rms_norm(x, w):
    ms = jnp.mean(x * x, axis=-1, keepdims=True)
    return x * lax.rsqrt(ms + NORM_EPS) * w


def _dot(a, b):
    return jnp.dot(a, b, preferred_element_type=F32)


BF16_ROW_TILE = 16


def _cast_specs(stacked_weights, layer, grid):
    steps = grid[0] * grid[1]
    specs, shapes = [], []
    for w in stacked_weights:
        _, rows, cols = w.shape
        slab, share = rows // steps, 1
        while slab % BF16_ROW_TILE:
            slab, share = slab * 2, share * 2
        assert rows % slab == 0 and steps % share == 0
        idx = lambda i, j, share=share: ((i * grid[1] + j) // share, 0)
        specs.append((pl.BlockSpec((None, slab, cols), lambda i, j, idx=idx: (layer,) + idx(i, j)),
                      pl.BlockSpec((slab, cols), idx)))
        shapes.append(jax.ShapeDtypeStruct((rows, cols), BF16))
    return [s[0] for s in specs], [s[1] for s in specs], shapes


def _cast_slabs(src_refs, dst_refs):
    for src, dst in zip(src_refs, dst_refs):
        dst[...] = src[...].astype(BF16)


ROPE_ROW_TILE = 1024


def _split_bf16(v):
    hi = v.astype(BF16)
    return hi, (v - hi.astype(F32)).astype(BF16)


def _prep_kernel(c_ref, adaw_ref, adab_ref, pos_ref, invf_ref, *refs):
    n_cast = (len(refs) - 3) // 2
    mod_ref, cos_ref, sin_ref = refs[n_cast:n_cast + 3]
    _cast_slabs(refs[:n_cast], refs[n_cast + 3:])

    c = c_ref[...]
    rows = c.shape[0]
    c_hi, c_lo = _split_bf16(c * jax.nn.sigmoid(c))
    lhs = jnp.concatenate([c_hi, c_lo], axis=0)
    w_hi, w_lo = _split_bf16(adaw_ref[...])
    parts = _dot(lhs, w_hi) + _dot(lhs, w_lo)
    mod_ref[...] = parts[0:rows] + parts[rows:2 * rows] + adab_ref[...]

    row_i = lax.broadcasted_iota(jnp.int32, (LANES, LANES), 0)
    lane_i = lax.broadcasted_iota(jnp.int32, (LANES, LANES), 1)
    diag = row_i == lane_i
    head_lane = lane_i & (HEAD_DIM - 1)
    freq_idx = head_lane & (ROT_DIM // 2 - 1)
    inv_freq = jnp.zeros((LANES, LANES), F32)
    for f in range(ROT_DIM // 2):
        inv_freq = jnp.where(freq_idx == f, invf_ref[f], inv_freq)
    for r in range(pos_ref.shape[1]):
        pos_row = pos_ref[0, r:r + 1, :].astype(F32)
        pos_col = jnp.sum(jnp.where(diag, pos_row, 0.0), axis=1, keepdims=True)
        ang = pos_col * inv_freq
        rows_r = slice(r * LANES, (r + 1) * LANES)
        cos_ref[0, rows_r, :] = jnp.where(head_lane < ROT_DIM, jnp.cos(ang), 1.0)
        s = jnp.sin(ang)
        sin_ref[0, rows_r, :] = jnp.where(head_lane < ROT_DIM // 2, -s,
                                          jnp.where(head_lane < ROT_DIM, s, 0.0))


def _prepare(c, ada_w, ada_b, positions, cast_weights):
    b, s = positions.shape
    depth, d, n = ada_w.shape
    rows = c.shape[0]
    inv_freq = ROPE_THETA ** (-jnp.arange(0, ROT_DIM, 2, dtype=F32) / ROT_DIM)
    tab = jax.ShapeDtypeStruct((b, s, LANES), F32)
    grid = (b, s // ROPE_ROW_TILE)
    steps = grid[0] * grid[1]
    assert steps % depth == 0 and n % (steps // depth) == 0
    col_tiles = steps // depth
    ada_cols = n // col_tiles
    ada_idx = lambda i, j: ((i * grid[1] + j) // col_tiles, 0, (i * grid[1] + j) % col_tiles)
    cast_in, cast_out, cast_shapes = _cast_specs(cast_weights, 0, grid)
    return pl.pallas_call(
        _prep_kernel,
        grid=grid,
        in_specs=[
            pl.BlockSpec((rows, d), lambda i, j: (0, 0)),
            pl.BlockSpec((None, d, ada_cols), ada_idx),
            pl.BlockSpec((None, 1, ada_cols), ada_idx),
            pl.BlockSpec((1, ROPE_ROW_TILE // LANES, LANES), lambda i, j: (i, j, 0)),
            pl.BlockSpec(memory_space=pltpu.SMEM),
        ] + cast_in,
        out_specs=[pl.BlockSpec((None, rows, ada_cols), ada_idx)]
        + [pl.BlockSpec((1, ROPE_ROW_TILE, LANES), lambda i, j: (i, j, 0))] * 2 + cast_out,
        out_shape=[jax.ShapeDtypeStruct((depth, rows, n), F32), tab, tab] + cast_shapes,
        compiler_params=pltpu.CompilerParams(
            dimension_semantics=("arbitrary", "arbitrary"),
            vmem_limit_bytes=VMEM_LIMIT_BYTES),
        name="prepare",
    )(c, ada_w, ada_b.reshape(depth, 1, n), positions.reshape(b, s // LANES, LANES), inv_freq,
      *cast_weights)


MIX_N_INPUTS = 14


def _mix_kernel(*refs, n_cast):
    (x_ref, mod_ref, n1w_ref, cos_ref, sin_ref, win_ref, sink_ref, lnw_ref, lnb_ref,
     sguw_ref, sgub_ref, pa_ref, pb_ref, wo_ref) = refs[:MIX_N_INPUTS]
    o_ref = refs[MIX_N_INPUTS + n_cast]
    q_s, k4_s, v4_s, bias_s, yattn_s, ysgu_s = refs[MIX_N_INPUTS + 2 * n_cast + 1:]
    _cast_slabs(refs[MIX_N_INPUTS:MIX_N_INPUTS + n_cast],
                refs[MIX_N_INPUTS + n_cast + 1:MIX_N_INPUTS + 2 * n_cast + 1])
    tm = TOKEN_TILE
    nblk = tm // ATTN_BLOCK
    band_keys = 2 * ATTN_BLOCK
    j = pl.program_id(1)

    @pl.when(j == 0)
    def _():
        k4_s[:, 0:ATTN_BLOCK, :] = jnp.zeros((4, ATTN_BLOCK, LANES), BF16)
        v4_s[:, 0:ATTN_BLOCK, :] = jnp.zeros((4, ATTN_BLOCK, LANES), BF16)

    scale1 = 1.0 + mod_ref[0, 1:2, :]
    hb_chunks, q_chunks = [], []
    for r in range(tm // NORM_ROW_CHUNK):
        xr = x_ref[0, r * NORM_ROW_CHUNK:(r + 1) * NORM_ROW_CHUNK, :]
        hr = (_rms_norm(xr, n1w_ref[...]) * scale1 + mod_ref[0, 0:1, :]).astype(BF16)
        hb_chunks.append(hr)
        q_chunks.append(_dot(hr, win_ref[:, 0:Q_END]))
    hb = jnp.concatenate(hb_chunks, axis=0)
    q = jnp.concatenate(q_chunks, axis=0)
    kv = _dot(hb, win_ref[:, Q_END:V_END])
    z = _dot(hb, win_ref[:, V_END:Z_END])

    cos_t = cos_ref[0]
    sin_t = sin_ref[0]
    lane = lax.broadcasted_iota(jnp.int32, (tm, LANES), 1)
    first_half = (lane & (HEAD_DIM - 1)) < (ROT_DIM // 2)
    left_head = lane < HEAD_DIM

    def rope(t):
        partner = jnp.where(first_half,
                            pltpu.roll(t, LANES - ROT_DIM // 2, 1),
                            pltpu.roll(t, ROT_DIM // 2, 1))
        return t * cos_t + partner * sin_t

    for c in range(Q_END // LANES):
        qc = rope(q[:, c * LANES:(c + 1) * LANES]) * (HEAD_DIM ** -0.5 * LOG2E)
        q_s[:, c * LANES:(c + 1) * LANES] = qc.astype(BF16)

    k = rope(kv[:, 0:LANES])
    v = kv[:, LANES:2 * LANES]
    for src, dst in ((k, k4_s), (v, v4_s)):
        swapped = pltpu.roll(src, HEAD_DIM, 1)
        zero = jnp.zeros_like(src)
        rows = slice(ATTN_BLOCK, ATTN_BLOCK + tm)
        dst[0, rows, :] = jnp.where(left_head, src, zero).astype(BF16)
        dst[1, rows, :] = jnp.where(left_head, zero, swapped).astype(BF16)
        dst[2, rows, :] = jnp.where(left_head, swapped, zero).astype(BF16)
        dst[3, rows, :] = jnp.where(left_head, zero, src).astype(BF16)

    gates = _dot(hb, win_ref[:, Z_END:IN_COLS])

    z = 0.5 * z * (1.0 + lax.erf(z * (2.0 ** -0.5)))
    u = z[:, 0:SGU_WIDTH]
    v_sgu = z[:, SGU_WIDTH:]
    mu = jnp.mean(v_sgu, axis=-1, keepdims=True)
    dv = v_sgu - mu
    var = jnp.mean(dv * dv, axis=-1, keepdims=True)
    vn = (dv * lax.rsqrt(var + NORM_EPS) * lnw_ref[...] + lnb_ref[...]).astype(BF16)
    ti = lax.broadcasted_iota(jnp.int32, (SGU_CHUNK, SGU_CHUNK), 0)
    si = lax.broadcasted_iota(jnp.int32, (SGU_CHUNK, SGU_CHUNK), 1)
    causal = si <= ti
    gdim = SGU_WIDTH // SGU_GROUPS
    nchunk = tm // SGU_CHUNK
    for g in range(SGU_GROUPS):
        cols = slice(g * gdim, (g + 1) * gdim)
        wm = jnp.where(causal, sguw_ref[g], 0.0).astype(BF16)
        rhs = jnp.concatenate(
            [vn[c * SGU_CHUNK:(c + 1) * SGU_CHUNK, cols] for c in range(nchunk)], axis=1)
        f = _dot(wm, rhs)
        for c in range(nchunk):
            rows = slice(c * SGU_CHUNK, (c + 1) * SGU_CHUNK)
            fc = f[:, c * gdim:(c + 1) * gdim] + sgub_ref[:, cols]
            ysgu_s[rows, cols] = (u[rows, cols] * fc).astype(BF16)

    sgu_part = jax.nn.sigmoid(gates[:, 0:D_MODEL]) * _dot(ysgu_s[...], pa_ref[...])
    gate_b = jax.nn.sigmoid(gates[:, D_MODEL:])

    er = lax.broadcasted_iota(jnp.int32, (ATTN_BLOCK, LANES), 0)
    ec = lax.broadcasted_iota(jnp.int32, (ATTN_BLOCK, LANES), 1)
    eye = jnp.where(er == ec, 1.0, 0.0).astype(BF16)
    kr = lax.broadcasted_iota(jnp.int32, (band_keys, ATTN_BLOCK), 0)
    qc_i = lax.broadcasted_iota(jnp.int32, (band_keys, ATTN_BLOCK), 1)
    band_t = (kr > qc_i) & (kr <= qc_i + ATTN_BLOCK)
    for idx, ok in ((0, band_t & (kr >= ATTN_BLOCK)), (1, band_t)):
        bias_t = jnp.where(ok, 0.0, MASK_VALUE).astype(BF16)
        bias_s[idx, 0:band_keys, :] = bias_t
        bias_s[idx, band_keys:2 * band_keys, :] = bias_t
    sr = lax.broadcasted_iota(jnp.int32, (2 * band_keys, LANES), 0)
    sc = lax.broadcasted_iota(jnp.int32, (2 * band_keys, LANES), 1)
    ones_sel = jnp.where((sr < band_keys) == (sc < HEAD_DIM), 1.0, 0.0).astype(BF16)
    sink_value_row = (sr == 0) | (sr == band_keys)
    sink_lane = ec == 0
    contract_last = (((1,), (1,)), ((), ()))

    def score_matmul(b, g):
        r0 = b * ATTN_BLOCK
        bias_t = bias_s[jnp.where(j > 0, 1, 0)] if b == 0 else bias_s[1]
        kbd = jnp.concatenate([k4_s[2 * g, pl.ds(r0, band_keys), :],
                               k4_s[2 * g + 1, pl.ds(r0, band_keys), :]], axis=0)
        kbd = jnp.concatenate([kbd, bias_t], axis=1)
        qrows = []
        for pr in range(Q_PER_KV // 2):
            col = (g * Q_PER_KV + 2 * pr) * HEAD_DIM
            qrows.append(jnp.concatenate([q_s[pl.ds(r0, ATTN_BLOCK), col:col + LANES], eye], axis=1))
        return lax.dot_general(jnp.concatenate(qrows, axis=0), kbd, contract_last,
                               preferred_element_type=F32)

    def softmax_value_matmul(b, g, scores):
        r0 = b * ATTN_BLOCK
        vbd = jnp.concatenate([v4_s[2 * g, pl.ds(r0, band_keys), :],
                               v4_s[2 * g + 1, pl.ds(r0, band_keys), :]], axis=0)
        vbd = jnp.where(sink_value_row, jnp.zeros_like(vbd), vbd)
        value_op = jnp.concatenate([vbd, ones_sel], axis=1)
        probs = []
        for pr in range(Q_PER_KV // 2):
            rows = slice(pr * ATTN_BLOCK, (pr + 1) * ATTN_BLOCK)
            pair_probs = []
            for hh in range(2):
                c0 = hh * band_keys
                sink = sink_ref[g * Q_PER_KV + 2 * pr + hh] * LOG2E
                first = jnp.where(sink_lane, sink, scores[rows, c0:c0 + LANES])
                rest = scores[rows, c0 + LANES:c0 + band_keys]
                m = jnp.max(jnp.maximum(first, rest), axis=-1, keepdims=True)
                pair_probs += [jnp.exp2(first - m).astype(BF16), jnp.exp2(rest - m).astype(BF16)]
            probs.append(jnp.concatenate(pair_probs, axis=1))
        o = _dot(jnp.concatenate(probs, axis=0), value_op)
        for pr in range(Q_PER_KV // 2):
            rows = slice(pr * ATTN_BLOCK, (pr + 1) * ATTN_BLOCK)
            col = (g * Q_PER_KV + 2 * pr) * HEAD_DIM
            yattn_s[pl.ds(r0, ATTN_BLOCK), col:col + LANES] = (
                o[rows, 0:LANES] / o[rows, LANES:]).astype(BF16)

    units = [(b, g) for b in range(nblk) for g in range(N_KV_HEADS)]
    pending = [score_matmul(*u) for u in units[:ATTN_LOOKAHEAD]]
    for i, (b, g) in enumerate(units):
        if i + ATTN_LOOKAHEAD < len(units):
            pending.append(score_matmul(*units[i + ATTN_LOOKAHEAD]))
        softmax_value_matmul(b, g, pending.pop(0))

    k4_s[:, 0:ATTN_BLOCK, :] = k4_s[:, tm:tm + ATTN_BLOCK, :]
    v4_s[:, 0:ATTN_BLOCK, :] = v4_s[:, tm:tm + ATTN_BLOCK, :]

    merged = (sgu_part + gate_b * _dot(yattn_s[...], pb_ref[...])).astype(BF16)
    for r in range(tm // OUT_ROW_CHUNK):
        rows = slice(r * OUT_ROW_CHUNK, (r + 1) * OUT_ROW_CHUNK)
        o_ref[0, rows, :] = x_ref[0, rows, :] + mod_ref[0, 2:3, :] * _dot(merged[rows, :], wo_ref[...])


def _full(shape):
    zeros = (0,) * len(shape)
    return pl.BlockSpec(shape, lambda i, j: zeros)


def _mix_layer(x, mod, n1w, cos_tab, sin_tab, w_in, sinks, ln_w, ln_b, sgu_w, sgu_b_full,
               proj_a, proj_b, w_out, cast_weights, layer):
    b, s, d = x.shape
    tm = TOKEN_TILE
    tile = pl.BlockSpec((1, tm, d), lambda i, j: (i, j, 0))
    tab = pl.BlockSpec((1, tm, LANES), lambda i, j: (i, j, 0))
    grid = (b, s // tm)
    cast_in, cast_out, cast_shapes = _cast_specs(cast_weights, layer, grid)
    return pl.pallas_call(
        functools.partial(_mix_kernel, n_cast=len(cast_weights)),
        grid=grid,
        in_specs=[
            tile,
            pl.BlockSpec((1, 6, d), lambda i, j: (i, 0, 0)),
            _full((1, d)),
            tab, tab,
            _full((d, IN_COLS)),
            pl.BlockSpec(memory_space=pltpu.SMEM),
            _full((1, SGU_WIDTH)), _full((1, SGU_WIDTH)),
            _full((SGU_GROUPS, SGU_CHUNK, SGU_CHUNK)),
            _full((SGU_CHUNK, SGU_WIDTH)),
            _full((SGU_WIDTH, d)), _full((Q_END, d)), _full((d, d)),
        ] + cast_in,
        out_specs=[tile] + cast_out,
        out_shape=[jax.ShapeDtypeStruct(x.shape, x.dtype)] + cast_shapes,
        scratch_shapes=[
            pltpu.VMEM((tm, Q_END), BF16),
            pltpu.VMEM((4, tm + ATTN_BLOCK, LANES), BF16),
            pltpu.VMEM((4, tm + ATTN_BLOCK, LANES), BF16),
            pltpu.VMEM((2, 4 * ATTN_BLOCK, LANES), BF16),
            pltpu.VMEM((tm, Q_END), BF16),
            pltpu.VMEM((tm, SGU_WIDTH), BF16),
        ],
        compiler_params=pltpu.CompilerParams(
            dimension_semantics=("arbitrary", "arbitrary"),
            vmem_limit_bytes=VMEM_LIMIT_BYTES),
        name="token_mixer",
    )(x, mod, n1w, cos_tab, sin_tab, w_in, sinks, ln_w, ln_b, sgu_w, sgu_b_full,
      proj_a, proj_b, w_out, *cast_weights)


CONV_PAD_ROWS = SUBLANES


FFN_N_INPUTS = 9


def _ffn_kernel(*refs, n_cast, final_norm):
    (x_ref, mod_ref, n2w_ref, wg_ref, wu_ref, cw_ref, cb_ref, wd_ref, fnw_ref) = refs[:FFN_N_INPUTS]
    o_ref = refs[FFN_N_INPUTS + n_cast]
    a_s, carry_s, act_s = refs[FFN_N_INPUTS + 2 * n_cast + 1:]
    _cast_slabs(refs[FFN_N_INPUTS:FFN_N_INPUTS + n_cast],
                refs[FFN_N_INPUTS + n_cast + 1:FFN_N_INPUTS + 2 * n_cast + 1])
    tm = FFN_TOKEN_TILE
    j = pl.program_id(1)

    @pl.when(j == 0)
    def _():
        carry_s[...] = jnp.zeros((CONV_PAD_ROWS, FFN_DIM), F32)

    scale2 = 1.0 + mod_ref[0, 4:5, :]
    hb_chunks, a_chunks, up_chunks = [], [], []
    for r in range(tm // NORM_ROW_CHUNK):
        xr = x_ref[0, r * NORM_ROW_CHUNK:(r + 1) * NORM_ROW_CHUNK, :]
        hr = (_rms_norm(xr, n2w_ref[...]) * scale2 + mod_ref[0, 3:4, :]).astype(BF16)
        hb_chunks.append(hr)
        a_chunks.append(_dot(hr, wg_ref[:, 0:MXU_COLS]))
        up_chunks.append(_dot(hr, wu_ref[:, 0:MXU_COLS]))
    hb = jnp.concatenate(hb_chunks, axis=0)
    first_a = jnp.concatenate(a_chunks, axis=0)
    first_up = jnp.concatenate(up_chunks, axis=0)

    for c in range(FFN_DIM // MXU_COLS):
        cols = slice(c * MXU_COLS, (c + 1) * MXU_COLS)
        a = first_a if c == 0 else _dot(hb, wg_ref[:, cols])
        stage = a_s.at[c % 2]
        stage[0:CONV_PAD_ROWS, :] = carry_s[:, cols]
        stage[CONV_PAD_ROWS:CONV_PAD_ROWS + tm, :] = a
        conv = cb_ref[:, cols] + cw_ref[2:3, cols] * a
        for tap in range(CONV_WIDTH - 1):
            back = CONV_WIDTH - 1 - tap
            conv = conv + cw_ref[tap:tap + 1, cols] * stage[CONV_PAD_ROWS - back:CONV_PAD_ROWS - back + tm, :]
        carry_s[:, cols] = stage[tm:tm + CONV_PAD_ROWS, :]
        up = first_up if c == 0 else _dot(hb, wu_ref[:, cols])
        act_s[:, cols] = (conv * jax.nn.sigmoid(conv) * up).astype(BF16)

    for r in range(tm // OUT_ROW_CHUNK):
        rows = slice(r * OUT_ROW_CHUNK, (r + 1) * OUT_ROW_CHUNK)
        y = x_ref[0, rows, :] + mod_ref[0, 5:6, :] * _dot(act_s[rows, :], wd_ref[...])
        if final_norm:
            y = _rms_norm(y, fnw_ref[...])
        o_ref[0, rows, :] = y


def _ffn_layer(x, mod, n2w, w_gate, w_up, conv_w, conv_b, w_down, final_w, final_norm,
               cast_weights, layer):
    b, s, d = x.shape
    tm = FFN_TOKEN_TILE
    tile = pl.BlockSpec((1, tm, d), lambda i, j: (i, j, 0))
    grid = (b, s // tm)
    cast_in, cast_out, cast_shapes = _cast_specs(cast_weights, layer, grid)
    return pl.pallas_call(
        functools.partial(_ffn_kernel, n_cast=len(cast_weights), final_norm=final_norm),
        grid=grid,
        in_specs=[
            tile,
            pl.BlockSpec((1, 6, d), lambda i, j: (i, 0, 0)),
            _full((1, d)),
            _full((d, FFN_DIM)), _full((d, FFN_DIM)),
            _full((CONV_WIDTH, FFN_DIM)), _full((1, FFN_DIM)),
            _full((FFN_DIM, d)),
            _full((1, d)),
        ] + cast_in,
        out_specs=[tile] + cast_out,
        out_shape=[jax.ShapeDtypeStruct(x.shape, x.dtype)] + cast_shapes,
        scratch_shapes=[
            pltpu.VMEM((2, CONV_PAD_ROWS + tm, MXU_COLS), F32),
            pltpu.VMEM((CONV_PAD_ROWS, FFN_DIM), F32),
            pltpu.VMEM((tm, FFN_DIM), BF16),
        ],
        compiler_params=pltpu.CompilerParams(
            dimension_semantics=("arbitrary", "arbitrary"),
            vmem_limit_bytes=VMEM_LIMIT_BYTES),
        name="conv_ffn",
    )(x, mod, n2w, w_gate, w_up, conv_w, conv_b, w_down, final_w, *cast_weights)


def kernel(x, c, positions, ada_w, ada_b, norm1_w, w_in, attn_sinks, sgu_ln_w, sgu_ln_b, sgu_w, sgu_b,
           proj_a, proj_b, w_out, norm2_w, ffn_w_gate, ffn_w_up, ffn_conv_w, ffn_conv_b, ffn_w_down,
           final_norm_w):
    batch, seq, d = x.shape
    depth = ada_w.shape[0]
    assert seq % TOKEN_TILE == 0 and seq % FFN_TOKEN_TILE == 0 and TOKEN_TILE % ATTN_BLOCK == 0
    assert d == D_MODEL

    mix_f32 = (w_in, proj_a, proj_b, w_out)
    ffn_f32 = (ffn_w_gate, ffn_w_up, ffn_w_down)
    rows = -(-batch // SUBLANES) * SUBLANES
    c_pad = jnp.pad(c, ((0, rows - batch), (0, 0)))
    mod, cos_tab, sin_tab, *mix_w = _prepare(c_pad, ada_w, ada_b, positions, mix_f32)
    mod = mod[:, :batch]

    for l in range(depth):
        mod_l = mod[l].reshape(batch, 6, d)
        sgu_b_full = jnp.repeat(sgu_b[l].T, SGU_WIDTH // SGU_GROUPS, axis=1)
        x, *ffn_w = _mix_layer(
            x, mod_l, norm1_w[l].reshape(1, d), cos_tab, sin_tab, mix_w[0],
            attn_sinks[l], sgu_ln_w[l].reshape(1, -1), sgu_ln_b[l].reshape(1, -1),
            sgu_w[l], sgu_b_full, mix_w[1], mix_w[2], mix_w[3], ffn_f32, l)
        last = l == depth - 1
        x, *mix_w = _ffn_layer(
            x, mod_l, norm2_w[l].reshape(1, d), ffn_w[0], ffn_w[1],
            ffn_conv_w[l], ffn_conv_b[l].reshape(1, -1), ffn_w[2],
            final_norm_w.reshape(1, d), last, () if last else mix_f32, l + 1)
    return x
```

```python
import functools

import jax
import jax.numpy as jnp
from jax import lax
from jax.experimental import pallas as pl
from jax.experimental.pallas import tpu as pltpu

D_MODEL = 1024
N_Q_HEADS = 16
N_KV_HEADS = 2
HEAD_DIM = 64
Q_PER_KV = N_Q_HEADS // N_KV_HEADS
ATTN_BLOCK = 128
ROPE_THETA = 500000.0
ROT_DIM = HEAD_DIM // 4
SGU_WIDTH = 1024
SGU_GROUPS = 8
SGU_CHUNK = 128
FFN_DIM = 2816
CONV_WIDTH = 3
NORM_EPS = 1e-6

Q_END = N_Q_HEADS * HEAD_DIM
K_END = Q_END + N_KV_HEADS * HEAD_DIM
V_END = K_END + N_KV_HEADS * HEAD_DIM
Z_END = V_END + 2 * SGU_WIDTH
IN_COLS = Z_END + 2 * D_MODEL

LANES = 128
SUBLANES = 8
MXU_COLS = 256
TOKEN_TILE = 512
FFN_TOKEN_TILE = 1024
NORM_ROW_CHUNK = 128
OUT_ROW_CHUNK = 256
ATTN_LOOKAHEAD = 2
VMEM_LIMIT_BYTES = 56 * 1024 * 1024
MASK_VALUE = -1e30
LOG2E = 1.4426950408889634

F32 = jnp.float32
BF16 = jnp.bfloat16


def _rms_norm(x, w):
    ms = jnp.mean(x * x, axis=-1, keepdims=True)
    return x * lax.rsqrt(ms + NORM_EPS) * w


def _mod_row(mod_ref, part):
    return mod_ref[pl.ds(pl.program_id(0), 1), part * D_MODEL:(part + 1) * D_MODEL]


def _dot(a, b):
    return jnp.dot(a, b, preferred_element_type=F32)


BF16_ROW_TILE = 16


def _cast_specs(stacked_weights, layer, grid):
    steps = grid[0] * grid[1]
    specs, shapes = [], []
    for w in stacked_weights:
        _, rows, cols = w.shape
        slab, share = rows // steps, 1
        while slab % BF16_ROW_TILE:
            slab, share = slab * 2, share * 2
        assert rows % slab == 0 and steps % share == 0
        idx = lambda i, j, share=share: ((i * grid[1] + j) // share, 0)
        specs.append((pl.BlockSpec((None, slab, cols), lambda i, j, idx=idx: (layer,) + idx(i, j)),
                      pl.BlockSpec((slab, cols), idx)))
        shapes.append(jax.ShapeDtypeStruct((rows, cols), BF16))
    return [s[0] for s in specs], [s[1] for s in specs], shapes


def _cast_slabs(src_refs, dst_refs):
    for src, dst in zip(src_refs, dst_refs):
        dst[...] = src[...].astype(BF16)


ROPE_ROW_TILE = 1024


def _split_bf16(v):
    hi = v.astype(BF16)
    return hi, (v - hi.astype(F32)).astype(BF16)


def _prep_kernel(c_ref, adaw_ref, adab_ref, pos_ref, invf_ref, *refs):
    n_cast = (len(refs) - 3) // 2
    mod_ref, cos_ref, sin_ref = refs[n_cast:n_cast + 3]
    _cast_slabs(refs[:n_cast], refs[n_cast + 3:])

    c = c_ref[...]
    rows = c.shape[0]
    c_hi, c_lo = _split_bf16(c * jax.nn.sigmoid(c))
    lhs = jnp.concatenate([c_hi, c_lo], axis=0)
    w_hi, w_lo = _split_bf16(adaw_ref[...])
    parts = _dot(lhs, w_hi) + _dot(lhs, w_lo)
    mod_ref[...] = parts[0:rows] + parts[rows:2 * rows] + adab_ref[...]

    row_i = lax.broadcasted_iota(jnp.int32, (LANES, LANES), 0)
    lane_i = lax.broadcasted_iota(jnp.int32, (LANES, LANES), 1)
    diag = row_i == lane_i
    head_lane = lane_i & (HEAD_DIM - 1)
    freq_idx = head_lane & (ROT_DIM // 2 - 1)
    inv_freq = jnp.zeros((LANES, LANES), F32)
    for f in range(ROT_DIM // 2):
        inv_freq = jnp.where(freq_idx == f, invf_ref[f], inv_freq)
    for r in range(pos_ref.shape[1]):
        pos_row = pos_ref[0, r:r + 1, :].astype(F32)
        pos_col = jnp.sum(jnp.where(diag, pos_row, 0.0), axis=1, keepdims=True)
        ang = pos_col * inv_freq
        rows_r = slice(r * LANES, (r + 1) * LANES)
        cos_ref[0, rows_r, :] = jnp.where(head_lane < ROT_DIM, jnp.cos(ang), 1.0)
        s = jnp.sin(ang)
        sin_ref[0, rows_r, :] = jnp.where(head_lane < ROT_DIM // 2, -s,
                                          jnp.where(head_lane < ROT_DIM, s, 0.0))


def _prepare(c, ada_w, ada_b, positions, cast_weights):
    b, s = positions.shape
    depth, d, n = ada_w.shape
    rows = c.shape[0]
    inv_freq = ROPE_THETA ** (-jnp.arange(0, ROT_DIM, 2, dtype=F32) / ROT_DIM)
    tab = jax.ShapeDtypeStruct((b, s, LANES), F32)
    grid = (b, s // ROPE_ROW_TILE)
    steps = grid[0] * grid[1]
    assert steps % depth == 0 and n % (steps // depth) == 0
    col_tiles = steps // depth
    ada_cols = n // col_tiles
    ada_idx = lambda i, j: ((i * grid[1] + j) // col_tiles, 0, (i * grid[1] + j) % col_tiles)
    cast_in, cast_out, cast_shapes = _cast_specs(cast_weights, 0, grid)
    return pl.pallas_call(
        _prep_kernel,
        grid=grid,
        in_specs=[
            pl.BlockSpec((rows, d), lambda i, j: (0, 0)),
            pl.BlockSpec((None, d, ada_cols), ada_idx),
            pl.BlockSpec((None, 1, ada_cols), ada_idx),
            pl.BlockSpec((1, ROPE_ROW_TILE // LANES, LANES), lambda i, j: (i, j, 0)),
            pl.BlockSpec(memory_space=pltpu.SMEM),
        ] + cast_in,
        out_specs=[pl.BlockSpec((None, rows, ada_cols), ada_idx)]
        + [pl.BlockSpec((1, ROPE_ROW_TILE, LANES), lambda i, j: (i, j, 0))] * 2 + cast_out,
        out_shape=[jax.ShapeDtypeStruct((depth, rows, n), F32), tab, tab] + cast_shapes,
        compiler_params=pltpu.CompilerParams(
            dimension_semantics=("arbitrary", "arbitrary"),
            vmem_limit_bytes=VMEM_LIMIT_BYTES),
        name="prepare",
    )(c, ada_w, ada_b.reshape(depth, 1, n), positions.reshape(b, s // LANES, LANES), inv_freq,
      *cast_weights)


MIX_N_INPUTS = 14


def _mix_kernel(*refs, n_cast):
    (x_ref, mod_ref, n1w_ref, cos_ref, sin_ref, win_ref, sink_ref, lnw_ref, lnb_ref,
     sguw_ref, sgub_ref, pa_ref, pb_ref, wo_ref) = refs[:MIX_N_INPUTS]
    o_ref = refs[MIX_N_INPUTS + n_cast]
    q_s, k4_s, v4_s, bias_s, yattn_s, ysgu_s = refs[MIX_N_INPUTS + 2 * n_cast + 1:]
    _cast_slabs(refs[MIX_N_INPUTS:MIX_N_INPUTS + n_cast],
                refs[MIX_N_INPUTS + n_cast + 1:MIX_N_INPUTS + 2 * n_cast + 1])
    tm = TOKEN_TILE
    nblk = tm // ATTN_BLOCK
    band_keys = 2 * ATTN_BLOCK
    j = pl.program_id(1)

    @pl.when(j == 0)
    def _():
        k4_s[:, 0:ATTN_BLOCK, :] = jnp.zeros((4, ATTN_BLOCK, LANES), BF16)
        v4_s[:, 0:ATTN_BLOCK, :] = jnp.zeros((4, ATTN_BLOCK, LANES), BF16)

    norm_scale = n1w_ref[...] * (1.0 + _mod_row(mod_ref, 1))
    shift1 = _mod_row(mod_ref, 0)
    hb_chunks, q_chunks = [], []
    for r in range(tm // NORM_ROW_CHUNK):
        xr = x_ref[0, r * NORM_ROW_CHUNK:(r + 1) * NORM_ROW_CHUNK, :]
        hr = (_rms_norm(xr, norm_scale) + shift1).astype(BF16)
        hb_chunks.append(hr)
        q_chunks.append(_dot(hr, win_ref[:, 0:Q_END]))
    hb = jnp.concatenate(hb_chunks, axis=0)
    q = jnp.concatenate(q_chunks, axis=0)
    kv = _dot(hb, win_ref[:, Q_END:V_END])
    z = _dot(hb, win_ref[:, V_END:Z_END])

    cos_t = cos_ref[0]
    sin_t = sin_ref[0]
    lane = lax.broadcasted_iota(jnp.int32, (tm, LANES), 1)
    first_half = (lane & (HEAD_DIM - 1)) < (ROT_DIM // 2)
    left_head = lane < HEAD_DIM

    def rope(t):
        partner = jnp.where(first_half,
                            pltpu.roll(t, LANES - ROT_DIM // 2, 1),
                            pltpu.roll(t, ROT_DIM // 2, 1))
        return t * cos_t + partner * sin_t

    for c in range(Q_END // LANES):
        qc = rope(q[:, c * LANES:(c + 1) * LANES]) * (HEAD_DIM ** -0.5 * LOG2E)
        q_s[:, c * LANES:(c + 1) * LANES] = qc.astype(BF16)

    k = rope(kv[:, 0:LANES])
    v = kv[:, LANES:2 * LANES]
    for src, dst in ((k, k4_s), (v, v4_s)):
        swapped = pltpu.roll(src, HEAD_DIM, 1)
        zero = jnp.zeros_like(src)
        rows = slice(ATTN_BLOCK, ATTN_BLOCK + tm)
        dst[0, rows, :] = jnp.where(left_head, src, zero).astype(BF16)
        dst[1, rows, :] = jnp.where(left_head, zero, swapped).astype(BF16)
        dst[2, rows, :] = jnp.where(left_head, swapped, zero).astype(BF16)
        dst[3, rows, :] = jnp.where(left_head, zero, src).astype(BF16)

    gates = _dot(hb, win_ref[:, Z_END:IN_COLS])

    z = 0.5 * z * (1.0 + lax.erf(z * (2.0 ** -0.5)))
    u = z[:, 0:SGU_WIDTH]
    v_sgu = z[:, SGU_WIDTH:]
    mu = jnp.mean(v_sgu, axis=-1, keepdims=True)
    dv = v_sgu - mu
    var = jnp.mean(dv * dv, axis=-1, keepdims=True)
    vn = (dv * lax.rsqrt(var + NORM_EPS) * lnw_ref[...] + lnb_ref[...]).astype(BF16)
    ti = lax.broadcasted_iota(jnp.int32, (SGU_CHUNK, SGU_CHUNK), 0)
    si = lax.broadcasted_iota(jnp.int32, (SGU_CHUNK, SGU_CHUNK), 1)
    causal = si <= ti
    gdim = SGU_WIDTH // SGU_GROUPS
    nchunk = tm // SGU_CHUNK
    for g in range(SGU_GROUPS):
        cols = slice(g * gdim, (g + 1) * gdim)
        wm = jnp.where(causal, sguw_ref[g], 0.0).astype(BF16)
        rhs = jnp.concatenate(
            [vn[c * SGU_CHUNK:(c + 1) * SGU_CHUNK, cols] for c in range(nchunk)], axis=1)
        f = _dot(wm, rhs)
        for c in range(nchunk):
            rows = slice(c * SGU_CHUNK, (c + 1) * SGU_CHUNK)
            fc = f[:, c * gdim:(c + 1) * gdim] + sgub_ref[:, cols]
            ysgu_s[rows, cols] = (u[rows, cols] * fc).astype(BF16)

    sgu_part = jax.nn.sigmoid(gates[:, 0:D_MODEL]) * _dot(ysgu_s[...], pa_ref[...])
    gate_b = jax.nn.sigmoid(gates[:, D_MODEL:])

    er = lax.broadcasted_iota(jnp.int32, (ATTN_BLOCK, LANES), 0)
    ec = lax.broadcasted_iota(jnp.int32, (ATTN_BLOCK, LANES), 1)
    eye = jnp.where(er == ec, 1.0, 0.0).astype(BF16)
    kr = lax.broadcasted_iota(jnp.int32, (band_keys, ATTN_BLOCK), 0)
    qc_i = lax.broadcasted_iota(jnp.int32, (band_keys, ATTN_BLOCK), 1)
    band_t = (kr > qc_i) & (kr <= qc_i + ATTN_BLOCK)
    for idx, ok in ((0, band_t & (kr >= ATTN_BLOCK)), (1, band_t)):
        bias_t = jnp.where(ok, 0.0, MASK_VALUE).astype(BF16)
        bias_s[idx, 0:band_keys, :] = bias_t
        bias_s[idx, band_keys:2 * band_keys, :] = bias_t
    sr = lax.broadcasted_iota(jnp.int32, (2 * band_keys, LANES), 0)
    sc = lax.broadcasted_iota(jnp.int32, (2 * band_keys, LANES), 1)
    ones_sel = jnp.where((sr < band_keys) == (sc < HEAD_DIM), 1.0, 0.0).astype(BF16)
    sink_value_row = (sr == 0) | (sr == band_keys)
    sink_lane = ec == 0
    contract_last = (((1,), (1,)), ((), ()))

    def score_matmul(b, g):
        r0 = b * ATTN_BLOCK
        bias_t = bias_s[jnp.where(j > 0, 1, 0)] if b == 0 else bias_s[1]
        kbd = jnp.concatenate([k4_s[2 * g, pl.ds(r0, band_keys), :],
                               k4_s[2 * g + 1, pl.ds(r0, band_keys), :]], axis=0)
        kbd = jnp.concatenate([kbd, bias_t], axis=1)
        qrows = []
        for pr in range(Q_PER_KV // 2):
            col = (g * Q_PER_KV + 2 * pr) * HEAD_DIM
            qrows.append(jnp.concatenate([q_s[pl.ds(r0, ATTN_BLOCK), col:col + LANES], eye], axis=1))
        return lax.dot_general(jnp.concatenate(qrows, axis=0), kbd, contract_last,
                               preferred_element_type=F32)

    def softmax_value_matmul(b, g, scores):
        r0 = b * ATTN_BLOCK
        vbd = jnp.concatenate([v4_s[2 * g, pl.ds(r0, band_keys), :],
                               v4_s[2 * g + 1, pl.ds(r0, band_keys), :]], axis=0)
        vbd = jnp.where(sink_value_row, jnp.zeros_like(vbd), vbd)
        value_op = jnp.concatenate([vbd, ones_sel], axis=1)
        probs = []
        for pr in range(Q_PER_KV // 2):
            rows = slice(pr * ATTN_BLOCK, (pr + 1) * ATTN_BLOCK)
            pair_probs = []
            for hh in range(2):
                c0 = hh * band_keys
                sink = sink_ref[g * Q_PER_KV + 2 * pr + hh] * LOG2E
                first = jnp.where(sink_lane, sink, scores[rows, c0:c0 + LANES])
                rest = scores[rows, c0 + LANES:c0 + band_keys]
                m = jnp.max(jnp.maximum(first, rest), axis=-1, keepdims=True)
                pair_probs += [jnp.exp2(first - m).astype(BF16), jnp.exp2(rest - m).astype(BF16)]
            probs.append(jnp.concatenate(pair_probs, axis=1))
        o = _dot(jnp.concatenate(probs, axis=0), value_op)
        for pr in range(Q_PER_KV // 2):
            rows = slice(pr * ATTN_BLOCK, (pr + 1) * ATTN_BLOCK)
            col = (g * Q_PER_KV + 2 * pr) * HEAD_DIM
            yattn_s[pl.ds(r0, ATTN_BLOCK), col:col + LANES] = (
                o[rows, 0:LANES] / o[rows, LANES:]).astype(BF16)

    units = [(b, g) for b in range(nblk) for g in range(N_KV_HEADS)]
    pending = [score_matmul(*u) for u in units[:ATTN_LOOKAHEAD]]
    for i, (b, g) in enumerate(units):
        if i + ATTN_LOOKAHEAD < len(units):
            pending.append(score_matmul(*units[i + ATTN_LOOKAHEAD]))
        softmax_value_matmul(b, g, pending.pop(0))

    k4_s[:, 0:ATTN_BLOCK, :] = k4_s[:, tm:tm + ATTN_BLOCK, :]
    v4_s[:, 0:ATTN_BLOCK, :] = v4_s[:, tm:tm + ATTN_BLOCK, :]

    merged = (sgu_part + gate_b * _dot(yattn_s[...], pb_ref[...])).astype(BF16)
    for r in range(tm // OUT_ROW_CHUNK):
        rows = slice(r * OUT_ROW_CHUNK, (r + 1) * OUT_ROW_CHUNK)
        o_ref[0, rows, :] = x_ref[0, rows, :] + _mod_row(mod_ref, 2) * _dot(merged[rows, :], wo_ref[...])


def _full(shape):
    zeros = (0,) * len(shape)
    return pl.BlockSpec(shape, lambda i, j: zeros)


def _mix_layer(x, mod, n1w, cos_tab, sin_tab, w_in, sinks, ln_w, ln_b, sgu_w, sgu_b_full,
               proj_a, proj_b, w_out, mod_layer, cast_weights, layer):
    b, s, d = x.shape
    tm = TOKEN_TILE
    tile = pl.BlockSpec((1, tm, d), lambda i, j: (i, j, 0))
    tab = pl.BlockSpec((1, tm, LANES), lambda i, j: (i, j, 0))
    grid = (b, s // tm)
    cast_in, cast_out, cast_shapes = _cast_specs(cast_weights, layer, grid)
    return pl.pallas_call(
        functools.partial(_mix_kernel, n_cast=len(cast_weights)),
        grid=grid,
        in_specs=[
            tile,
            pl.BlockSpec((None,) + mod.shape[1:], lambda i, j: (mod_layer, 0, 0)),
            _full((1, d)),
            tab, tab,
            _full((d, IN_COLS)),
            pl.BlockSpec(memory_space=pltpu.SMEM),
            _full((1, SGU_WIDTH)), _full((1, SGU_WIDTH)),
            _full((SGU_GROUPS, SGU_CHUNK, SGU_CHUNK)),
            _full((SGU_CHUNK, SGU_WIDTH)),
            _full((SGU_WIDTH, d)), _full((Q_END, d)), _full((d, d)),
        ] + cast_in,
        out_specs=[tile] + cast_out,
        out_shape=[jax.ShapeDtypeStruct(x.shape, x.dtype)] + cast_shapes,
        scratch_shapes=[
            pltpu.VMEM((tm, Q_END), BF16),
            pltpu.VMEM((4, tm + ATTN_BLOCK, LANES), BF16),
            pltpu.VMEM((4, tm + ATTN_BLOCK, LANES), BF16),
            pltpu.VMEM((2, 4 * ATTN_BLOCK, LANES), BF16),
            pltpu.VMEM((tm, Q_END), BF16),
            pltpu.VMEM((tm, SGU_WIDTH), BF16),
        ],
        compiler_params=pltpu.CompilerParams(
            dimension_semantics=("arbitrary", "arbitrary"),
            vmem_limit_bytes=VMEM_LIMIT_BYTES),
        name="token_mixer",
    )(x, mod, n1w, cos_tab, sin_tab, w_in, sinks, ln_w, ln_b, sgu_w, sgu_b_full,
      proj_a, proj_b, w_out, *cast_weights)


CONV_PAD_ROWS = SUBLANES


FFN_N_INPUTS = 9


def _ffn_kernel(*refs, n_cast, final_norm):
    (x_ref, mod_ref, n2w_ref, wg_ref, wu_ref, cw_ref, cb_ref, wd_ref, fnw_ref) = refs[:FFN_N_INPUTS]
    o_ref = refs[FFN_N_INPUTS + n_cast]
    a_s, carry_s, act_s = refs[FFN_N_INPUTS + 2 * n_cast + 1:]
    _cast_slabs(refs[FFN_N_INPUTS:FFN_N_INPUTS + n_cast],
                refs[FFN_N_INPUTS + n_cast + 1:FFN_N_INPUTS + 2 * n_cast + 1])
    tm = FFN_TOKEN_TILE
    j = pl.program_id(1)

    @pl.when(j == 0)
    def _():
        carry_s[...] = jnp.zeros((CONV_PAD_ROWS, FFN_DIM), F32)

    norm_scale = n2w_ref[...] * (1.0 + _mod_row(mod_ref, 4))
    shift2 = _mod_row(mod_ref, 3)
    hb_chunks, a_chunks, up_chunks = [], [], []
    for r in range(tm // NORM_ROW_CHUNK):
        xr = x_ref[0, r * NORM_ROW_CHUNK:(r + 1) * NORM_ROW_CHUNK, :]
        hr = (_rms_norm(xr, norm_scale) + shift2).astype(BF16)
        hb_chunks.append(hr)
        a_chunks.append(_dot(hr, wg_ref[:, 0:MXU_COLS]))
        up_chunks.append(_dot(hr, wu_ref[:, 0:MXU_COLS]))
    hb = jnp.concatenate(hb_chunks, axis=0)
    first_a = jnp.concatenate(a_chunks, axis=0)
    first_up = jnp.concatenate(up_chunks, axis=0)

    for c in range(FFN_DIM // MXU_COLS):
        cols = slice(c * MXU_COLS, (c + 1) * MXU_COLS)
        a = first_a if c == 0 else _dot(hb, wg_ref[:, cols])
        stage = a_s.at[c % 2]
        stage[0:CONV_PAD_ROWS, :] = carry_s[:, cols]
        stage[CONV_PAD_ROWS:CONV_PAD_ROWS + tm, :] = a
        conv = cb_ref[:, cols] + cw_ref[2:3, cols] * a
        for tap in range(CONV_WIDTH - 1):
            back = CONV_WIDTH - 1 - tap
            conv = conv + cw_ref[tap:tap + 1, cols] * stage[CONV_PAD_ROWS - back:CONV_PAD_ROWS - back + tm, :]
        carry_s[:, cols] = stage[tm:tm + CONV_PAD_ROWS, :]
        up = first_up if c == 0 else _dot(hb, wu_ref[:, cols])
        act_s[:, cols] = (conv * jax.nn.sigmoid(conv) * up).astype(BF16)

    for r in range(tm // OUT_ROW_CHUNK):
        rows = slice(r * OUT_ROW_CHUNK, (r + 1) * OUT_ROW_CHUNK)
        y = x_ref[0, rows, :] + _mod_row(mod_ref, 5) * _dot(act_s[rows, :], wd_ref[...])
        if final_norm:
            y = _rms_norm(y, fnw_ref[...])
        o_ref[0, rows, :] = y


def _ffn_layer(x, mod, n2w, w_gate, w_up, conv_w, conv_b, w_down, final_w, final_norm,
               mod_layer, cast_weights, layer):
    b, s, d = x.shape
    tm = FFN_TOKEN_TILE
    tile = pl.BlockSpec((1, tm, d), lambda i, j: (i, j, 0))
    grid = (b, s // tm)
    cast_in, cast_out, cast_shapes = _cast_specs(cast_weights, layer, grid)
    return pl.pallas_call(
        functools.partial(_ffn_kernel, n_cast=len(cast_weights), final_norm=final_norm),
        grid=grid,
        in_specs=[
            tile,
            pl.BlockSpec((None,) + mod.shape[1:], lambda i, j: (mod_layer, 0, 0)),
            _full((1, d)),
            _full((d, FFN_DIM)), _full((d, FFN_DIM)),
            _full((CONV_WIDTH, FFN_DIM)), _full((1, FFN_DIM)),
            _full((FFN_DIM, d)),
            _full((1, d)),
        ] + cast_in,
        out_specs=[tile] + cast_out,
        out_shape=[jax.ShapeDtypeStruct(x.shape, x.dtype)] + cast_shapes,
        scratch_shapes=[
            pltpu.VMEM((2, CONV_PAD_ROWS + tm, MXU_COLS), F32),
            pltpu.VMEM((CONV_PAD_ROWS, FFN_DIM), F32),
            pltpu.VMEM((tm, FFN_DIM), BF16),
        ],
        compiler_params=pltpu.CompilerParams(
            dimension_semantics=("arbitrary", "arbitrary"),
            vmem_limit_bytes=VMEM_LIMIT_BYTES),
        name="conv_ffn",
    )(x, mod, n2w, w_gate, w_up, conv_w, conv_b, w_down, final_w, *cast_weights)


def kernel(x, c, positions, ada_w, ada_b, norm1_w, w_in, attn_sinks, sgu_ln_w, sgu_ln_b, sgu_w, sgu_b,
           proj_a, proj_b, w_out, norm2_w, ffn_w_gate, ffn_w_up, ffn_conv_w, ffn_conv_b, ffn_w_down,
           final_norm_w):
    batch, seq, d = x.shape
    depth = ada_w.shape[0]
    assert seq % TOKEN_TILE == 0 and seq % FFN_TOKEN_TILE == 0 and TOKEN_TILE % ATTN_BLOCK == 0
    assert d == D_MODEL

    mix_f32 = (w_in, proj_a, proj_b, w_out)
    ffn_f32 = (ffn_w_gate, ffn_w_up, ffn_w_down)
    rows = -(-batch // SUBLANES) * SUBLANES
    c_pad = jnp.pad(c, ((0, rows - batch), (0, 0)))
    mod, cos_tab, sin_tab, *mix_w = _prepare(c_pad, ada_w, ada_b, positions, mix_f32)

    for l in range(depth):
        sgu_b_full = jnp.repeat(sgu_b[l].T, SGU_WIDTH // SGU_GROUPS, axis=1)
        x, *ffn_w = _mix_layer(
            x, mod, norm1_w[l].reshape(1, d), cos_tab, sin_tab, mix_w[0],
            attn_sinks[l], sgu_ln_w[l].reshape(1, -1), sgu_ln_b[l].reshape(1, -1),
            sgu_w[l], sgu_b_full, mix_w[1], mix_w[2], mix_w[3], l, ffn_f32, l)
        last = l == depth - 1
        x, *mix_w = _ffn_layer(
            x, mod, norm2_w[l].reshape(1, d), ffn_w[0], ffn_w[1],
            ffn_conv_w[l], ffn_conv_b[l].reshape(1, -1), ffn_w[2],
            final_norm_w.reshape(1, d), last, l, () if last else mix_f32, l + 1)
    return x
```

```python
import functools

import jax
import jax.numpy as jnp
from jax import lax
from jax.experimental import pallas as pl
from jax.experimental.pallas import tpu as pltpu

D_MODEL = 1024
N_Q_HEADS = 16
N_KV_HEADS = 2
HEAD_DIM = 64
Q_PER_KV = N_Q_HEADS // N_KV_HEADS
ATTN_BLOCK = 128
ROPE_THETA = 500000.0
ROT_DIM = HEAD_DIM // 4
SGU_WIDTH = 1024
SGU_GROUPS = 8
SGU_CHUNK = 128
FFN_DIM = 2816
CONV_WIDTH = 3
NORM_EPS = 1e-6

Q_END = N_Q_HEADS * HEAD_DIM
K_END = Q_END + N_KV_HEADS * HEAD_DIM
V_END = K_END + N_KV_HEADS * HEAD_DIM
Z_END = V_END + 2 * SGU_WIDTH
IN_COLS = Z_END + 2 * D_MODEL

LANES = 128
SUBLANES = 8
MXU_COLS = 256
TOKEN_TILE = 512
FFN_TOKEN_TILE = 1024
NORM_ROW_CHUNK = 128
OUT_ROW_CHUNK = 256
ATTN_LOOKAHEAD = 2
VMEM_LIMIT_BYTES = 56 * 1024 * 1024
MASK_VALUE = -1e30
LOG2E = 1.4426950408889634

F32 = jnp.float32
BF16 = jnp.bfloat16


def _rms_norm(x, w):
    ms = jnp.mean(x * x, axis=-1, keepdims=True)
    return x * lax.rsqrt(ms + NORM_EPS) * w


def _mod_row(mod_ref, part):
    return mod_ref[pl.ds(pl.program_id(0), 1), part * D_MODEL:(part + 1) * D_MODEL]


def _dot(a, b):
    return jnp.dot(a, b, preferred_element_type=F32)


BF16_ROW_TILE = 16


def _cast_specs(stacked_weights, layer, grid):
    steps = grid[0] * grid[1]
    specs, shapes = [], []
    for w in stacked_weights:
        _, rows, cols = w.shape
        slab, share = rows // steps, 1
        while slab % BF16_ROW_TILE:
            slab, share = slab * 2, share * 2
        assert rows % slab == 0 and steps % share == 0
        idx = lambda i, j, share=share: ((i * grid[1] + j) // share, 0)
        specs.append((pl.BlockSpec((None, slab, cols), lambda i, j, idx=idx: (layer,) + idx(i, j)),
                      pl.BlockSpec((slab, cols), idx)))
        shapes.append(jax.ShapeDtypeStruct((rows, cols), BF16))
    return [s[0] for s in specs], [s[1] for s in specs], shapes


def _cast_slabs(src_refs, dst_refs):
    for src, dst in zip(src_refs, dst_refs):
        dst[...] = src[...].astype(BF16)


ROPE_ROW_TILE = 1024


def _split_bf16(v):
    hi = v.astype(BF16)
    return hi, (v - hi.astype(F32)).astype(BF16)


def _prep_kernel(c_ref, adaw_ref, adab_ref, pos_ref, invf_ref, *refs):
    n_cast = (len(refs) - 3) // 2
    mod_ref, cos_ref, sin_ref = refs[n_cast:n_cast + 3]
    _cast_slabs(refs[:n_cast], refs[n_cast + 3:])

    c = c_ref[...]
    rows = c.shape[0]
    c_hi, c_lo = _split_bf16(c * jax.nn.sigmoid(c))
    lhs = jnp.concatenate([c_hi, c_lo], axis=0)
    w_hi, w_lo = _split_bf16(adaw_ref[...])
    parts = _dot(lhs, w_hi) + _dot(lhs, w_lo)
    mod_ref[...] = parts[0:rows] + parts[rows:2 * rows] + adab_ref[...]

    nfreq = ROT_DIM // 2
    freq_row = lax.broadcasted_iota(jnp.int32, (nfreq, LANES), 0)
    inv_freq = jnp.zeros((nfreq, LANES), F32)
    for f in range(nfreq):
        inv_freq = jnp.where(freq_row == f, invf_ref[f], inv_freq)
    part_freq = lax.broadcasted_iota(jnp.int32, (2 * nfreq, LANES), 0) & (nfreq - 1)
    part_lane = lax.broadcasted_iota(jnp.int32, (2 * nfreq, LANES), 1) & (HEAD_DIM - 1)
    uses = (part_lane < ROT_DIM) & ((part_lane & (nfreq - 1)) == part_freq)
    cos_pattern = jnp.where(uses, 1.0, 0.0).astype(BF16)
    sin_pattern = jnp.where(uses, jnp.where(part_lane < nfreq, -1.0, 1.0), 0.0).astype(BF16)
    head_lane = lax.broadcasted_iota(jnp.int32, (LANES, LANES), 1) & (HEAD_DIM - 1)
    contract_first = (((0,), (0,)), ((), ()))

    def expand(values, pattern):
        hi = values.astype(BF16).astype(F32)
        parts = jnp.concatenate([hi, values - hi], axis=0).astype(BF16)
        return lax.dot_general(parts, pattern, contract_first, preferred_element_type=F32)

    for r in range(pos_ref.shape[1]):
        ang = inv_freq * pos_ref[0, r:r + 1, :].astype(F32)
        rows_r = slice(r * LANES, (r + 1) * LANES)
        cos_ref[0, rows_r, :] = jnp.where(head_lane < ROT_DIM, expand(jnp.cos(ang), cos_pattern), 1.0)
        sin_ref[0, rows_r, :] = expand(jnp.sin(ang), sin_pattern)


def _prepare(c, ada_w, ada_b, positions, cast_weights):
    b, s = positions.shape
    depth, d, n = ada_w.shape
    rows = c.shape[0]
    inv_freq = ROPE_THETA ** (-jnp.arange(0, ROT_DIM, 2, dtype=F32) / ROT_DIM)
    tab = jax.ShapeDtypeStruct((b, s, LANES), F32)
    grid = (b, s // ROPE_ROW_TILE)
    steps = grid[0] * grid[1]
    assert steps % depth == 0 and n % (steps // depth) == 0
    col_tiles = steps // depth
    ada_cols = n // col_tiles
    ada_idx = lambda i, j: ((i * grid[1] + j) // col_tiles, 0, (i * grid[1] + j) % col_tiles)
    cast_in, cast_out, cast_shapes = _cast_specs(cast_weights, 0, grid)
    return pl.pallas_call(
        _prep_kernel,
        grid=grid,
        in_specs=[
            pl.BlockSpec((rows, d), lambda i, j: (0, 0)),
            pl.BlockSpec((None, d, ada_cols), ada_idx),
            pl.BlockSpec((None, 1, ada_cols), ada_idx),
            pl.BlockSpec((1, ROPE_ROW_TILE // LANES, LANES), lambda i, j: (i, j, 0)),
            pl.BlockSpec(memory_space=pltpu.SMEM),
        ] + cast_in,
        out_specs=[pl.BlockSpec((None, rows, ada_cols), ada_idx)]
        + [pl.BlockSpec((1, ROPE_ROW_TILE, LANES), lambda i, j: (i, j, 0))] * 2 + cast_out,
        out_shape=[jax.ShapeDtypeStruct((depth, rows, n), F32), tab, tab] + cast_shapes,
        compiler_params=pltpu.CompilerParams(
            dimension_semantics=("arbitrary", "arbitrary"),
            vmem_limit_bytes=VMEM_LIMIT_BYTES),
        name="prepare",
    )(c, ada_w, ada_b.reshape(depth, 1, n), positions.reshape(b, s // LANES, LANES), inv_freq,
      *cast_weights)


MIX_N_INPUTS = 14


def _mix_kernel(*refs, n_cast):
    (x_ref, mod_ref, n1w_ref, cos_ref, sin_ref, win_ref, sink_ref, lnw_ref, lnb_ref,
     sguw_ref, sgub_ref, pa_ref, pb_ref, wo_ref) = refs[:MIX_N_INPUTS]
    o_ref = refs[MIX_N_INPUTS + n_cast]
    q_s, k4_s, v4_s, bias_s, yattn_s, ysgu_s = refs[MIX_N_INPUTS + 2 * n_cast + 1:]
    _cast_slabs(refs[MIX_N_INPUTS:MIX_N_INPUTS + n_cast],
                refs[MIX_N_INPUTS + n_cast + 1:MIX_N_INPUTS + 2 * n_cast + 1])
    tm = TOKEN_TILE
    nblk = tm // ATTN_BLOCK
    band_keys = 2 * ATTN_BLOCK
    j = pl.program_id(1)

    @pl.when(j == 0)
    def _():
        k4_s[:, 0:ATTN_BLOCK, :] = jnp.zeros((4, ATTN_BLOCK, LANES), BF16)
        v4_s[:, 0:ATTN_BLOCK, :] = jnp.zeros((4, ATTN_BLOCK, LANES), BF16)

    norm_scale = n1w_ref[...] * (1.0 + _mod_row(mod_ref, 1))
    shift1 = _mod_row(mod_ref, 0)
    hb_chunks, q_chunks = [], []
    for r in range(tm // NORM_ROW_CHUNK):
        xr = x_ref[0, r * NORM_ROW_CHUNK:(r + 1) * NORM_ROW_CHUNK, :]
        hr = (_rms_norm(xr, norm_scale) + shift1).astype(BF16)
        hb_chunks.append(hr)
        q_chunks.append(_dot(hr, win_ref[:, 0:Q_END]))
    hb = jnp.concatenate(hb_chunks, axis=0)
    q = jnp.concatenate(q_chunks, axis=0)
    kv = _dot(hb, win_ref[:, Q_END:V_END])
    z = _dot(hb, win_ref[:, V_END:Z_END])

    cos_t = cos_ref[0]
    sin_t = sin_ref[0]
    lane = lax.broadcasted_iota(jnp.int32, (tm, LANES), 1)
    first_half = (lane & (HEAD_DIM - 1)) < (ROT_DIM // 2)
    left_head = lane < HEAD_DIM

    def rope(t):
        partner = jnp.where(first_half,
                            pltpu.roll(t, LANES - ROT_DIM // 2, 1),
                            pltpu.roll(t, ROT_DIM // 2, 1))
        return t * cos_t + partner * sin_t

    for c in range(Q_END // LANES):
        qc = rope(q[:, c * LANES:(c + 1) * LANES]) * (HEAD_DIM ** -0.5 * LOG2E)
        q_s[:, c * LANES:(c + 1) * LANES] = qc.astype(BF16)

    k = rope(kv[:, 0:LANES])
    v = kv[:, LANES:2 * LANES]
    for src, dst in ((k, k4_s), (v, v4_s)):
        swapped = pltpu.roll(src, HEAD_DIM, 1)
        zero = jnp.zeros_like(src)
        rows = slice(ATTN_BLOCK, ATTN_BLOCK + tm)
        dst[0, rows, :] = jnp.where(left_head, src, zero).astype(BF16)
        dst[1, rows, :] = jnp.where(left_head, zero, swapped).astype(BF16)
        dst[2, rows, :] = jnp.where(left_head, swapped, zero).astype(BF16)
        dst[3, rows, :] = jnp.where(left_head, zero, src).astype(BF16)

    gates = _dot(hb, win_ref[:, Z_END:IN_COLS])

    z = 0.5 * z * (1.0 + lax.erf(z * (2.0 ** -0.5)))
    u = z[:, 0:SGU_WIDTH]
    v_sgu = z[:, SGU_WIDTH:]
    mu = jnp.mean(v_sgu, axis=-1, keepdims=True)
    dv = v_sgu - mu
    var = jnp.mean(dv * dv, axis=-1, keepdims=True)
    vn = (dv * lax.rsqrt(var + NORM_EPS) * lnw_ref[...] + lnb_ref[...]).astype(BF16)
    ti = lax.broadcasted_iota(jnp.int32, (SGU_CHUNK, SGU_CHUNK), 0)
    si = lax.broadcasted_iota(jnp.int32, (SGU_CHUNK, SGU_CHUNK), 1)
    causal = si <= ti
    gdim = SGU_WIDTH // SGU_GROUPS
    nchunk = tm // SGU_CHUNK
    for g in range(SGU_GROUPS):
        cols = slice(g * gdim, (g + 1) * gdim)
        wm = jnp.where(causal, sguw_ref[g], 0.0).astype(BF16)
        rhs = jnp.concatenate(
            [vn[c * SGU_CHUNK:(c + 1) * SGU_CHUNK, cols] for c in range(nchunk)], axis=1)
        f = _dot(wm, rhs)
        for c in range(nchunk):
            rows = slice(c * SGU_CHUNK, (c + 1) * SGU_CHUNK)
            fc = f[:, c * gdim:(c + 1) * gdim] + sgub_ref[:, cols]
            ysgu_s[rows, cols] = (u[rows, cols] * fc).astype(BF16)

    sgu_part = jax.nn.sigmoid(gates[:, 0:D_MODEL]) * _dot(ysgu_s[...], pa_ref[...])
    gate_b = jax.nn.sigmoid(gates[:, D_MODEL:])

    er = lax.broadcasted_iota(jnp.int32, (ATTN_BLOCK, LANES), 0)
    ec = lax.broadcasted_iota(jnp.int32, (ATTN_BLOCK, LANES), 1)
    eye = jnp.where(er == ec, 1.0, 0.0).astype(BF16)
    kr = lax.broadcasted_iota(jnp.int32, (band_keys, ATTN_BLOCK), 0)
    qc_i = lax.broadcasted_iota(jnp.int32, (band_keys, ATTN_BLOCK), 1)
    band_t = (kr > qc_i) & (kr <= qc_i + ATTN_BLOCK)
    for idx, ok in ((0, band_t & (kr >= ATTN_BLOCK)), (1, band_t)):
        bias_t = jnp.where(ok, 0.0, MASK_VALUE).astype(BF16)
        bias_s[idx, 0:band_keys, :] = bias_t
        bias_s[idx, band_keys:2 * band_keys, :] = bias_t
    sr = lax.broadcasted_iota(jnp.int32, (2 * band_keys, LANES), 0)
    sc = lax.broadcasted_iota(jnp.int32, (2 * band_keys, LANES), 1)
    ones_sel = jnp.where((sr < band_keys) == (sc < HEAD_DIM), 1.0, 0.0).astype(BF16)
    sink_value_row = (sr == 0) | (sr == band_keys)
    sink_lane = ec == 0
    contract_last = (((1,), (1,)), ((), ()))

    def score_matmul(b, g):
        r0 = b * ATTN_BLOCK
        bias_t = bias_s[jnp.where(j > 0, 1, 0)] if b == 0 else bias_s[1]
        kbd = jnp.concatenate([k4_s[2 * g, pl.ds(r0, band_keys), :],
                               k4_s[2 * g + 1, pl.ds(r0, band_keys), :]], axis=0)
        kbd = jnp.concatenate([kbd, bias_t], axis=1)
        qrows = []
        for pr in range(Q_PER_KV // 2):
            col = (g * Q_PER_KV + 2 * pr) * HEAD_DIM
            qrows.append(jnp.concatenate([q_s[pl.ds(r0, ATTN_BLOCK), col:col + LANES], eye], axis=1))
        return lax.dot_general(jnp.concatenate(qrows, axis=0), kbd, contract_last,
                               preferred_element_type=F32)

    def softmax_value_matmul(b, g, scores):
        r0 = b * ATTN_BLOCK
        vbd = jnp.concatenate([v4_s[2 * g, pl.ds(r0, band_keys), :],
                               v4_s[2 * g + 1, pl.ds(r0, band_keys), :]], axis=0)
        vbd = jnp.where(sink_value_row, jnp.zeros_like(vbd), vbd)
        value_op = jnp.concatenate([vbd, ones_sel], axis=1)
        probs = []
        for pr in range(Q_PER_KV // 2):
            rows = slice(pr * ATTN_BLOCK, (pr + 1) * ATTN_BLOCK)
            pair_probs = []
            for hh in range(2):
                c0 = hh * band_keys
                sink = sink_ref[g * Q_PER_KV + 2 * pr + hh] * LOG2E
                first = jnp.where(sink_lane, sink, scores[rows, c0:c0 + LANES])
                rest = scores[rows, c0 + LANES:c0 + band_keys]
                m = jnp.max(jnp.maximum(first, rest), axis=-1, keepdims=True)
                pair_probs += [jnp.exp2(first - m).astype(BF16), jnp.exp2(rest - m).astype(BF16)]
            probs.append(jnp.concatenate(pair_probs, axis=1))
        o = _dot(jnp.concatenate(probs, axis=0), value_op)
        for pr in range(Q_PER_KV // 2):
            rows = slice(pr * ATTN_BLOCK, (pr + 1) * ATTN_BLOCK)
            col = (g * Q_PER_KV + 2 * pr) * HEAD_DIM
            yattn_s[pl.ds(r0, ATTN_BLOCK), col:col + LANES] = (
                o[rows, 0:LANES] / o[rows, LANES:]).astype(BF16)

    units = [(b, g) for b in range(nblk) for g in range(N_KV_HEADS)]
    pending = [score_matmul(*u) for u in units[:ATTN_LOOKAHEAD]]
    for i, (b, g) in enumerate(units):
        if i + ATTN_LOOKAHEAD < len(units):
            pending.append(score_matmul(*units[i + ATTN_LOOKAHEAD]))
        softmax_value_matmul(b, g, pending.pop(0))

    k4_s[:, 0:ATTN_BLOCK, :] = k4_s[:, tm:tm + ATTN_BLOCK, :]
    v4_s[:, 0:ATTN_BLOCK, :] = v4_s[:, tm:tm + ATTN_BLOCK, :]

    merged = (sgu_part + gate_b * _dot(yattn_s[...], pb_ref[...])).astype(BF16)
    for r in range(tm // OUT_ROW_CHUNK):
        rows = slice(r * OUT_ROW_CHUNK, (r + 1) * OUT_ROW_CHUNK)
        o_ref[0, rows, :] = x_ref[0, rows, :] + _mod_row(mod_ref, 2) * _dot(merged[rows, :], wo_ref[...])


def _full(shape):
    zeros = (0,) * len(shape)
    return pl.BlockSpec(shape, lambda i, j: zeros)


def _mix_layer(x, mod, n1w, cos_tab, sin_tab, w_in, sinks, ln_w, ln_b, sgu_w, sgu_b_full,
               proj_a, proj_b, w_out, mod_layer, cast_weights, layer):
    b, s, d = x.shape
    tm = TOKEN_TILE
    tile = pl.BlockSpec((1, tm, d), lambda i, j: (i, j, 0))
    tab = pl.BlockSpec((1, tm, LANES), lambda i, j: (i, j, 0))
    grid = (b, s // tm)
    cast_in, cast_out, cast_shapes = _cast_specs(cast_weights, layer, grid)
    return pl.pallas_call(
        functools.partial(_mix_kernel, n_cast=len(cast_weights)),
        grid=grid,
        in_specs=[
            tile,
            pl.BlockSpec((None,) + mod.shape[1:], lambda i, j: (mod_layer, 0, 0)),
            _full((1, d)),
            tab, tab,
            _full((d, IN_COLS)),
            pl.BlockSpec(memory_space=pltpu.SMEM),
            _full((1, SGU_WIDTH)), _full((1, SGU_WIDTH)),
            _full((SGU_GROUPS, SGU_CHUNK, SGU_CHUNK)),
            _full((SGU_CHUNK, SGU_WIDTH)),
            _full((SGU_WIDTH, d)), _full((Q_END, d)), _full((d, d)),
        ] + cast_in,
        out_specs=[tile] + cast_out,
        out_shape=[jax.ShapeDtypeStruct(x.shape, x.dtype)] + cast_shapes,
        scratch_shapes=[
            pltpu.VMEM((tm, Q_END), BF16),
            pltpu.VMEM((4, tm + ATTN_BLOCK, LANES), BF16),
            pltpu.VMEM((4, tm + ATTN_BLOCK, LANES), BF16),
            pltpu.VMEM((2, 4 * ATTN_BLOCK, LANES), BF16),
            pltpu.VMEM((tm, Q_END), BF16),
            pltpu.VMEM((tm, SGU_WIDTH), BF16),
        ],
        compiler_params=pltpu.CompilerParams(
            dimension_semantics=("arbitrary", "arbitrary"),
            vmem_limit_bytes=VMEM_LIMIT_BYTES),
        name="token_mixer",
    )(x, mod, n1w, cos_tab, sin_tab, w_in, sinks, ln_w, ln_b, sgu_w, sgu_b_full,
      proj_a, proj_b, w_out, *cast_weights)


CONV_PAD_ROWS = SUBLANES


FFN_N_INPUTS = 9


def _ffn_kernel(*refs, n_cast, final_norm):
    (x_ref, mod_ref, n2w_ref, wg_ref, wu_ref, cw_ref, cb_ref, wd_ref, fnw_ref) = refs[:FFN_N_INPUTS]
    o_ref = refs[FFN_N_INPUTS + n_cast]
    a_s, carry_s, act_s = refs[FFN_N_INPUTS + 2 * n_cast + 1:]
    _cast_slabs(refs[FFN_N_INPUTS:FFN_N_INPUTS + n_cast],
                refs[FFN_N_INPUTS + n_cast + 1:FFN_N_INPUTS + 2 * n_cast + 1])
    tm = FFN_TOKEN_TILE
    j = pl.program_id(1)

    @pl.when(j == 0)
    def _():
        carry_s[...] = jnp.zeros((CONV_PAD_ROWS, FFN_DIM), F32)

    norm_scale = n2w_ref[...] * (1.0 + _mod_row(mod_ref, 4))
    shift2 = _mod_row(mod_ref, 3)
    hb_chunks, a_chunks, up_chunks = [], [], []
    for r in range(tm // NORM_ROW_CHUNK):
        xr = x_ref[0, r * NORM_ROW_CHUNK:(r + 1) * NORM_ROW_CHUNK, :]
        hr = (_rms_norm(xr, norm_scale) + shift2).astype(BF16)
        hb_chunks.append(hr)
        a_chunks.append(_dot(hr, wg_ref[:, 0:MXU_COLS]))
        up_chunks.append(_dot(hr, wu_ref[:, 0:MXU_COLS]))
    hb = jnp.concatenate(hb_chunks, axis=0)
    first_a = jnp.concatenate(a_chunks, axis=0)
    first_up = jnp.concatenate(up_chunks, axis=0)

    for c in range(FFN_DIM // MXU_COLS):
        cols = slice(c * MXU_COLS, (c + 1) * MXU_COLS)
        a = first_a if c == 0 else _dot(hb, wg_ref[:, cols])
        stage = a_s.at[c % 2]
        stage[0:CONV_PAD_ROWS, :] = carry_s[:, cols]
        stage[CONV_PAD_ROWS:CONV_PAD_ROWS + tm, :] = a
        conv = cb_ref[:, cols] + cw_ref[2:3, cols] * a
        for tap in range(CONV_WIDTH - 1):
            back = CONV_WIDTH - 1 - tap
            conv = conv + cw_ref[tap:tap + 1, cols] * stage[CONV_PAD_ROWS - back:CONV_PAD_ROWS - back + tm, :]
        carry_s[:, cols] = stage[tm:tm + CONV_PAD_ROWS, :]
        up = first_up if c == 0 else _dot(hb, wu_ref[:, cols])
        act_s[:, cols] = (conv * jax.nn.sigmoid(conv) * up).astype(BF16)

    for r in range(tm // OUT_ROW_CHUNK):
        rows = slice(r * OUT_ROW_CHUNK, (r + 1) * OUT_ROW_CHUNK)
        y = x_ref[0, rows, :] + _mod_row(mod_ref, 5) * _dot(act_s[rows, :], wd_ref[...])
        if final_norm:
            y = _rms_norm(y, fnw_ref[...])
        o_ref[0, rows, :] = y


def _ffn_layer(x, mod, n2w, w_gate, w_up, conv_w, conv_b, w_down, final_w, final_norm,
               mod_layer, cast_weights, layer):
    b, s, d = x.shape
    tm = FFN_TOKEN_TILE
    tile = pl.BlockSpec((1, tm, d), lambda i, j: (i, j, 0))
    grid = (b, s // tm)
    cast_in, cast_out, cast_shapes = _cast_specs(cast_weights, layer, grid)
    return pl.pallas_call(
        functools.partial(_ffn_kernel, n_cast=len(cast_weights), final_norm=final_norm),
        grid=grid,
        in_specs=[
            tile,
            pl.BlockSpec((None,) + mod.shape[1:], lambda i, j: (mod_layer, 0, 0)),
            _full((1, d)),
            _full((d, FFN_DIM)), _full((d, FFN_DIM)),
            _full((CONV_WIDTH, FFN_DIM)), _full((1, FFN_DIM)),
            _full((FFN_DIM, d)),
            _full((1, d)),
        ] + cast_in,
        out_specs=[tile] + cast_out,
        out_shape=[jax.ShapeDtypeStruct(x.shape, x.dtype)] + cast_shapes,
        scratch_shapes=[
            pltpu.VMEM((2, CONV_PAD_ROWS + tm, MXU_COLS), F32),
            pltpu.VMEM((CONV_PAD_ROWS, FFN_DIM), F32),
            pltpu.VMEM((tm, FFN_DIM), BF16),
        ],
        compiler_params=pltpu.CompilerParams(
            dimension_semantics=("arbitrary", "arbitrary"),
            vmem_limit_bytes=VMEM_LIMIT_BYTES),
        name="conv_ffn",
    )(x, mod, n2w, w_gate, w_up, conv_w, conv_b, w_down, final_w, *cast_weights)


def kernel(x, c, positions, ada_w, ada_b, norm1_w, w_in, attn_sinks, sgu_ln_w, sgu_ln_b, sgu_w, sgu_b,
           proj_a, proj_b, w_out, norm2_w, ffn_w_gate, ffn_w_up, ffn_conv_w, ffn_conv_b, ffn_w_down,
           final_norm_w):
    batch, seq, d = x.shape
    depth = ada_w.shape[0]
    assert seq % TOKEN_TILE == 0 and seq % FFN_TOKEN_TILE == 0 and TOKEN_TILE % ATTN_BLOCK == 0
    assert d == D_MODEL

    mix_f32 = (w_in, proj_a, proj_b, w_out)
    ffn_f32 = (ffn_w_gate, ffn_w_up, ffn_w_down)
    rows = -(-batch // SUBLANES) * SUBLANES
    c_pad = jnp.pad(c, ((0, rows - batch), (0, 0)))
    mod, cos_tab, sin_tab, *mix_w = _prepare(c_pad, ada_w, ada_b, positions, mix_f32)

    for l in range(depth):
        sgu_b_full = jnp.repeat(sgu_b[l].T, SGU_WIDTH // SGU_GROUPS, axis=1)
        x, *ffn_w = _mix_layer(
            x, mod, norm1_w[l].reshape(1, d), cos_tab, sin_tab, mix_w[0],
            attn_sinks[l], sgu_ln_w[l].reshape(1, -1), sgu_ln_b[l].reshape(1, -1),
            sgu_w[l], sgu_b_full, mix_w[1], mix_w[2], mix_w[3], l, ffn_f32, l)
        last = l == depth - 1
        x, *mix_w = _ffn_layer(
            x, mod, norm2_w[l].reshape(1, d), ffn_w[0], ffn_w[1],
            ffn_conv_w[l], ffn_conv_b[l].reshape(1, -1), ffn_w[2],
            final_norm_w.reshape(1, d), last, l, () if last else mix_f32, l + 1)
    return x
```

```python
import functools

import jax
import jax.numpy as jnp
from jax import lax
from jax.experimental import pallas as pl
from jax.experimental.pallas import tpu as pltpu

D_MODEL = 1024
N_Q_HEADS = 16
N_KV_HEADS = 2
HEAD_DIM = 64
Q_PER_KV = N_Q_HEADS // N_KV_HEADS
ATTN_BLOCK = 128
ROPE_THETA = 500000.0
ROT_DIM = HEAD_DIM // 4
SGU_WIDTH = 1024
SGU_GROUPS = 8
SGU_CHUNK = 128
FFN_DIM = 2816
CONV_WIDTH = 3
NORM_EPS = 1e-6

Q_END = N_Q_HEADS * HEAD_DIM
K_END = Q_END + N_KV_HEADS * HEAD_DIM
V_END = K_END + N_KV_HEADS * HEAD_DIM
Z_END = V_END + 2 * SGU_WIDTH
IN_COLS = Z_END + 2 * D_MODEL

LANES = 128
SUBLANES = 8
MXU_COLS = 256
TOKEN_TILE = 512
FFN_TOKEN_TILE = 1024
NORM_ROW_CHUNK = 128
FFN_NORM_ROW_CHUNK = 256
OUT_ROW_CHUNK = 256
ATTN_LOOKAHEAD = 2
VMEM_LIMIT_BYTES = 56 * 1024 * 1024
MASK_VALUE = -1e30
LOG2E = 1.4426950408889634

F32 = jnp.float32
BF16 = jnp.bfloat16


def _rms_norm(x, w):
    ms = jnp.mean(x * x, axis=-1, keepdims=True)
    return x * lax.rsqrt(ms + NORM_EPS) * w


def _mod_row(mod_ref, part):
    return mod_ref[pl.ds(pl.program_id(0), 1), part * D_MODEL:(part + 1) * D_MODEL]


def _dot(a, b):
    return jnp.dot(a, b, preferred_element_type=F32)


BF16_ROW_TILE = 16


def _cast_specs(stacked_weights, layer, grid):
    steps = grid[0] * grid[1]
    specs, shapes = [], []
    for w in stacked_weights:
        _, rows, cols = w.shape
        slab, share = rows // steps, 1
        while slab % BF16_ROW_TILE:
            slab, share = slab * 2, share * 2
        assert rows % slab == 0 and steps % share == 0
        idx = lambda i, j, share=share: ((i * grid[1] + j) // share, 0)
        specs.append((pl.BlockSpec((None, slab, cols), lambda i, j, idx=idx: (layer,) + idx(i, j)),
                      pl.BlockSpec((slab, cols), idx)))
        shapes.append(jax.ShapeDtypeStruct((rows, cols), BF16))
    return [s[0] for s in specs], [s[1] for s in specs], shapes


def _cast_slabs(src_refs, dst_refs):
    for src, dst in zip(src_refs, dst_refs):
        dst[...] = src[...].astype(BF16)


ROPE_ROW_TILE = 1024


def _split_bf16(v):
    hi = v.astype(BF16)
    return hi, (v - hi.astype(F32)).astype(BF16)


def _prep_kernel(c_ref, adaw_ref, adab_ref, pos_ref, invf_ref, *refs):
    n_cast = (len(refs) - 3) // 2
    mod_ref, cos_ref, sin_ref = refs[n_cast:n_cast + 3]
    _cast_slabs(refs[:n_cast], refs[n_cast + 3:])

    c = c_ref[...]
    rows = c.shape[0]
    c_hi, c_lo = _split_bf16(c * jax.nn.sigmoid(c))
    lhs = jnp.concatenate([c_hi, c_lo], axis=0)
    w_hi, w_lo = _split_bf16(adaw_ref[...])
    parts = _dot(lhs, w_hi) + _dot(lhs, w_lo)
    mod_ref[...] = parts[0:rows] + parts[rows:2 * rows] + adab_ref[...]

    nfreq = ROT_DIM // 2
    freq_row = lax.broadcasted_iota(jnp.int32, (nfreq, LANES), 0)
    inv_freq = jnp.zeros((nfreq, LANES), F32)
    for f in range(nfreq):
        inv_freq = jnp.where(freq_row == f, invf_ref[f], inv_freq)
    part_freq = lax.broadcasted_iota(jnp.int32, (2 * nfreq, LANES), 0) & (nfreq - 1)
    part_lane = lax.broadcasted_iota(jnp.int32, (2 * nfreq, LANES), 1) & (HEAD_DIM - 1)
    uses = (part_lane < ROT_DIM) & ((part_lane & (nfreq - 1)) == part_freq)
    cos_pattern = jnp.where(uses, 1.0, 0.0).astype(BF16)
    sin_pattern = jnp.where(uses, jnp.where(part_lane < nfreq, -1.0, 1.0), 0.0).astype(BF16)
    head_lane = lax.broadcasted_iota(jnp.int32, (LANES, LANES), 1) & (HEAD_DIM - 1)
    contract_first = (((0,), (0,)), ((), ()))

    def expand(values, pattern):
        hi = values.astype(BF16).astype(F32)
        parts = jnp.concatenate([hi, values - hi], axis=0).astype(BF16)
        return lax.dot_general(parts, pattern, contract_first, preferred_element_type=F32)

    for r in range(pos_ref.shape[1]):
        ang = inv_freq * pos_ref[0, r:r + 1, :].astype(F32)
        rows_r = slice(r * LANES, (r + 1) * LANES)
        cos_ref[0, rows_r, :] = jnp.where(head_lane < ROT_DIM, expand(jnp.cos(ang), cos_pattern), 1.0)
        sin_ref[0, rows_r, :] = expand(jnp.sin(ang), sin_pattern)


def _prepare(c, ada_w, ada_b, positions, cast_weights):
    b, s = positions.shape
    depth, d, n = ada_w.shape
    rows = c.shape[0]
    inv_freq = ROPE_THETA ** (-jnp.arange(0, ROT_DIM, 2, dtype=F32) / ROT_DIM)
    tab = jax.ShapeDtypeStruct((b, s, LANES), F32)
    grid = (b, s // ROPE_ROW_TILE)
    steps = grid[0] * grid[1]
    assert steps % depth == 0 and n % (steps // depth) == 0
    col_tiles = steps // depth
    ada_cols = n // col_tiles
    ada_idx = lambda i, j: ((i * grid[1] + j) // col_tiles, 0, (i * grid[1] + j) % col_tiles)
    cast_in, cast_out, cast_shapes = _cast_specs(cast_weights, 0, grid)
    return pl.pallas_call(
        _prep_kernel,
        grid=grid,
        in_specs=[
            pl.BlockSpec((rows, d), lambda i, j: (0, 0)),
            pl.BlockSpec((None, d, ada_cols), ada_idx),
            pl.BlockSpec((None, 1, ada_cols), ada_idx),
            pl.BlockSpec((1, ROPE_ROW_TILE // LANES, LANES), lambda i, j: (i, j, 0)),
            pl.BlockSpec(memory_space=pltpu.SMEM),
        ] + cast_in,
        out_specs=[pl.BlockSpec((None, rows, ada_cols), ada_idx)]
        + [pl.BlockSpec((1, ROPE_ROW_TILE, LANES), lambda i, j: (i, j, 0))] * 2 + cast_out,
        out_shape=[jax.ShapeDtypeStruct((depth, rows, n), F32), tab, tab] + cast_shapes,
        compiler_params=pltpu.CompilerParams(
            dimension_semantics=("arbitrary", "arbitrary"),
            vmem_limit_bytes=VMEM_LIMIT_BYTES),
        name="prepare",
    )(c, ada_w, ada_b.reshape(depth, 1, n), positions.reshape(b, s // LANES, LANES), inv_freq,
      *cast_weights)


MIX_N_INPUTS = 14


def _mix_kernel(*refs, n_cast):
    (x_ref, mod_ref, n1w_ref, cos_ref, sin_ref, win_ref, sink_ref, lnw_ref, lnb_ref,
     sguw_ref, sgub_ref, pa_ref, pb_ref, wo_ref) = refs[:MIX_N_INPUTS]
    o_ref = refs[MIX_N_INPUTS + n_cast]
    q_s, k4_s, v4_s, bias_s, yattn_s, ysgu_s = refs[MIX_N_INPUTS + 2 * n_cast + 1:]
    _cast_slabs(refs[MIX_N_INPUTS:MIX_N_INPUTS + n_cast],
                refs[MIX_N_INPUTS + n_cast + 1:MIX_N_INPUTS + 2 * n_cast + 1])
    tm = TOKEN_TILE
    nblk = tm // ATTN_BLOCK
    band_keys = 2 * ATTN_BLOCK
    j = pl.program_id(1)

    @pl.when(j == 0)
    def _():
        k4_s[:, 0:ATTN_BLOCK, :] = jnp.zeros((4, ATTN_BLOCK, LANES), BF16)
        v4_s[:, 0:ATTN_BLOCK, :] = jnp.zeros((4, ATTN_BLOCK, LANES), BF16)

    norm_scale = n1w_ref[...] * (1.0 + _mod_row(mod_ref, 1))
    shift1 = _mod_row(mod_ref, 0)
    hb_chunks, q_chunks = [], []
    for r in range(tm // NORM_ROW_CHUNK):
        xr = x_ref[0, r * NORM_ROW_CHUNK:(r + 1) * NORM_ROW_CHUNK, :]
        hr = (_rms_norm(xr, norm_scale) + shift1).astype(BF16)
        hb_chunks.append(hr)
        q_chunks.append(_dot(hr, win_ref[:, 0:Q_END]))
    hb = jnp.concatenate(hb_chunks, axis=0)
    q = jnp.concatenate(q_chunks, axis=0)
    kv = _dot(hb, win_ref[:, Q_END:V_END])
    z_v = _dot(hb, win_ref[:, V_END + SGU_WIDTH:Z_END])
    gates = _dot(hb, win_ref[:, Z_END:IN_COLS])
    z_u = _dot(hb, win_ref[:, V_END:V_END + SGU_WIDTH])

    cos_t = cos_ref[0]
    sin_t = sin_ref[0]
    lane = lax.broadcasted_iota(jnp.int32, (tm, LANES), 1)
    first_half = (lane & (HEAD_DIM - 1)) < (ROT_DIM // 2)
    left_head = lane < HEAD_DIM

    def rope(t):
        partner = jnp.where(first_half,
                            pltpu.roll(t, LANES - ROT_DIM // 2, 1),
                            pltpu.roll(t, ROT_DIM // 2, 1))
        return t * cos_t + partner * sin_t

    for c in range(Q_END // LANES):
        qc = rope(q[:, c * LANES:(c + 1) * LANES]) * (HEAD_DIM ** -0.5 * LOG2E)
        q_s[:, c * LANES:(c + 1) * LANES] = qc.astype(BF16)

    k = rope(kv[:, 0:LANES])
    v = kv[:, LANES:2 * LANES]
    for src, dst in ((k, k4_s), (v, v4_s)):
        swapped = pltpu.roll(src, HEAD_DIM, 1)
        zero = jnp.zeros_like(src)
        rows = slice(ATTN_BLOCK, ATTN_BLOCK + tm)
        dst[0, rows, :] = jnp.where(left_head, src, zero).astype(BF16)
        dst[1, rows, :] = jnp.where(left_head, zero, swapped).astype(BF16)
        dst[2, rows, :] = jnp.where(left_head, swapped, zero).astype(BF16)
        dst[3, rows, :] = jnp.where(left_head, zero, src).astype(BF16)

    def gelu(t):
        return 0.5 * t * (1.0 + lax.erf(t * (2.0 ** -0.5)))

    v_sgu = gelu(z_v)
    u = gelu(z_u)
    mu = jnp.mean(v_sgu, axis=-1, keepdims=True)
    dv = v_sgu - mu
    var = jnp.mean(dv * dv, axis=-1, keepdims=True)
    vn = (dv * lax.rsqrt(var + NORM_EPS) * lnw_ref[...] + lnb_ref[...]).astype(BF16)
    ti = lax.broadcasted_iota(jnp.int32, (SGU_CHUNK, SGU_CHUNK), 0)
    si = lax.broadcasted_iota(jnp.int32, (SGU_CHUNK, SGU_CHUNK), 1)
    causal = si <= ti
    gdim = SGU_WIDTH // SGU_GROUPS
    nchunk = tm // SGU_CHUNK
    for g in range(SGU_GROUPS):
        cols = slice(g * gdim, (g + 1) * gdim)
        wm = jnp.where(causal, sguw_ref[g], 0.0).astype(BF16)
        rhs = jnp.concatenate(
            [vn[c * SGU_CHUNK:(c + 1) * SGU_CHUNK, cols] for c in range(nchunk)], axis=1)
        f = _dot(wm, rhs)
        for c in range(nchunk):
            rows = slice(c * SGU_CHUNK, (c + 1) * SGU_CHUNK)
            fc = f[:, c * gdim:(c + 1) * gdim] + sgub_ref[:, cols]
            ysgu_s[rows, cols] = (u[rows, cols] * fc).astype(BF16)

    sgu_part = jax.nn.sigmoid(gates[:, 0:D_MODEL]) * _dot(ysgu_s[...], pa_ref[...])
    gate_b = jax.nn.sigmoid(gates[:, D_MODEL:])

    er = lax.broadcasted_iota(jnp.int32, (ATTN_BLOCK, LANES), 0)
    ec = lax.broadcasted_iota(jnp.int32, (ATTN_BLOCK, LANES), 1)
    eye = jnp.where(er == ec, 1.0, 0.0).astype(BF16)
    kr = lax.broadcasted_iota(jnp.int32, (band_keys, ATTN_BLOCK), 0)
    qc_i = lax.broadcasted_iota(jnp.int32, (band_keys, ATTN_BLOCK), 1)
    band_t = (kr > qc_i) & (kr <= qc_i + ATTN_BLOCK)
    for idx, ok in ((0, band_t & (kr >= ATTN_BLOCK)), (1, band_t)):
        bias_t = jnp.where(ok, 0.0, MASK_VALUE).astype(BF16)
        bias_s[idx, 0:band_keys, :] = bias_t
        bias_s[idx, band_keys:2 * band_keys, :] = bias_t
    sr = lax.broadcasted_iota(jnp.int32, (2 * band_keys, LANES), 0)
    sc = lax.broadcasted_iota(jnp.int32, (2 * band_keys, LANES), 1)
    ones_sel = jnp.where((sr < band_keys) == (sc < HEAD_DIM), 1.0, 0.0).astype(BF16)
    sink_value_row = (sr == 0) | (sr == band_keys)
    sink_lane = ec == 0
    contract_last = (((1,), (1,)), ((), ()))

    def score_matmul(b, g):
        r0 = b * ATTN_BLOCK
        bias_t = bias_s[jnp.where(j > 0, 1, 0)] if b == 0 else bias_s[1]
        kbd = jnp.concatenate([k4_s[2 * g, pl.ds(r0, band_keys), :],
                               k4_s[2 * g + 1, pl.ds(r0, band_keys), :]], axis=0)
        kbd = jnp.concatenate([kbd, bias_t], axis=1)
        qrows = []
        for pr in range(Q_PER_KV // 2):
            col = (g * Q_PER_KV + 2 * pr) * HEAD_DIM
            qrows.append(jnp.concatenate([q_s[pl.ds(r0, ATTN_BLOCK), col:col + LANES], eye], axis=1))
        return lax.dot_general(jnp.concatenate(qrows, axis=0), kbd, contract_last,
                               preferred_element_type=F32)

    def softmax_value_matmul(b, g, scores):
        r0 = b * ATTN_BLOCK
        vbd = jnp.concatenate([v4_s[2 * g, pl.ds(r0, band_keys), :],
                               v4_s[2 * g + 1, pl.ds(r0, band_keys), :]], axis=0)
        vbd = jnp.where(sink_value_row, jnp.zeros_like(vbd), vbd)
        value_op = jnp.concatenate([vbd, ones_sel], axis=1)
        probs = []
        for pr in range(Q_PER_KV // 2):
            rows = slice(pr * ATTN_BLOCK, (pr + 1) * ATTN_BLOCK)
            pair_probs = []
            for hh in range(2):
                c0 = hh * band_keys
                sink = sink_ref[g * Q_PER_KV + 2 * pr + hh] * LOG2E
                first = jnp.where(sink_lane, sink, scores[rows, c0:c0 + LANES])
                rest = scores[rows, c0 + LANES:c0 + band_keys]
                m = jnp.max(jnp.maximum(first, rest), axis=-1, keepdims=True)
                pair_probs += [jnp.exp2(first - m).astype(BF16), jnp.exp2(rest - m).astype(BF16)]
            probs.append(jnp.concatenate(pair_probs, axis=1))
        o = _dot(jnp.concatenate(probs, axis=0), value_op)
        for pr in range(Q_PER_KV // 2):
            rows = slice(pr * ATTN_BLOCK, (pr + 1) * ATTN_BLOCK)
            col = (g * Q_PER_KV + 2 * pr) * HEAD_DIM
            yattn_s[pl.ds(r0, ATTN_BLOCK), col:col + LANES] = (
                o[rows, 0:LANES] / o[rows, LANES:]).astype(BF16)

    units = [(b, g) for b in range(nblk) for g in range(N_KV_HEADS)]
    pending = [score_matmul(*u) for u in units[:ATTN_LOOKAHEAD]]
    for i, (b, g) in enumerate(units):
        if i + ATTN_LOOKAHEAD < len(units):
            pending.append(score_matmul(*units[i + ATTN_LOOKAHEAD]))
        softmax_value_matmul(b, g, pending.pop(0))

    k4_s[:, 0:ATTN_BLOCK, :] = k4_s[:, tm:tm + ATTN_BLOCK, :]
    v4_s[:, 0:ATTN_BLOCK, :] = v4_s[:, tm:tm + ATTN_BLOCK, :]

    merged = (sgu_part + gate_b * _dot(yattn_s[...], pb_ref[...])).astype(BF16)
    for r in range(tm // OUT_ROW_CHUNK):
        rows = slice(r * OUT_ROW_CHUNK, (r + 1) * OUT_ROW_CHUNK)
        o_ref[0, rows, :] = x_ref[0, rows, :] + _mod_row(mod_ref, 2) * _dot(merged[rows, :], wo_ref[...])


def _full(shape):
    zeros = (0,) * len(shape)
    return pl.BlockSpec(shape, lambda i, j: zeros)


def _mix_layer(x, mod, n1w, cos_tab, sin_tab, w_in, sinks, ln_w, ln_b, sgu_w, sgu_b_full,
               proj_a, proj_b, w_out, mod_layer, cast_weights, layer):
    b, s, d = x.shape
    tm = TOKEN_TILE
    tile = pl.BlockSpec((1, tm, d), lambda i, j: (i, j, 0))
    tab = pl.BlockSpec((1, tm, LANES), lambda i, j: (i, j, 0))
    grid = (b, s // tm)
    cast_in, cast_out, cast_shapes = _cast_specs(cast_weights, layer, grid)
    return pl.pallas_call(
        functools.partial(_mix_kernel, n_cast=len(cast_weights)),
        grid=grid,
        in_specs=[
            tile,
            pl.BlockSpec((None,) + mod.shape[1:], lambda i, j: (mod_layer, 0, 0)),
            _full((1, d)),
            tab, tab,
            _full((d, IN_COLS)),
            pl.BlockSpec(memory_space=pltpu.SMEM),
            _full((1, SGU_WIDTH)), _full((1, SGU_WIDTH)),
            _full((SGU_GROUPS, SGU_CHUNK, SGU_CHUNK)),
            _full((SGU_CHUNK, SGU_WIDTH)),
            _full((SGU_WIDTH, d)), _full((Q_END, d)), _full((d, d)),
        ] + cast_in,
        out_specs=[tile] + cast_out,
        out_shape=[jax.ShapeDtypeStruct(x.shape, x.dtype)] + cast_shapes,
        scratch_shapes=[
            pltpu.VMEM((tm, Q_END), BF16),
            pltpu.VMEM((4, tm + ATTN_BLOCK, LANES), BF16),
            pltpu.VMEM((4, tm + ATTN_BLOCK, LANES), BF16),
            pltpu.VMEM((2, 4 * ATTN_BLOCK, LANES), BF16),
            pltpu.VMEM((tm, Q_END), BF16),
            pltpu.VMEM((tm, SGU_WIDTH), BF16),
        ],
        compiler_params=pltpu.CompilerParams(
            dimension_semantics=("arbitrary", "arbitrary"),
            vmem_limit_bytes=VMEM_LIMIT_BYTES),
        name="token_mixer",
    )(x, mod, n1w, cos_tab, sin_tab, w_in, sinks, ln_w, ln_b, sgu_w, sgu_b_full,
      proj_a, proj_b, w_out, *cast_weights)


CONV_PAD_ROWS = SUBLANES


FFN_N_INPUTS = 9


def _ffn_kernel(*refs, n_cast, final_norm):
    (x_ref, mod_ref, n2w_ref, wg_ref, wu_ref, cw_ref, cb_ref, wd_ref, fnw_ref) = refs[:FFN_N_INPUTS]
    o_ref = refs[FFN_N_INPUTS + n_cast]
    a_s, carry_s, act_s = refs[FFN_N_INPUTS + 2 * n_cast + 1:]
    _cast_slabs(refs[FFN_N_INPUTS:FFN_N_INPUTS + n_cast],
                refs[FFN_N_INPUTS + n_cast + 1:FFN_N_INPUTS + 2 * n_cast + 1])
    tm = FFN_TOKEN_TILE
    j = pl.program_id(1)

    @pl.when(j == 0)
    def _():
        carry_s[...] = jnp.zeros((CONV_PAD_ROWS, FFN_DIM), F32)

    norm_scale = n2w_ref[...] * (1.0 + _mod_row(mod_ref, 4))
    shift2 = _mod_row(mod_ref, 3)
    hb_chunks, a_chunks, up_chunks = [], [], []
    for r in range(tm // FFN_NORM_ROW_CHUNK):
        xr = x_ref[0, r * FFN_NORM_ROW_CHUNK:(r + 1) * FFN_NORM_ROW_CHUNK, :]
        hr = (_rms_norm(xr, norm_scale) + shift2).astype(BF16)
        hb_chunks.append(hr)
        a_chunks.append(_dot(hr, wg_ref[:, 0:MXU_COLS]))
        up_chunks.append(_dot(hr, wu_ref[:, 0:MXU_COLS]))
    hb = jnp.concatenate(hb_chunks, axis=0)
    first_a = jnp.concatenate(a_chunks, axis=0)
    first_up = jnp.concatenate(up_chunks, axis=0)

    for c in range(FFN_DIM // MXU_COLS):
        cols = slice(c * MXU_COLS, (c + 1) * MXU_COLS)
        a = first_a if c == 0 else _dot(hb, wg_ref[:, cols])
        stage = a_s.at[c % 2]
        stage[0:CONV_PAD_ROWS, :] = carry_s[:, cols]
        stage[CONV_PAD_ROWS:CONV_PAD_ROWS + tm, :] = a
        conv = cb_ref[:, cols] + cw_ref[2:3, cols] * a
        for tap in range(CONV_WIDTH - 1):
            back = CONV_WIDTH - 1 - tap
            conv = conv + cw_ref[tap:tap + 1, cols] * stage[CONV_PAD_ROWS - back:CONV_PAD_ROWS - back + tm, :]
        carry_s[:, cols] = stage[tm:tm + CONV_PAD_ROWS, :]
        up = first_up if c == 0 else _dot(hb, wu_ref[:, cols])
        act_s[:, cols] = (conv * jax.nn.sigmoid(conv) * up).astype(BF16)

    for r in range(tm // OUT_ROW_CHUNK):
        rows = slice(r * OUT_ROW_CHUNK, (r + 1) * OUT_ROW_CHUNK)
        y = x_ref[0, rows, :] + _mod_row(mod_ref, 5) * _dot(act_s[rows, :], wd_ref[...])
        if final_norm:
            y = _rms_norm(y, fnw_ref[...])
        o_ref[0, rows, :] = y


def _ffn_layer(x, mod, n2w, w_gate, w_up, conv_w, conv_b, w_down, final_w, final_norm,
               mod_layer, cast_weights, layer):
    b, s, d = x.shape
    tm = FFN_TOKEN_TILE
    tile = pl.BlockSpec((1, tm, d), lambda i, j: (i, j, 0))
    grid = (b, s // tm)
    cast_in, cast_out, cast_shapes = _cast_specs(cast_weights, layer, grid)
    return pl.pallas_call(
        functools.partial(_ffn_kernel, n_cast=len(cast_weights), final_norm=final_norm),
        grid=grid,
        in_specs=[
            tile,
            pl.BlockSpec((None,) + mod.shape[1:], lambda i, j: (mod_layer, 0, 0)),
            _full((1, d)),
            _full((d, FFN_DIM)), _full((d, FFN_DIM)),
            _full((CONV_WIDTH, FFN_DIM)), _full((1, FFN_DIM)),
            _full((FFN_DIM, d)),
            _full((1, d)),
        ] + cast_in,
        out_specs=[tile] + cast_out,
        out_shape=[jax.ShapeDtypeStruct(x.shape, x.dtype)] + cast_shapes,
        scratch_shapes=[
            pltpu.VMEM((2, CONV_PAD_ROWS + tm, MXU_COLS), F32),
            pltpu.VMEM((CONV_PAD_ROWS, FFN_DIM), F32),
            pltpu.VMEM((tm, FFN_DIM), BF16),
        ],
        compiler_params=pltpu.CompilerParams(
            dimension_semantics=("arbitrary", "arbitrary"),
            vmem_limit_bytes=VMEM_LIMIT_BYTES),
        name="conv_ffn",
    )(x, mod, n2w, w_gate, w_up, conv_w, conv_b, w_down, final_w, *cast_weights)


def kernel(x, c, positions, ada_w, ada_b, norm1_w, w_in, attn_sinks, sgu_ln_w, sgu_ln_b, sgu_w, sgu_b,
           proj_a, proj_b, w_out, norm2_w, ffn_w_gate, ffn_w_up, ffn_conv_w, ffn_conv_b, ffn_w_down,
           final_norm_w):
    batch, seq, d = x.shape
    depth = ada_w.shape[0]
    assert seq % TOKEN_TILE == 0 and seq % FFN_TOKEN_TILE == 0 and TOKEN_TILE % ATTN_BLOCK == 0
    assert d == D_MODEL

    mix_f32 = (w_in, proj_a, proj_b, w_out)
    ffn_f32 = (ffn_w_gate, ffn_w_up, ffn_w_down)
    rows = -(-batch // SUBLANES) * SUBLANES
    c_pad = jnp.pad(c, ((0, rows - batch), (0, 0)))
    mod, cos_tab, sin_tab, *mix_w = _prepare(c_pad, ada_w, ada_b, positions, mix_f32)

    for l in range(depth):
        sgu_b_full = jnp.repeat(sgu_b[l].T, SGU_WIDTH // SGU_GROUPS, axis=1)
        x, *ffn_w = _mix_layer(
            x, mod, norm1_w[l].reshape(1, d), cos_tab, sin_tab, mix_w[0],
            attn_sinks[l], sgu_ln_w[l].reshape(1, -1), sgu_ln_b[l].reshape(1, -1),
            sgu_w[l], sgu_b_full, mix_w[1], mix_w[2], mix_w[3], l, ffn_f32, l)
        last = l == depth - 1
        x, *mix_w = _ffn_layer(
            x, mod, norm2_w[l].reshape(1, d), ffn_w[0], ffn_w[1],
            ffn_conv_w[l], ffn_conv_b[l].reshape(1, -1), ffn_w[2],
            final_norm_w.reshape(1, d), last, l, () if last else mix_f32, l + 1)
    return x
```

```python
import functools

import jax
import jax.numpy as jnp
from jax import lax
from jax.experimental import pallas as pl
from jax.experimental.pallas import tpu as pltpu

D_MODEL = 1024
N_Q_HEADS = 16
N_KV_HEADS = 2
HEAD_DIM = 64
Q_PER_KV = N_Q_HEADS // N_KV_HEADS
ATTN_BLOCK = 128
ROPE_THETA = 500000.0
ROT_DIM = HEAD_DIM // 4
SGU_WIDTH = 1024
SGU_GROUPS = 8
SGU_CHUNK = 128
FFN_DIM = 2816
CONV_WIDTH = 3
NORM_EPS = 1e-6

Q_END = N_Q_HEADS * HEAD_DIM
K_END = Q_END + N_KV_HEADS * HEAD_DIM
V_END = K_END + N_KV_HEADS * HEAD_DIM
Z_END = V_END + 2 * SGU_WIDTH
IN_COLS = Z_END + 2 * D_MODEL

LANES = 128
SUBLANES = 8
MXU_COLS = 256
TOKEN_TILE = 1024
MIX_SUB_TILE = 512
FFN_TOKEN_TILE = 1024
NORM_ROW_CHUNK = 128
FFN_NORM_ROW_CHUNK = 256
OUT_ROW_CHUNK = 256
ATTN_LOOKAHEAD = 2
VMEM_LIMIT_BYTES = 56 * 1024 * 1024
MASK_VALUE = -1e30
LOG2E = 1.4426950408889634

F32 = jnp.float32
BF16 = jnp.bfloat16


def _rms_norm(x, w):
    ms = jnp.mean(x * x, axis=-1, keepdims=True)
    return x * lax.rsqrt(ms + NORM_EPS) * w


def _mod_row(mod_ref, part):
    return mod_ref[pl.ds(pl.program_id(0), 1), part * D_MODEL:(part + 1) * D_MODEL]


def _dot(a, b):
    return jnp.dot(a, b, preferred_element_type=F32)


BF16_ROW_TILE = 16


def _cast_specs(stacked_weights, layer, grid):
    steps = grid[0] * grid[1]
    specs, shapes = [], []
    for w in stacked_weights:
        _, rows, cols = w.shape
        slab, share = rows // steps, 1
        while slab % BF16_ROW_TILE:
            slab, share = slab * 2, share * 2
        assert rows % slab == 0 and steps % share == 0
        idx = lambda i, j, share=share: ((i * grid[1] + j) // share, 0)
        specs.append((pl.BlockSpec((None, slab, cols), lambda i, j, idx=idx: (layer,) + idx(i, j)),
                      pl.BlockSpec((slab, cols), idx)))
        shapes.append(jax.ShapeDtypeStruct((rows, cols), BF16))
    return [s[0] for s in specs], [s[1] for s in specs], shapes


def _cast_slabs(src_refs, dst_refs):
    for src, dst in zip(src_refs, dst_refs):
        dst[...] = src[...].astype(BF16)


ROPE_ROW_TILE = 1024


def _split_bf16(v):
    hi = v.astype(BF16)
    return hi, (v - hi.astype(F32)).astype(BF16)


def _prep_kernel(c_ref, adaw_ref, adab_ref, pos_ref, invf_ref, *refs):
    n_cast = (len(refs) - 3) // 2
    mod_ref, cos_ref, sin_ref = refs[n_cast:n_cast + 3]
    _cast_slabs(refs[:n_cast], refs[n_cast + 3:])

    c = c_ref[...]
    rows = c.shape[0]
    c_hi, c_lo = _split_bf16(c * jax.nn.sigmoid(c))
    lhs = jnp.concatenate([c_hi, c_lo], axis=0)
    w_hi, w_lo = _split_bf16(adaw_ref[...])
    parts = _dot(lhs, w_hi) + _dot(lhs, w_lo)
    mod_ref[...] = parts[0:rows] + parts[rows:2 * rows] + adab_ref[...]

    nfreq = ROT_DIM // 2
    freq_row = lax.broadcasted_iota(jnp.int32, (nfreq, LANES), 0)
    inv_freq = jnp.zeros((nfreq, LANES), F32)
    for f in range(nfreq):
        inv_freq = jnp.where(freq_row == f, invf_ref[f], inv_freq)
    part_freq = lax.broadcasted_iota(jnp.int32, (2 * nfreq, LANES), 0) & (nfreq - 1)
    part_lane = lax.broadcasted_iota(jnp.int32, (2 * nfreq, LANES), 1) & (HEAD_DIM - 1)
    uses = (part_lane < ROT_DIM) & ((part_lane & (nfreq - 1)) == part_freq)
    cos_pattern = jnp.where(uses, 1.0, 0.0).astype(BF16)
    sin_pattern = jnp.where(uses, jnp.where(part_lane < nfreq, -1.0, 1.0), 0.0).astype(BF16)
    head_lane = lax.broadcasted_iota(jnp.int32, (LANES, LANES), 1) & (HEAD_DIM - 1)
    contract_first = (((0,), (0,)), ((), ()))

    def expand(values, pattern):
        hi = values.astype(BF16).astype(F32)
        parts = jnp.concatenate([hi, values - hi], axis=0).astype(BF16)
        return lax.dot_general(parts, pattern, contract_first, preferred_element_type=F32)

    for r in range(pos_ref.shape[1]):
        ang = inv_freq * pos_ref[0, r:r + 1, :].astype(F32)
        rows_r = slice(r * LANES, (r + 1) * LANES)
        cos_ref[0, rows_r, :] = jnp.where(head_lane < ROT_DIM, expand(jnp.cos(ang), cos_pattern), 1.0)
        sin_ref[0, rows_r, :] = expand(jnp.sin(ang), sin_pattern)


def _prepare(c, ada_w, ada_b, positions, cast_weights):
    b, s = positions.shape
    depth, d, n = ada_w.shape
    rows = c.shape[0]
    inv_freq = ROPE_THETA ** (-jnp.arange(0, ROT_DIM, 2, dtype=F32) / ROT_DIM)
    tab = jax.ShapeDtypeStruct((b, s, LANES), F32)
    grid = (b, s // ROPE_ROW_TILE)
    steps = grid[0] * grid[1]
    assert steps % depth == 0 and n % (steps // depth) == 0
    col_tiles = steps // depth
    ada_cols = n // col_tiles
    ada_idx = lambda i, j: ((i * grid[1] + j) // col_tiles, 0, (i * grid[1] + j) % col_tiles)
    cast_in, cast_out, cast_shapes = _cast_specs(cast_weights, 0, grid)
    return pl.pallas_call(
        _prep_kernel,
        grid=grid,
        in_specs=[
            pl.BlockSpec((rows, d), lambda i, j: (0, 0)),
            pl.BlockSpec((None, d, ada_cols), ada_idx),
            pl.BlockSpec((None, 1, ada_cols), ada_idx),
            pl.BlockSpec((1, ROPE_ROW_TILE // LANES, LANES), lambda i, j: (i, j, 0)),
            pl.BlockSpec(memory_space=pltpu.SMEM),
        ] + cast_in,
        out_specs=[pl.BlockSpec((None, rows, ada_cols), ada_idx)]
        + [pl.BlockSpec((1, ROPE_ROW_TILE, LANES), lambda i, j: (i, j, 0))] * 2 + cast_out,
        out_shape=[jax.ShapeDtypeStruct((depth, rows, n), F32), tab, tab] + cast_shapes,
        compiler_params=pltpu.CompilerParams(
            dimension_semantics=("arbitrary", "arbitrary"),
            vmem_limit_bytes=VMEM_LIMIT_BYTES),
        name="prepare",
    )(c, ada_w, ada_b.reshape(depth, 1, n), positions.reshape(b, s // LANES, LANES), inv_freq,
      *cast_weights)


MIX_N_INPUTS = 14


def _mix_kernel(*refs, n_cast):
    ins = refs[:MIX_N_INPUTS]
    o_ref = refs[MIX_N_INPUTS + n_cast]
    scratch = refs[MIX_N_INPUTS + 2 * n_cast + 1:]
    k4_s, v4_s = scratch[1:3]
    _cast_slabs(refs[MIX_N_INPUTS:MIX_N_INPUTS + n_cast],
                refs[MIX_N_INPUTS + n_cast + 1:MIX_N_INPUTS + 2 * n_cast + 1])
    j = pl.program_id(1)

    @pl.when(j == 0)
    def _():
        k4_s[:, 0:ATTN_BLOCK, :] = jnp.zeros((4, ATTN_BLOCK, LANES), BF16)
        v4_s[:, 0:ATTN_BLOCK, :] = jnp.zeros((4, ATTN_BLOCK, LANES), BF16)

    for sub in range(TOKEN_TILE // MIX_SUB_TILE):
        _mix_sub_tile(*ins, o_ref, *scratch, base=sub * MIX_SUB_TILE,
                      seq_start=(j == 0) if sub == 0 else None)


def _mix_sub_tile(x_ref, mod_ref, n1w_ref, cos_ref, sin_ref, win_ref, sink_ref, lnw_ref, lnb_ref,
                  sguw_ref, sgub_ref, pa_ref, pb_ref, wo_ref, o_ref,
                  q_s, k4_s, v4_s, bias_s, yattn_s, ysgu_s, *, base, seq_start):
    tm = MIX_SUB_TILE
    nblk = tm // ATTN_BLOCK
    band_keys = 2 * ATTN_BLOCK

    norm_scale = n1w_ref[...] * (1.0 + _mod_row(mod_ref, 1))
    shift1 = _mod_row(mod_ref, 0)
    hb_chunks, q_chunks = [], []
    for r in range(tm // NORM_ROW_CHUNK):
        xr = x_ref[0, base + r * NORM_ROW_CHUNK:base + (r + 1) * NORM_ROW_CHUNK, :]
        hr = (_rms_norm(xr, norm_scale) + shift1).astype(BF16)
        hb_chunks.append(hr)
        q_chunks.append(_dot(hr, win_ref[:, 0:Q_END]))
    hb = jnp.concatenate(hb_chunks, axis=0)
    q = jnp.concatenate(q_chunks, axis=0)
    kv = _dot(hb, win_ref[:, Q_END:V_END])
    z_v = _dot(hb, win_ref[:, V_END + SGU_WIDTH:Z_END])
    gates = _dot(hb, win_ref[:, Z_END:IN_COLS])
    z_u = _dot(hb, win_ref[:, V_END:V_END + SGU_WIDTH])

    cos_t = cos_ref[0, base:base + tm, :]
    sin_t = sin_ref[0, base:base + tm, :]
    lane = lax.broadcasted_iota(jnp.int32, (tm, LANES), 1)
    first_half = (lane & (HEAD_DIM - 1)) < (ROT_DIM // 2)
    left_head = lane < HEAD_DIM

    def rope(t):
        partner = jnp.where(first_half,
                            pltpu.roll(t, LANES - ROT_DIM // 2, 1),
                            pltpu.roll(t, ROT_DIM // 2, 1))
        return t * cos_t + partner * sin_t

    for c in range(Q_END // LANES):
        qc = rope(q[:, c * LANES:(c + 1) * LANES]) * (HEAD_DIM ** -0.5 * LOG2E)
        q_s[:, c * LANES:(c + 1) * LANES] = qc.astype(BF16)

    k = rope(kv[:, 0:LANES])
    v = kv[:, LANES:2 * LANES]
    for src, dst in ((k, k4_s), (v, v4_s)):
        swapped = pltpu.roll(src, HEAD_DIM, 1)
        zero = jnp.zeros_like(src)
        rows = slice(ATTN_BLOCK, ATTN_BLOCK + tm)
        dst[0, rows, :] = jnp.where(left_head, src, zero).astype(BF16)
        dst[1, rows, :] = jnp.where(left_head, zero, swapped).astype(BF16)
        dst[2, rows, :] = jnp.where(left_head, swapped, zero).astype(BF16)
        dst[3, rows, :] = jnp.where(left_head, zero, src).astype(BF16)

    def gelu(t):
        return 0.5 * t * (1.0 + lax.erf(t * (2.0 ** -0.5)))

    v_sgu = gelu(z_v)
    u = gelu(z_u)
    mu = jnp.mean(v_sgu, axis=-1, keepdims=True)
    dv = v_sgu - mu
    var = jnp.mean(dv * dv, axis=-1, keepdims=True)
    vn = (dv * lax.rsqrt(var + NORM_EPS) * lnw_ref[...] + lnb_ref[...]).astype(BF16)
    ti = lax.broadcasted_iota(jnp.int32, (SGU_CHUNK, SGU_CHUNK), 0)
    si = lax.broadcasted_iota(jnp.int32, (SGU_CHUNK, SGU_CHUNK), 1)
    causal = si <= ti
    gdim = SGU_WIDTH // SGU_GROUPS
    nchunk = tm // SGU_CHUNK
    for g in range(SGU_GROUPS):
        cols = slice(g * gdim, (g + 1) * gdim)
        wm = jnp.where(causal, sguw_ref[g], 0.0).astype(BF16)
        rhs = jnp.concatenate(
            [vn[c * SGU_CHUNK:(c + 1) * SGU_CHUNK, cols] for c in range(nchunk)], axis=1)
        f = _dot(wm, rhs)
        for c in range(nchunk):
            rows = slice(c * SGU_CHUNK, (c + 1) * SGU_CHUNK)
            fc = f[:, c * gdim:(c + 1) * gdim] + sgub_ref[:, cols]
            ysgu_s[rows, cols] = (u[rows, cols] * fc).astype(BF16)

    sgu_part = jax.nn.sigmoid(gates[:, 0:D_MODEL]) * _dot(ysgu_s[...], pa_ref[...])
    gate_b = jax.nn.sigmoid(gates[:, D_MODEL:])

    er = lax.broadcasted_iota(jnp.int32, (ATTN_BLOCK, LANES), 0)
    ec = lax.broadcasted_iota(jnp.int32, (ATTN_BLOCK, LANES), 1)
    eye = jnp.where(er == ec, 1.0, 0.0).astype(BF16)
    kr = lax.broadcasted_iota(jnp.int32, (band_keys, ATTN_BLOCK), 0)
    qc_i = lax.broadcasted_iota(jnp.int32, (band_keys, ATTN_BLOCK), 1)
    band_t = (kr > qc_i) & (kr <= qc_i + ATTN_BLOCK)
    for idx, ok in ((0, band_t & (kr >= ATTN_BLOCK)), (1, band_t)):
        bias_t = jnp.where(ok, 0.0, MASK_VALUE).astype(BF16)
        bias_s[idx, 0:band_keys, :] = bias_t
        bias_s[idx, band_keys:2 * band_keys, :] = bias_t
    sr = lax.broadcasted_iota(jnp.int32, (2 * band_keys, LANES), 0)
    sc = lax.broadcasted_iota(jnp.int32, (2 * band_keys, LANES), 1)
    ones_sel = jnp.where((sr < band_keys) == (sc < HEAD_DIM), 1.0, 0.0).astype(BF16)
    sink_value_row = (sr == 0) | (sr == band_keys)
    sink_lane = ec == 0
    contract_last = (((1,), (1,)), ((), ()))

    def score_matmul(b, g):
        r0 = b * ATTN_BLOCK
        no_prev = b == 0 and seq_start is not None
        bias_t = bias_s[jnp.where(seq_start, 0, 1)] if no_prev else bias_s[1]
        kbd = jnp.concatenate([k4_s[2 * g, pl.ds(r0, band_keys), :],
                               k4_s[2 * g + 1, pl.ds(r0, band_keys), :]], axis=0)
        kbd = jnp.concatenate([kbd, bias_t], axis=1)
        qrows = []
        for pr in range(Q_PER_KV // 2):
            col = (g * Q_PER_KV + 2 * pr) * HEAD_DIM
            qrows.append(jnp.concatenate([q_s[pl.ds(r0, ATTN_BLOCK), col:col + LANES], eye], axis=1))
        return lax.dot_general(jnp.concatenate(qrows, axis=0), kbd, contract_last,
                               preferred_element_type=F32)

    def softmax_value_matmul(b, g, scores):
        r0 = b * ATTN_BLOCK
        vbd = jnp.concatenate([v4_s[2 * g, pl.ds(r0, band_keys), :],
                               v4_s[2 * g + 1, pl.ds(r0, band_keys), :]], axis=0)
        vbd = jnp.where(sink_value_row, jnp.zeros_like(vbd), vbd)
        value_op = jnp.concatenate([vbd, ones_sel], axis=1)
        probs = []
        for pr in range(Q_PER_KV // 2):
            rows = slice(pr * ATTN_BLOCK, (pr + 1) * ATTN_BLOCK)
            pair_probs = []
            for hh in range(2):
                c0 = hh * band_keys
                sink = sink_ref[g * Q_PER_KV + 2 * pr + hh] * LOG2E
                first = jnp.where(sink_lane, sink, scores[rows, c0:c0 + LANES])
                rest = scores[rows, c0 + LANES:c0 + band_keys]
                m = jnp.max(jnp.maximum(first, rest), axis=-1, keepdims=True)
                pair_probs += [jnp.exp2(first - m).astype(BF16), jnp.exp2(rest - m).astype(BF16)]
            probs.append(jnp.concatenate(pair_probs, axis=1))
        o = _dot(jnp.concatenate(probs, axis=0), value_op)
        for pr in range(Q_PER_KV // 2):
            rows = slice(pr * ATTN_BLOCK, (pr + 1) * ATTN_BLOCK)
            col = (g * Q_PER_KV + 2 * pr) * HEAD_DIM
            yattn_s[pl.ds(r0, ATTN_BLOCK), col:col + LANES] = (
                o[rows, 0:LANES] / o[rows, LANES:]).astype(BF16)

    units = [(b, g) for b in range(nblk) for g in range(N_KV_HEADS)]
    pending = [score_matmul(*u) for u in units[:ATTN_LOOKAHEAD]]
    for i, (b, g) in enumerate(units):
        if i + ATTN_LOOKAHEAD < len(units):
            pending.append(score_matmul(*units[i + ATTN_LOOKAHEAD]))
        softmax_value_matmul(b, g, pending.pop(0))

    k4_s[:, 0:ATTN_BLOCK, :] = k4_s[:, tm:tm + ATTN_BLOCK, :]
    v4_s[:, 0:ATTN_BLOCK, :] = v4_s[:, tm:tm + ATTN_BLOCK, :]

    merged = (sgu_part + gate_b * _dot(yattn_s[...], pb_ref[...])).astype(BF16)
    for r in range(tm // OUT_ROW_CHUNK):
        rows = slice(r * OUT_ROW_CHUNK, (r + 1) * OUT_ROW_CHUNK)
        tile_rows = slice(base + rows.start, base + rows.stop)
        o_ref[0, tile_rows, :] = (x_ref[0, tile_rows, :]
                                  + _mod_row(mod_ref, 2) * _dot(merged[rows, :], wo_ref[...]))


def _full(shape):
    zeros = (0,) * len(shape)
    return pl.BlockSpec(shape, lambda i, j: zeros)


def _mix_layer(x, mod, n1w, cos_tab, sin_tab, w_in, sinks, ln_w, ln_b, sgu_w, sgu_b_full,
               proj_a, proj_b, w_out, mod_layer, cast_weights, layer):
    b, s, d = x.shape
    tm = TOKEN_TILE
    tile = pl.BlockSpec((1, tm, d), lambda i, j: (i, j, 0))
    tab = pl.BlockSpec((1, tm, LANES), lambda i, j: (i, j, 0))
    grid = (b, s // tm)
    cast_in, cast_out, cast_shapes = _cast_specs(cast_weights, layer, grid)
    return pl.pallas_call(
        functools.partial(_mix_kernel, n_cast=len(cast_weights)),
        grid=grid,
        in_specs=[
            tile,
            pl.BlockSpec((None,) + mod.shape[1:], lambda i, j: (mod_layer, 0, 0)),
            _full((1, d)),
            tab, tab,
            _full((d, IN_COLS)),
            pl.BlockSpec(memory_space=pltpu.SMEM),
            _full((1, SGU_WIDTH)), _full((1, SGU_WIDTH)),
            _full((SGU_GROUPS, SGU_CHUNK, SGU_CHUNK)),
            _full((SGU_CHUNK, SGU_WIDTH)),
            _full((SGU_WIDTH, d)), _full((Q_END, d)), _full((d, d)),
        ] + cast_in,
        out_specs=[tile] + cast_out,
        out_shape=[jax.ShapeDtypeStruct(x.shape, x.dtype)] + cast_shapes,
        scratch_shapes=[
            pltpu.VMEM((MIX_SUB_TILE, Q_END), BF16),
            pltpu.VMEM((4, MIX_SUB_TILE + ATTN_BLOCK, LANES), BF16),
            pltpu.VMEM((4, MIX_SUB_TILE + ATTN_BLOCK, LANES), BF16),
            pltpu.VMEM((2, 4 * ATTN_BLOCK, LANES), BF16),
            pltpu.VMEM((MIX_SUB_TILE, Q_END), BF16),
            pltpu.VMEM((MIX_SUB_TILE, SGU_WIDTH), BF16),
        ],
        compiler_params=pltpu.CompilerParams(
            dimension_semantics=("arbitrary", "arbitrary"),
            vmem_limit_bytes=VMEM_LIMIT_BYTES),
        name="token_mixer",
    )(x, mod, n1w, cos_tab, sin_tab, w_in, sinks, ln_w, ln_b, sgu_w, sgu_b_full,
      proj_a, proj_b, w_out, *cast_weights)


CONV_PAD_ROWS = SUBLANES


FFN_N_INPUTS = 9


def _ffn_kernel(*refs, n_cast, final_norm):
    (x_ref, mod_ref, n2w_ref, wg_ref, wu_ref, cw_ref, cb_ref, wd_ref, fnw_ref) = refs[:FFN_N_INPUTS]
    o_ref = refs[FFN_N_INPUTS + n_cast]
    a_s, carry_s, act_s = refs[FFN_N_INPUTS + 2 * n_cast + 1:]
    _cast_slabs(refs[FFN_N_INPUTS:FFN_N_INPUTS + n_cast],
                refs[FFN_N_INPUTS + n_cast + 1:FFN_N_INPUTS + 2 * n_cast + 1])
    tm = FFN_TOKEN_TILE
    j = pl.program_id(1)

    @pl.when(j == 0)
    def _():
        carry_s[...] = jnp.zeros((CONV_PAD_ROWS, FFN_DIM), F32)

    norm_scale = n2w_ref[...] * (1.0 + _mod_row(mod_ref, 4))
    shift2 = _mod_row(mod_ref, 3)
    hb_chunks, a_chunks, up_chunks = [], [], []
    for r in range(tm // FFN_NORM_ROW_CHUNK):
        xr = x_ref[0, r * FFN_NORM_ROW_CHUNK:(r + 1) * FFN_NORM_ROW_CHUNK, :]
        hr = (_rms_norm(xr, norm_scale) + shift2).astype(BF16)
        hb_chunks.append(hr)
        a_chunks.append(_dot(hr, wg_ref[:, 0:MXU_COLS]))
        up_chunks.append(_dot(hr, wu_ref[:, 0:MXU_COLS]))
    hb = jnp.concatenate(hb_chunks, axis=0)
    first_a = jnp.concatenate(a_chunks, axis=0)
    first_up = jnp.concatenate(up_chunks, axis=0)

    for c in range(FFN_DIM // MXU_COLS):
        cols = slice(c * MXU_COLS, (c + 1) * MXU_COLS)
        a = first_a if c == 0 else _dot(hb, wg_ref[:, cols])
        stage = a_s.at[c % 2]
        stage[0:CONV_PAD_ROWS, :] = carry_s[:, cols]
        stage[CONV_PAD_ROWS:CONV_PAD_ROWS + tm, :] = a
        conv = cb_ref[:, cols] + cw_ref[2:3, cols] * a
        for tap in range(CONV_WIDTH - 1):
            back = CONV_WIDTH - 1 - tap
            conv = conv + cw_ref[tap:tap + 1, cols] * stage[CONV_PAD_ROWS - back:CONV_PAD_ROWS - back + tm, :]
        carry_s[:, cols] = stage[tm:tm + CONV_PAD_ROWS, :]
        up = first_up if c == 0 else _dot(hb, wu_ref[:, cols])
        act_s[:, cols] = (conv * jax.nn.sigmoid(conv) * up).astype(BF16)

    for r in range(tm // OUT_ROW_CHUNK):
        rows = slice(r * OUT_ROW_CHUNK, (r + 1) * OUT_ROW_CHUNK)
        y = x_ref[0, rows, :] + _mod_row(mod_ref, 5) * _dot(act_s[rows, :], wd_ref[...])
        if final_norm:
            y = _rms_norm(y, fnw_ref[...])
        o_ref[0, rows, :] = y


def _ffn_layer(x, mod, n2w, w_gate, w_up, conv_w, conv_b, w_down, final_w, final_norm,
               mod_layer, cast_weights, layer):
    b, s, d = x.shape
    tm = FFN_TOKEN_TILE
    tile = pl.BlockSpec((1, tm, d), lambda i, j: (i, j, 0))
    grid = (b, s // tm)
    cast_in, cast_out, cast_shapes = _cast_specs(cast_weights, layer, grid)
    return pl.pallas_call(
        functools.partial(_ffn_kernel, n_cast=len(cast_weights), final_norm=final_norm),
        grid=grid,
        in_specs=[
            tile,
            pl.BlockSpec((None,) + mod.shape[1:], lambda i, j: (mod_layer, 0, 0)),
            _full((1, d)),
            _full((d, FFN_DIM)), _full((d, FFN_DIM)),
            _full((CONV_WIDTH, FFN_DIM)), _full((1, FFN_DIM)),
            _full((FFN_DIM, d)),
            _full((1, d)),
        ] + cast_in,
        out_specs=[tile] + cast_out,
        out_shape=[jax.ShapeDtypeStruct(x.shape, x.dtype)] + cast_shapes,
        scratch_shapes=[
            pltpu.VMEM((2, CONV_PAD_ROWS + tm, MXU_COLS), F32),
            pltpu.VMEM((CONV_PAD_ROWS, FFN_DIM), F32),
            pltpu.VMEM((tm, FFN_DIM), BF16),
        ],
        compiler_params=pltpu.CompilerParams(
            dimension_semantics=("arbitrary", "arbitrary"),
            vmem_limit_bytes=VMEM_LIMIT_BYTES),
        name="conv_ffn",
    )(x, mod, n2w, w_gate, w_up, conv_w, conv_b, w_down, final_w, *cast_weights)


def kernel(x, c, positions, ada_w, ada_b, norm1_w, w_in, attn_sinks, sgu_ln_w, sgu_ln_b, sgu_w, sgu_b,
           proj_a, proj_b, w_out, norm2_w, ffn_w_gate, ffn_w_up, ffn_conv_w, ffn_conv_b, ffn_w_down,
           final_norm_w):
    batch, seq, d = x.shape
    depth = ada_w.shape[0]
    assert seq % TOKEN_TILE == 0 and seq % FFN_TOKEN_TILE == 0
    assert TOKEN_TILE % MIX_SUB_TILE == 0 and MIX_SUB_TILE % ATTN_BLOCK == 0
    assert d == D_MODEL

    mix_f32 = (w_in, proj_a, proj_b, w_out)
    ffn_f32 = (ffn_w_gate, ffn_w_up, ffn_w_down)
    rows = -(-batch // SUBLANES) * SUBLANES
    c_pad = jnp.pad(c, ((0, rows - batch), (0, 0)))
    mod, cos_tab, sin_tab, *mix_w = _prepare(c_pad, ada_w, ada_b, positions, mix_f32)

    for l in range(depth):
        sgu_b_full = jnp.repeat(sgu_b[l].T, SGU_WIDTH // SGU_GROUPS, axis=1)
        x, *ffn_w = _mix_layer(
            x, mod, norm1_w[l].reshape(1, d), cos_tab, sin_tab, mix_w[0],
            attn_sinks[l], sgu_ln_w[l].reshape(1, -1), sgu_ln_b[l].reshape(1, -1),
            sgu_w[l], sgu_b_full, mix_w[1], mix_w[2], mix_w[3], l, ffn_f32, l)
        last = l == depth - 1
        x, *mix_w = _ffn_layer(
            x, mod, norm2_w[l].reshape(1, d), ffn_w[0], ffn_w[1],
            ffn_conv_w[l], ffn_conv_b[l].reshape(1, -1), ffn_w[2],
            final_norm_w.reshape(1, d), last, l, () if last else mix_f32, l + 1)
    return x
```

```python
import functools

import jax
import jax.numpy as jnp
from jax import lax
from jax.experimental import pallas as pl
from jax.experimental.pallas import tpu as pltpu

D_MODEL = 1024
N_Q_HEADS = 16
N_KV_HEADS = 2
HEAD_DIM = 64
Q_PER_KV = N_Q_HEADS // N_KV_HEADS
ATTN_BLOCK = 128
ROPE_THETA = 500000.0
ROT_DIM = HEAD_DIM // 4
SGU_WIDTH = 1024
SGU_GROUPS = 8
SGU_CHUNK = 128
FFN_DIM = 2816
CONV_WIDTH = 3
NORM_EPS = 1e-6

Q_END = N_Q_HEADS * HEAD_DIM
K_END = Q_END + N_KV_HEADS * HEAD_DIM
V_END = K_END + N_KV_HEADS * HEAD_DIM
Z_END = V_END + 2 * SGU_WIDTH
IN_COLS = Z_END + 2 * D_MODEL

LANES = 128
SUBLANES = 8
MXU_COLS = 256
TOKEN_TILE = 512
FFN_TOKEN_TILE = 1024
NORM_ROW_CHUNK = 128
FFN_NORM_ROW_CHUNK = 256
OUT_ROW_CHUNK = 256
ATTN_LOOKAHEAD = 2
VMEM_LIMIT_BYTES = 56 * 1024 * 1024
MASK_VALUE = -1e30
LOG2E = 1.4426950408889634

F32 = jnp.float32
BF16 = jnp.bfloat16


def _rms_norm(x, w):
    ms = jnp.mean(x * x, axis=-1, keepdims=True)
    return x * lax.rsqrt(ms + NORM_EPS) * w


def _mod_row(mod_ref, part):
    return mod_ref[pl.ds(pl.program_id(0), 1), part * D_MODEL:(part + 1) * D_MODEL]


def _dot(a, b):
    return jnp.dot(a, b, preferred_element_type=F32)


BF16_ROW_TILE = 16


def _cast_specs(stacked_weights, layer, grid):
    steps = grid[0] * grid[1]
    specs, shapes = [], []
    for w in stacked_weights:
        _, rows, cols = w.shape
        slab, share = rows // steps, 1
        while slab % BF16_ROW_TILE:
            slab, share = slab * 2, share * 2
        assert rows % slab == 0 and steps % share == 0
        idx = lambda i, j, share=share: ((i * grid[1] + j) // share, 0)
        specs.append((pl.BlockSpec((None, slab, cols), lambda i, j, idx=idx: (layer,) + idx(i, j)),
                      pl.BlockSpec((slab, cols), idx)))
        shapes.append(jax.ShapeDtypeStruct((rows, cols), BF16))
    return [s[0] for s in specs], [s[1] for s in specs], shapes


def _cast_slabs(src_refs, dst_refs):
    for src, dst in zip(src_refs, dst_refs):
        dst[...] = src[...].astype(BF16)


ROPE_ROW_TILE = 1024


def _split_bf16(v):
    hi = v.astype(BF16)
    return hi, (v - hi.astype(F32)).astype(BF16)


def _prep_kernel(c_ref, adaw_ref, adab_ref, pos_ref, invf_ref, *refs):
    n_cast = (len(refs) - 3) // 2
    mod_ref, cos_ref, sin_ref = refs[n_cast:n_cast + 3]
    _cast_slabs(refs[:n_cast], refs[n_cast + 3:])

    c = c_ref[...]
    rows = c.shape[0]
    c_hi, c_lo = _split_bf16(c * jax.nn.sigmoid(c))
    lhs = jnp.concatenate([c_hi, c_lo], axis=0)
    w_hi, w_lo = _split_bf16(adaw_ref[...])
    parts = _dot(lhs, w_hi) + _dot(lhs, w_lo)
    mod_ref[...] = parts[0:rows] + parts[rows:2 * rows] + adab_ref[...]

    nfreq = ROT_DIM // 2
    freq_row = lax.broadcasted_iota(jnp.int32, (nfreq, LANES), 0)
    inv_freq = jnp.zeros((nfreq, LANES), F32)
    for f in range(nfreq):
        inv_freq = jnp.where(freq_row == f, invf_ref[f], inv_freq)
    part_freq = lax.broadcasted_iota(jnp.int32, (2 * nfreq, LANES), 0) & (nfreq - 1)
    part_lane = lax.broadcasted_iota(jnp.int32, (2 * nfreq, LANES), 1) & (HEAD_DIM - 1)
    uses = (part_lane < ROT_DIM) & ((part_lane & (nfreq - 1)) == part_freq)
    cos_pattern = jnp.where(uses, 1.0, 0.0).astype(BF16)
    sin_pattern = jnp.where(uses, jnp.where(part_lane < nfreq, -1.0, 1.0), 0.0).astype(BF16)
    head_lane = lax.broadcasted_iota(jnp.int32, (LANES, LANES), 1) & (HEAD_DIM - 1)
    contract_first = (((0,), (0,)), ((), ()))

    def expand(values, pattern):
        hi = values.astype(BF16).astype(F32)
        parts = jnp.concatenate([hi, values - hi], axis=0).astype(BF16)
        return lax.dot_general(parts, pattern, contract_first, preferred_element_type=F32)

    for r in range(pos_ref.shape[1]):
        ang = inv_freq * pos_ref[0, r:r + 1, :].astype(F32)
        rows_r = slice(r * LANES, (r + 1) * LANES)
        cos_ref[0, rows_r, :] = jnp.where(head_lane < ROT_DIM, expand(jnp.cos(ang), cos_pattern), 1.0)
        sin_ref[0, rows_r, :] = expand(jnp.sin(ang), sin_pattern)


def _prepare(c, ada_w, ada_b, positions, cast_weights):
    b, s = positions.shape
    depth, d, n = ada_w.shape
    rows = c.shape[0]
    inv_freq = ROPE_THETA ** (-jnp.arange(0, ROT_DIM, 2, dtype=F32) / ROT_DIM)
    tab = jax.ShapeDtypeStruct((b, s, LANES), F32)
    grid = (b, s // ROPE_ROW_TILE)
    steps = grid[0] * grid[1]
    assert steps % depth == 0 and n % (steps // depth) == 0
    col_tiles = steps // depth
    ada_cols = n // col_tiles
    ada_idx = lambda i, j: ((i * grid[1] + j) // col_tiles, 0, (i * grid[1] + j) % col_tiles)
    cast_in, cast_out, cast_shapes = _cast_specs(cast_weights, 0, grid)
    return pl.pallas_call(
        _prep_kernel,
        grid=grid,
        in_specs=[
            pl.BlockSpec((rows, d), lambda i, j: (0, 0)),
            pl.BlockSpec((None, d, ada_cols), ada_idx),
            pl.BlockSpec((None, 1, ada_cols), ada_idx),
            pl.BlockSpec((1, ROPE_ROW_TILE // LANES, LANES), lambda i, j: (i, j, 0)),
            pl.BlockSpec(memory_space=pltpu.SMEM),
        ] + cast_in,
        out_specs=[pl.BlockSpec((None, rows, ada_cols), ada_idx)]
        + [pl.BlockSpec((1, ROPE_ROW_TILE, LANES), lambda i, j: (i, j, 0))] * 2 + cast_out,
        out_shape=[jax.ShapeDtypeStruct((depth, rows, n), F32), tab, tab] + cast_shapes,
        compiler_params=pltpu.CompilerParams(
            dimension_semantics=("arbitrary", "arbitrary"),
            vmem_limit_bytes=VMEM_LIMIT_BYTES),
        name="prepare",
    )(c, ada_w, ada_b.reshape(depth, 1, n), positions.reshape(b, s // LANES, LANES), inv_freq,
      *cast_weights)


MIX_N_INPUTS = 14


def _mix_kernel(*refs, n_cast, layer):
    (x_ref, mod_ref, n1w_ref, cos_ref, sin_ref, win_ref, sink_ref, lnw_ref, lnb_ref,
     sguw_ref, sgub_ref, pa_ref, pb_ref, wo_ref) = refs[:MIX_N_INPUTS]
    o_ref = refs[MIX_N_INPUTS + n_cast]
    q_s, k4_s, v4_s, bias_s, yattn_s, ysgu_s = refs[MIX_N_INPUTS + 2 * n_cast + 1:]
    _cast_slabs(refs[MIX_N_INPUTS:MIX_N_INPUTS + n_cast],
                refs[MIX_N_INPUTS + n_cast + 1:MIX_N_INPUTS + 2 * n_cast + 1])
    tm = TOKEN_TILE
    nblk = tm // ATTN_BLOCK
    band_keys = 2 * ATTN_BLOCK
    j = pl.program_id(1)

    @pl.when(j == 0)
    def _():
        k4_s[:, 0:ATTN_BLOCK, :] = jnp.zeros((4, ATTN_BLOCK, LANES), BF16)
        v4_s[:, 0:ATTN_BLOCK, :] = jnp.zeros((4, ATTN_BLOCK, LANES), BF16)

    norm_scale = n1w_ref[layer:layer + 1, :] * (1.0 + _mod_row(mod_ref, 1))
    shift1 = _mod_row(mod_ref, 0)
    hb_chunks, q_chunks = [], []
    for r in range(tm // NORM_ROW_CHUNK):
        xr = x_ref[0, r * NORM_ROW_CHUNK:(r + 1) * NORM_ROW_CHUNK, :]
        hr = (_rms_norm(xr, norm_scale) + shift1).astype(BF16)
        hb_chunks.append(hr)
        q_chunks.append(_dot(hr, win_ref[:, 0:Q_END]))
    hb = jnp.concatenate(hb_chunks, axis=0)
    q = jnp.concatenate(q_chunks, axis=0)
    kv = _dot(hb, win_ref[:, Q_END:V_END])
    z_v = _dot(hb, win_ref[:, V_END + SGU_WIDTH:Z_END])
    gates = _dot(hb, win_ref[:, Z_END:IN_COLS])
    z_u = _dot(hb, win_ref[:, V_END:V_END + SGU_WIDTH])

    cos_t = cos_ref[0]
    sin_t = sin_ref[0]
    lane = lax.broadcasted_iota(jnp.int32, (tm, LANES), 1)
    first_half = (lane & (HEAD_DIM - 1)) < (ROT_DIM // 2)
    left_head = lane < HEAD_DIM

    def rope(t):
        partner = jnp.where(first_half,
                            pltpu.roll(t, LANES - ROT_DIM // 2, 1),
                            pltpu.roll(t, ROT_DIM // 2, 1))
        return t * cos_t + partner * sin_t

    for c in range(Q_END // LANES):
        qc = rope(q[:, c * LANES:(c + 1) * LANES]) * (HEAD_DIM ** -0.5 * LOG2E)
        q_s[:, c * LANES:(c + 1) * LANES] = qc.astype(BF16)

    k = rope(kv[:, 0:LANES])
    v = kv[:, LANES:2 * LANES]
    for src, dst in ((k, k4_s), (v, v4_s)):
        swapped = pltpu.roll(src, HEAD_DIM, 1)
        zero = jnp.zeros_like(src)
        rows = slice(ATTN_BLOCK, ATTN_BLOCK + tm)
        dst[0, rows, :] = jnp.where(left_head, src, zero).astype(BF16)
        dst[1, rows, :] = jnp.where(left_head, zero, swapped).astype(BF16)
        dst[2, rows, :] = jnp.where(left_head, swapped, zero).astype(BF16)
        dst[3, rows, :] = jnp.where(left_head, zero, src).astype(BF16)

    def gelu(t):
        return 0.5 * t * (1.0 + lax.erf(t * (2.0 ** -0.5)))

    v_sgu = gelu(z_v)
    u = gelu(z_u)
    mu = jnp.mean(v_sgu, axis=-1, keepdims=True)
    dv = v_sgu - mu
    var = jnp.mean(dv * dv, axis=-1, keepdims=True)
    vn = (dv * lax.rsqrt(var + NORM_EPS) * lnw_ref[layer:layer + 1, :]
          + lnb_ref[layer:layer + 1, :]).astype(BF16)
    ti = lax.broadcasted_iota(jnp.int32, (SGU_CHUNK, SGU_CHUNK), 0)
    si = lax.broadcasted_iota(jnp.int32, (SGU_CHUNK, SGU_CHUNK), 1)
    causal = si <= ti
    gdim = SGU_WIDTH // SGU_GROUPS
    nchunk = tm // SGU_CHUNK
    for g in range(SGU_GROUPS):
        cols = slice(g * gdim, (g + 1) * gdim)
        wm = jnp.where(causal, sguw_ref[g], 0.0).astype(BF16)
        rhs = jnp.concatenate(
            [vn[c * SGU_CHUNK:(c + 1) * SGU_CHUNK, cols] for c in range(nchunk)], axis=1)
        f = _dot(wm, rhs)
        for c in range(nchunk):
            rows = slice(c * SGU_CHUNK, (c + 1) * SGU_CHUNK)
            fc = f[:, c * gdim:(c + 1) * gdim] + sgub_ref[:, cols]
            ysgu_s[rows, cols] = (u[rows, cols] * fc).astype(BF16)

    sgu_part = jax.nn.sigmoid(gates[:, 0:D_MODEL]) * _dot(ysgu_s[...], pa_ref[...])
    gate_b = jax.nn.sigmoid(gates[:, D_MODEL:])

    er = lax.broadcasted_iota(jnp.int32, (ATTN_BLOCK, LANES), 0)
    ec = lax.broadcasted_iota(jnp.int32, (ATTN_BLOCK, LANES), 1)
    eye = jnp.where(er == ec, 1.0, 0.0).astype(BF16)
    kr = lax.broadcasted_iota(jnp.int32, (band_keys, ATTN_BLOCK), 0)
    qc_i = lax.broadcasted_iota(jnp.int32, (band_keys, ATTN_BLOCK), 1)
    band_t = (kr > qc_i) & (kr <= qc_i + ATTN_BLOCK)
    for idx, ok in ((0, band_t & (kr >= ATTN_BLOCK)), (1, band_t)):
        bias_t = jnp.where(ok, 0.0, MASK_VALUE).astype(BF16)
        bias_s[idx, 0:band_keys, :] = bias_t
        bias_s[idx, band_keys:2 * band_keys, :] = bias_t
    sr = lax.broadcasted_iota(jnp.int32, (2 * band_keys, LANES), 0)
    sc = lax.broadcasted_iota(jnp.int32, (2 * band_keys, LANES), 1)
    ones_sel = jnp.where((sr < band_keys) == (sc < HEAD_DIM), 1.0, 0.0).astype(BF16)
    sink_value_row = (sr == 0) | (sr == band_keys)
    sink_lane = ec == 0
    contract_last = (((1,), (1,)), ((), ()))

    def score_matmul(b, g):
        r0 = b * ATTN_BLOCK
        bias_t = bias_s[jnp.where(j > 0, 1, 0)] if b == 0 else bias_s[1]
        kbd = jnp.concatenate([k4_s[2 * g, pl.ds(r0, band_keys), :],
                               k4_s[2 * g + 1, pl.ds(r0, band_keys), :]], axis=0)
        kbd = jnp.concatenate([kbd, bias_t], axis=1)
        qrows = []
        for pr in range(Q_PER_KV // 2):
            col = (g * Q_PER_KV + 2 * pr) * HEAD_DIM
            qrows.append(jnp.concatenate([q_s[pl.ds(r0, ATTN_BLOCK), col:col + LANES], eye], axis=1))
        return lax.dot_general(jnp.concatenate(qrows, axis=0), kbd, contract_last,
                               preferred_element_type=F32)

    def softmax_value_matmul(b, g, scores):
        r0 = b * ATTN_BLOCK
        vbd = jnp.concatenate([v4_s[2 * g, pl.ds(r0, band_keys), :],
                               v4_s[2 * g + 1, pl.ds(r0, band_keys), :]], axis=0)
        vbd = jnp.where(sink_value_row, jnp.zeros_like(vbd), vbd)
        value_op = jnp.concatenate([vbd, ones_sel], axis=1)
        probs = []
        for pr in range(Q_PER_KV // 2):
            rows = slice(pr * ATTN_BLOCK, (pr + 1) * ATTN_BLOCK)
            pair_probs = []
            for hh in range(2):
                c0 = hh * band_keys
                sink = sink_ref[layer, g * Q_PER_KV + 2 * pr + hh] * LOG2E
                first = jnp.where(sink_lane, sink, scores[rows, c0:c0 + LANES])
                rest = scores[rows, c0 + LANES:c0 + band_keys]
                m = jnp.max(jnp.maximum(first, rest), axis=-1, keepdims=True)
                pair_probs += [jnp.exp2(first - m).astype(BF16), jnp.exp2(rest - m).astype(BF16)]
            probs.append(jnp.concatenate(pair_probs, axis=1))
        o = _dot(jnp.concatenate(probs, axis=0), value_op)
        for pr in range(Q_PER_KV // 2):
            rows = slice(pr * ATTN_BLOCK, (pr + 1) * ATTN_BLOCK)
            col = (g * Q_PER_KV + 2 * pr) * HEAD_DIM
            yattn_s[pl.ds(r0, ATTN_BLOCK), col:col + LANES] = (
                o[rows, 0:LANES] / o[rows, LANES:]).astype(BF16)

    units = [(b, g) for b in range(nblk) for g in range(N_KV_HEADS)]
    pending = [score_matmul(*u) for u in units[:ATTN_LOOKAHEAD]]
    for i, (b, g) in enumerate(units):
        if i + ATTN_LOOKAHEAD < len(units):
            pending.append(score_matmul(*units[i + ATTN_LOOKAHEAD]))
        softmax_value_matmul(b, g, pending.pop(0))

    k4_s[:, 0:ATTN_BLOCK, :] = k4_s[:, tm:tm + ATTN_BLOCK, :]
    v4_s[:, 0:ATTN_BLOCK, :] = v4_s[:, tm:tm + ATTN_BLOCK, :]

    merged = (sgu_part + gate_b * _dot(yattn_s[...], pb_ref[...])).astype(BF16)
    for r in range(tm // OUT_ROW_CHUNK):
        rows = slice(r * OUT_ROW_CHUNK, (r + 1) * OUT_ROW_CHUNK)
        o_ref[0, rows, :] = x_ref[0, rows, :] + _mod_row(mod_ref, 2) * _dot(merged[rows, :], wo_ref[...])


def _full(shape):
    zeros = (0,) * len(shape)
    return pl.BlockSpec(shape, lambda i, j: zeros)


def _mix_layer(x, mod, n1w, cos_tab, sin_tab, w_in, sinks, ln_w, ln_b, sgu_w, sgu_b_full,
               proj_a, proj_b, w_out, mod_layer, cast_weights, layer):
    b, s, d = x.shape
    tm = TOKEN_TILE
    tile = pl.BlockSpec((1, tm, d), lambda i, j: (i, j, 0))
    tab = pl.BlockSpec((1, tm, LANES), lambda i, j: (i, j, 0))
    grid = (b, s // tm)
    cast_in, cast_out, cast_shapes = _cast_specs(cast_weights, layer, grid)
    return pl.pallas_call(
        functools.partial(_mix_kernel, n_cast=len(cast_weights), layer=mod_layer),
        grid=grid,
        in_specs=[
            tile,
            pl.BlockSpec((None,) + mod.shape[1:], lambda i, j: (mod_layer, 0, 0)),
            _full(n1w.shape),
            tab, tab,
            _full((d, IN_COLS)),
            pl.BlockSpec(memory_space=pltpu.SMEM),
            _full(ln_w.shape), _full(ln_b.shape),
            pl.BlockSpec((None,) + sgu_w.shape[1:], lambda i, j: (mod_layer, 0, 0, 0)),
            _full((SGU_CHUNK, SGU_WIDTH)),
            _full((SGU_WIDTH, d)), _full((Q_END, d)), _full((d, d)),
        ] + cast_in,
        out_specs=[tile] + cast_out,
        out_shape=[jax.ShapeDtypeStruct(x.shape, x.dtype)] + cast_shapes,
        scratch_shapes=[
            pltpu.VMEM((tm, Q_END), BF16),
            pltpu.VMEM((4, tm + ATTN_BLOCK, LANES), BF16),
            pltpu.VMEM((4, tm + ATTN_BLOCK, LANES), BF16),
            pltpu.VMEM((2, 4 * ATTN_BLOCK, LANES), BF16),
            pltpu.VMEM((tm, Q_END), BF16),
            pltpu.VMEM((tm, SGU_WIDTH), BF16),
        ],
        compiler_params=pltpu.CompilerParams(
            dimension_semantics=("arbitrary", "arbitrary"),
            vmem_limit_bytes=VMEM_LIMIT_BYTES),
        name="token_mixer",
    )(x, mod, n1w, cos_tab, sin_tab, w_in, sinks, ln_w, ln_b, sgu_w, sgu_b_full,
      proj_a, proj_b, w_out, *cast_weights)


CONV_PAD_ROWS = SUBLANES


FFN_N_INPUTS = 9


def _ffn_kernel(*refs, n_cast, layer, final_norm):
    (x_ref, mod_ref, n2w_ref, wg_ref, wu_ref, cw_ref, cb_ref, wd_ref, fnw_ref) = refs[:FFN_N_INPUTS]
    o_ref = refs[FFN_N_INPUTS + n_cast]
    a_s, carry_s, act_s = refs[FFN_N_INPUTS + 2 * n_cast + 1:]
    _cast_slabs(refs[FFN_N_INPUTS:FFN_N_INPUTS + n_cast],
                refs[FFN_N_INPUTS + n_cast + 1:FFN_N_INPUTS + 2 * n_cast + 1])
    tm = FFN_TOKEN_TILE
    j = pl.program_id(1)

    @pl.when(j == 0)
    def _():
        carry_s[...] = jnp.zeros((CONV_PAD_ROWS, FFN_DIM), F32)

    norm_scale = n2w_ref[layer:layer + 1, :] * (1.0 + _mod_row(mod_ref, 4))
    shift2 = _mod_row(mod_ref, 3)
    hb_chunks, a_chunks, up_chunks = [], [], []
    for r in range(tm // FFN_NORM_ROW_CHUNK):
        xr = x_ref[0, r * FFN_NORM_ROW_CHUNK:(r + 1) * FFN_NORM_ROW_CHUNK, :]
        hr = (_rms_norm(xr, norm_scale) + shift2).astype(BF16)
        hb_chunks.append(hr)
        a_chunks.append(_dot(hr, wg_ref[:, 0:MXU_COLS]))
        up_chunks.append(_dot(hr, wu_ref[:, 0:MXU_COLS]))
    hb = jnp.concatenate(hb_chunks, axis=0)
    first_a = jnp.concatenate(a_chunks, axis=0)
    first_up = jnp.concatenate(up_chunks, axis=0)

    for c in range(FFN_DIM // MXU_COLS):
        cols = slice(c * MXU_COLS, (c + 1) * MXU_COLS)
        a = first_a if c == 0 else _dot(hb, wg_ref[:, cols])
        stage = a_s.at[c % 2]
        stage[0:CONV_PAD_ROWS, :] = carry_s[:, cols]
        stage[CONV_PAD_ROWS:CONV_PAD_ROWS + tm, :] = a
        conv = cb_ref[layer:layer + 1, cols] + cw_ref[2:3, cols] * a
        for tap in range(CONV_WIDTH - 1):
            back = CONV_WIDTH - 1 - tap
            conv = conv + cw_ref[tap:tap + 1, cols] * stage[CONV_PAD_ROWS - back:CONV_PAD_ROWS - back + tm, :]
        carry_s[:, cols] = stage[tm:tm + CONV_PAD_ROWS, :]
        up = first_up if c == 0 else _dot(hb, wu_ref[:, cols])
        act_s[:, cols] = (conv * jax.nn.sigmoid(conv) * up).astype(BF16)

    for r in range(tm // OUT_ROW_CHUNK):
        rows = slice(r * OUT_ROW_CHUNK, (r + 1) * OUT_ROW_CHUNK)
        y = x_ref[0, rows, :] + _mod_row(mod_ref, 5) * _dot(act_s[rows, :], wd_ref[...])
        if final_norm:
            y = _rms_norm(y, fnw_ref[...])
        o_ref[0, rows, :] = y


def _ffn_layer(x, mod, n2w, w_gate, w_up, conv_w, conv_b, w_down, final_w, final_norm,
               mod_layer, cast_weights, layer):
    b, s, d = x.shape
    tm = FFN_TOKEN_TILE
    tile = pl.BlockSpec((1, tm, d), lambda i, j: (i, j, 0))
    grid = (b, s // tm)
    cast_in, cast_out, cast_shapes = _cast_specs(cast_weights, layer, grid)
    return pl.pallas_call(
        functools.partial(_ffn_kernel, n_cast=len(cast_weights), layer=mod_layer,
                          final_norm=final_norm),
        grid=grid,
        in_specs=[
            tile,
            pl.BlockSpec((None,) + mod.shape[1:], lambda i, j: (mod_layer, 0, 0)),
            _full(n2w.shape),
            _full((d, FFN_DIM)), _full((d, FFN_DIM)),
            pl.BlockSpec((None,) + conv_w.shape[1:], lambda i, j: (mod_layer, 0, 0)),
            _full(conv_b.shape),
            _full((FFN_DIM, d)),
            _full((1, d)),
        ] + cast_in,
        out_specs=[tile] + cast_out,
        out_shape=[jax.ShapeDtypeStruct(x.shape, x.dtype)] + cast_shapes,
        scratch_shapes=[
            pltpu.VMEM((2, CONV_PAD_ROWS + tm, MXU_COLS), F32),
            pltpu.VMEM((CONV_PAD_ROWS, FFN_DIM), F32),
            pltpu.VMEM((tm, FFN_DIM), BF16),
        ],
        compiler_params=pltpu.CompilerParams(
            dimension_semantics=("arbitrary", "arbitrary"),
            vmem_limit_bytes=VMEM_LIMIT_BYTES),
        name="conv_ffn",
    )(x, mod, n2w, w_gate, w_up, conv_w, conv_b, w_down, final_w, *cast_weights)


def kernel(x, c, positions, ada_w, ada_b, norm1_w, w_in, attn_sinks, sgu_ln_w, sgu_ln_b, sgu_w, sgu_b,
           proj_a, proj_b, w_out, norm2_w, ffn_w_gate, ffn_w_up, ffn_conv_w, ffn_conv_b, ffn_w_down,
           final_norm_w):
    batch, seq, d = x.shape
    depth = ada_w.shape[0]
    assert seq % TOKEN_TILE == 0 and seq % FFN_TOKEN_TILE == 0 and TOKEN_TILE % ATTN_BLOCK == 0
    assert d == D_MODEL

    mix_f32 = (w_in, proj_a, proj_b, w_out)
    ffn_f32 = (ffn_w_gate, ffn_w_up, ffn_w_down)
    rows = -(-batch // SUBLANES) * SUBLANES
    c_pad = jnp.pad(c, ((0, rows - batch), (0, 0)))
    mod, cos_tab, sin_tab, *mix_w = _prepare(c_pad, ada_w, ada_b, positions, mix_f32)

    for l in range(depth):
        sgu_b_full = jnp.repeat(sgu_b[l].T, SGU_WIDTH // SGU_GROUPS, axis=1)
        x, *ffn_w = _mix_layer(
            x, mod, norm1_w, cos_tab, sin_tab, mix_w[0], attn_sinks, sgu_ln_w, sgu_ln_b,
            sgu_w, sgu_b_full, mix_w[1], mix_w[2], mix_w[3], l, ffn_f32, l)
        last = l == depth - 1
        x, *mix_w = _ffn_layer(
            x, mod, norm2_w, ffn_w[0], ffn_w[1], ffn_conv_w, ffn_conv_b, ffn_w[2],
            final_norm_w.reshape(1, d), last, l, () if last else mix_f32, l + 1)
    return x
```

```python
import functools

import jax
import jax.numpy as jnp
from jax import lax
from jax.experimental import pallas as pl
from jax.experimental.pallas import tpu as pltpu

D_MODEL = 1024
N_Q_HEADS = 16
N_KV_HEADS = 2
HEAD_DIM = 64
Q_PER_KV = N_Q_HEADS // N_KV_HEADS
ATTN_BLOCK = 128
ROPE_THETA = 500000.0
ROT_DIM = HEAD_DIM // 4
SGU_WIDTH = 1024
SGU_GROUPS = 8
SGU_CHUNK = 128
FFN_DIM = 2816
CONV_WIDTH = 3
NORM_EPS = 1e-6

Q_END = N_Q_HEADS * HEAD_DIM
K_END = Q_END + N_KV_HEADS * HEAD_DIM
V_END = K_END + N_KV_HEADS * HEAD_DIM
Z_END = V_END + 2 * SGU_WIDTH
IN_COLS = Z_END + 2 * D_MODEL

LANES = 128
SUBLANES = 8
MXU_COLS = 256
TOKEN_TILE = 512
FFN_TOKEN_TILE = 1024
NORM_ROW_CHUNK = 128
FFN_NORM_ROW_CHUNK = 256
OUT_ROW_CHUNK = 256
ATTN_LOOKAHEAD = 2
VMEM_LIMIT_BYTES = 56 * 1024 * 1024
MASK_VALUE = -1e30
LOG2E = 1.4426950408889634

F32 = jnp.float32
BF16 = jnp.bfloat16


def _rms_norm(x, w):
    ms = jnp.mean(x * x, axis=-1, keepdims=True)
    return x * lax.rsqrt(ms + NORM_EPS) * w


def _mod_row(mod_ref, part):
    return mod_ref[pl.ds(pl.program_id(0), 1), part * D_MODEL:(part + 1) * D_MODEL]


def _dot(a, b):
    return jnp.dot(a, b, preferred_element_type=F32)


BF16_ROW_TILE = 16


def _cast_specs(stacked_weights, layer, grid):
    steps = grid[0] * grid[1]
    specs, shapes = [], []
    for w in stacked_weights:
        _, rows, cols = w.shape
        slab, share = rows // steps, 1
        while slab % BF16_ROW_TILE:
            slab, share = slab * 2, share * 2
        assert rows % slab == 0 and steps % share == 0
        idx = lambda i, j, share=share: ((i * grid[1] + j) // share, 0)
        specs.append((pl.BlockSpec((None, slab, cols), lambda i, j, idx=idx: (layer,) + idx(i, j)),
                      pl.BlockSpec((slab, cols), idx)))
        shapes.append(jax.ShapeDtypeStruct((rows, cols), BF16))
    return [s[0] for s in specs], [s[1] for s in specs], shapes


def _cast_slabs(src_refs, dst_refs):
    for src, dst in zip(src_refs, dst_refs):
        dst[...] = src[...].astype(BF16)


ROPE_ROW_TILE = 2048


def _split_bf16(v):
    hi = v.astype(BF16)
    return hi, (v - hi.astype(F32)).astype(BF16)


def _prep_kernel(c_ref, adaw_ref, adab_ref, pos_ref, invf_ref, *refs):
    n_cast = (len(refs) - 3) // 2
    mod_ref, cos_ref, sin_ref = refs[n_cast:n_cast + 3]
    _cast_slabs(refs[:n_cast], refs[n_cast + 3:])

    c = c_ref[...]
    rows = c.shape[0]
    c_hi, c_lo = _split_bf16(c * jax.nn.sigmoid(c))
    lhs = jnp.concatenate([c_hi, c_lo], axis=0)
    w_hi, w_lo = _split_bf16(adaw_ref[...])
    parts = _dot(lhs, w_hi) + _dot(lhs, w_lo)
    mod_ref[...] = parts[0:rows] + parts[rows:2 * rows] + adab_ref[...]

    nfreq = ROT_DIM // 2
    freq_row = lax.broadcasted_iota(jnp.int32, (nfreq, LANES), 0)
    inv_freq = jnp.zeros((nfreq, LANES), F32)
    for f in range(nfreq):
        inv_freq = jnp.where(freq_row == f, invf_ref[f], inv_freq)
    part_freq = lax.broadcasted_iota(jnp.int32, (2 * nfreq, LANES), 0) & (nfreq - 1)
    part_lane = lax.broadcasted_iota(jnp.int32, (2 * nfreq, LANES), 1) & (HEAD_DIM - 1)
    uses = (part_lane < ROT_DIM) & ((part_lane & (nfreq - 1)) == part_freq)
    cos_pattern = jnp.where(uses, 1.0, 0.0).astype(BF16)
    sin_pattern = jnp.where(uses, jnp.where(part_lane < nfreq, -1.0, 1.0), 0.0).astype(BF16)
    head_lane = lax.broadcasted_iota(jnp.int32, (LANES, LANES), 1) & (HEAD_DIM - 1)
    contract_first = (((0,), (0,)), ((), ()))

    def expand(values, pattern):
        hi = values.astype(BF16).astype(F32)
        parts = jnp.concatenate([hi, values - hi], axis=0).astype(BF16)
        return lax.dot_general(parts, pattern, contract_first, preferred_element_type=F32)

    for r in range(pos_ref.shape[1]):
        ang = inv_freq * pos_ref[0, r:r + 1, :].astype(F32)
        rows_r = slice(r * LANES, (r + 1) * LANES)
        cos_ref[0, rows_r, :] = jnp.where(head_lane < ROT_DIM, expand(jnp.cos(ang), cos_pattern), 1.0)
        sin_ref[0, rows_r, :] = expand(jnp.sin(ang), sin_pattern)


def _prepare(c, ada_w, ada_b, positions, cast_weights):
    b, s = positions.shape
    depth, d, n = ada_w.shape
    rows = c.shape[0]
    inv_freq = ROPE_THETA ** (-jnp.arange(0, ROT_DIM, 2, dtype=F32) / ROT_DIM)
    tab = jax.ShapeDtypeStruct((b, s, LANES), F32)
    grid = (b, s // ROPE_ROW_TILE)
    steps = grid[0] * grid[1]
    assert steps % depth == 0 and n % (steps // depth) == 0
    col_tiles = steps // depth
    ada_cols = n // col_tiles
    ada_idx = lambda i, j: ((i * grid[1] + j) // col_tiles, 0, (i * grid[1] + j) % col_tiles)
    cast_in, cast_out, cast_shapes = _cast_specs(cast_weights, 0, grid)
    return pl.pallas_call(
        _prep_kernel,
        grid=grid,
        in_specs=[
            pl.BlockSpec((rows, d), lambda i, j: (0, 0)),
            pl.BlockSpec((None, d, ada_cols), ada_idx),
            pl.BlockSpec((None, 1, ada_cols), ada_idx),
            pl.BlockSpec((1, ROPE_ROW_TILE // LANES, LANES), lambda i, j: (i, j, 0)),
            pl.BlockSpec(memory_space=pltpu.SMEM),
        ] + cast_in,
        out_specs=[pl.BlockSpec((None, rows, ada_cols), ada_idx)]
        + [pl.BlockSpec((1, ROPE_ROW_TILE, LANES), lambda i, j: (i, j, 0))] * 2 + cast_out,
        out_shape=[jax.ShapeDtypeStruct((depth, rows, n), F32), tab, tab] + cast_shapes,
        compiler_params=pltpu.CompilerParams(
            dimension_semantics=("arbitrary", "arbitrary"),
            vmem_limit_bytes=VMEM_LIMIT_BYTES),
        name="prepare",
    )(c, ada_w, ada_b.reshape(depth, 1, n), positions.reshape(b, s // LANES, LANES), inv_freq,
      *cast_weights)


MIX_N_INPUTS = 14


def _mix_kernel(*refs, n_cast, layer):
    (x_ref, mod_ref, n1w_ref, cos_ref, sin_ref, win_ref, sink_ref, lnw_ref, lnb_ref,
     sguw_ref, sgub_ref, pa_ref, pb_ref, wo_ref) = refs[:MIX_N_INPUTS]
    o_ref = refs[MIX_N_INPUTS + n_cast]
    q_s, k4_s, v4_s, bias_s, yattn_s, ysgu_s = refs[MIX_N_INPUTS + 2 * n_cast + 1:]
    _cast_slabs(refs[MIX_N_INPUTS:MIX_N_INPUTS + n_cast],
                refs[MIX_N_INPUTS + n_cast + 1:MIX_N_INPUTS + 2 * n_cast + 1])
    tm = TOKEN_TILE
    nblk = tm // ATTN_BLOCK
    band_keys = 2 * ATTN_BLOCK
    j = pl.program_id(1)

    @pl.when(j == 0)
    def _():
        k4_s[:, 0:ATTN_BLOCK, :] = jnp.zeros((4, ATTN_BLOCK, LANES), BF16)
        v4_s[:, 0:ATTN_BLOCK, :] = jnp.zeros((4, ATTN_BLOCK, LANES), BF16)

    norm_scale = n1w_ref[layer:layer + 1, :] * (1.0 + _mod_row(mod_ref, 1))
    shift1 = _mod_row(mod_ref, 0)
    hb_chunks, q_chunks = [], []
    for r in range(tm // NORM_ROW_CHUNK):
        xr = x_ref[0, r * NORM_ROW_CHUNK:(r + 1) * NORM_ROW_CHUNK, :]
        hr = (_rms_norm(xr, norm_scale) + shift1).astype(BF16)
        hb_chunks.append(hr)
        q_chunks.append(_dot(hr, win_ref[:, 0:Q_END]))
    hb = jnp.concatenate(hb_chunks, axis=0)
    q = jnp.concatenate(q_chunks, axis=0)
    kv = _dot(hb, win_ref[:, Q_END:V_END])
    z_v = _dot(hb, win_ref[:, V_END + SGU_WIDTH:Z_END])
    gates = _dot(hb, win_ref[:, Z_END:IN_COLS])
    z_u = _dot(hb, win_ref[:, V_END:V_END + SGU_WIDTH])

    cos_t = cos_ref[0]
    sin_t = sin_ref[0]
    lane = lax.broadcasted_iota(jnp.int32, (tm, LANES), 1)
    first_half = (lane & (HEAD_DIM - 1)) < (ROT_DIM // 2)
    left_head = lane < HEAD_DIM

    def rope(t):
        partner = jnp.where(first_half,
                            pltpu.roll(t, LANES - ROT_DIM // 2, 1),
                            pltpu.roll(t, ROT_DIM // 2, 1))
        return t * cos_t + partner * sin_t

    for c in range(Q_END // LANES):
        qc = rope(q[:, c * LANES:(c + 1) * LANES]) * (HEAD_DIM ** -0.5 * LOG2E)
        q_s[:, c * LANES:(c + 1) * LANES] = qc.astype(BF16)

    k = rope(kv[:, 0:LANES])
    v = kv[:, LANES:2 * LANES]
    for src, dst in ((k, k4_s), (v, v4_s)):
        swapped = pltpu.roll(src, HEAD_DIM, 1)
        zero = jnp.zeros_like(src)
        rows = slice(ATTN_BLOCK, ATTN_BLOCK + tm)
        dst[0, rows, :] = jnp.where(left_head, src, zero).astype(BF16)
        dst[1, rows, :] = jnp.where(left_head, zero, swapped).astype(BF16)
        dst[2, rows, :] = jnp.where(left_head, swapped, zero).astype(BF16)
        dst[3, rows, :] = jnp.where(left_head, zero, src).astype(BF16)

    def gelu(t):
        return 0.5 * t * (1.0 + lax.erf(t * (2.0 ** -0.5)))

    v_sgu = gelu(z_v)
    u = gelu(z_u)
    mu = jnp.mean(v_sgu, axis=-1, keepdims=True)
    dv = v_sgu - mu
    var = jnp.mean(dv * dv, axis=-1, keepdims=True)
    vn = (dv * lax.rsqrt(var + NORM_EPS) * lnw_ref[layer:layer + 1, :]
          + lnb_ref[layer:layer + 1, :]).astype(BF16)
    ti = lax.broadcasted_iota(jnp.int32, (SGU_CHUNK, SGU_CHUNK), 0)
    si = lax.broadcasted_iota(jnp.int32, (SGU_CHUNK, SGU_CHUNK), 1)
    causal = si <= ti
    gdim = SGU_WIDTH // SGU_GROUPS
    nchunk = tm // SGU_CHUNK
    for g in range(SGU_GROUPS):
        cols = slice(g * gdim, (g + 1) * gdim)
        wm = jnp.where(causal, sguw_ref[g], 0.0).astype(BF16)
        rhs = jnp.concatenate(
            [vn[c * SGU_CHUNK:(c + 1) * SGU_CHUNK, cols] for c in range(nchunk)], axis=1)
        f = _dot(wm, rhs)
        for c in range(nchunk):
            rows = slice(c * SGU_CHUNK, (c + 1) * SGU_CHUNK)
            fc = f[:, c * gdim:(c + 1) * gdim] + sgub_ref[:, cols]
            ysgu_s[rows, cols] = (u[rows, cols] * fc).astype(BF16)

    sgu_part = jax.nn.sigmoid(gates[:, 0:D_MODEL]) * _dot(ysgu_s[...], pa_ref[...])
    gate_b = jax.nn.sigmoid(gates[:, D_MODEL:])

    er = lax.broadcasted_iota(jnp.int32, (ATTN_BLOCK, LANES), 0)
    ec = lax.broadcasted_iota(jnp.int32, (ATTN_BLOCK, LANES), 1)
    eye = jnp.where(er == ec, 1.0, 0.0).astype(BF16)
    kr = lax.broadcasted_iota(jnp.int32, (band_keys, ATTN_BLOCK), 0)
    qc_i = lax.broadcasted_iota(jnp.int32, (band_keys, ATTN_BLOCK), 1)
    band_t = (kr > qc_i) & (kr <= qc_i + ATTN_BLOCK)
    for idx, ok in ((0, band_t & (kr >= ATTN_BLOCK)), (1, band_t)):
        bias_t = jnp.where(ok, 0.0, MASK_VALUE).astype(BF16)
        bias_s[idx, 0:band_keys, :] = bias_t
        bias_s[idx, band_keys:2 * band_keys, :] = bias_t
    sr = lax.broadcasted_iota(jnp.int32, (2 * band_keys, LANES), 0)
    sc = lax.broadcasted_iota(jnp.int32, (2 * band_keys, LANES), 1)
    ones_sel = jnp.where((sr < band_keys) == (sc < HEAD_DIM), 1.0, 0.0).astype(BF16)
    sink_value_row = (sr == 0) | (sr == band_keys)
    sink_lane = ec == 0
    contract_last = (((1,), (1,)), ((), ()))

    def score_matmul(b, g):
        r0 = b * ATTN_BLOCK
        bias_t = bias_s[jnp.where(j > 0, 1, 0)] if b == 0 else bias_s[1]
        kbd = jnp.concatenate([k4_s[2 * g, pl.ds(r0, band_keys), :],
                               k4_s[2 * g + 1, pl.ds(r0, band_keys), :]], axis=0)
        kbd = jnp.concatenate([kbd, bias_t], axis=1)
        qrows = []
        for pr in range(Q_PER_KV // 2):
            col = (g * Q_PER_KV + 2 * pr) * HEAD_DIM
            qrows.append(jnp.concatenate([q_s[pl.ds(r0, ATTN_BLOCK), col:col + LANES], eye], axis=1))
        return lax.dot_general(jnp.concatenate(qrows, axis=0), kbd, contract_last,
                               preferred_element_type=F32)

    def softmax_value_matmul(b, g, scores):
        r0 = b * ATTN_BLOCK
        vbd = jnp.concatenate([v4_s[2 * g, pl.ds(r0, band_keys), :],
                               v4_s[2 * g + 1, pl.ds(r0, band_keys), :]], axis=0)
        vbd = jnp.where(sink_value_row, jnp.zeros_like(vbd), vbd)
        value_op = jnp.concatenate([vbd, ones_sel], axis=1)
        probs = []
        for pr in range(Q_PER_KV // 2):
            rows = slice(pr * ATTN_BLOCK, (pr + 1) * ATTN_BLOCK)
            pair_probs = []
            for hh in range(2):
                c0 = hh * band_keys
                sink = sink_ref[layer, g * Q_PER_KV + 2 * pr + hh] * LOG2E
                first = jnp.where(sink_lane, sink, scores[rows, c0:c0 + LANES])
                rest = scores[rows, c0 + LANES:c0 + band_keys]
                m = jnp.max(jnp.maximum(first, rest), axis=-1, keepdims=True)
                pair_probs += [jnp.exp2(first - m).astype(BF16), jnp.exp2(rest - m).astype(BF16)]
            probs.append(jnp.concatenate(pair_probs, axis=1))
        o = _dot(jnp.concatenate(probs, axis=0), value_op)
        for pr in range(Q_PER_KV // 2):
            rows = slice(pr * ATTN_BLOCK, (pr + 1) * ATTN_BLOCK)
            col = (g * Q_PER_KV + 2 * pr) * HEAD_DIM
            yattn_s[pl.ds(r0, ATTN_BLOCK), col:col + LANES] = (
                o[rows, 0:LANES] / o[rows, LANES:]).astype(BF16)

    units = [(b, g) for b in range(nblk) for g in range(N_KV_HEADS)]
    pending = [score_matmul(*u) for u in units[:ATTN_LOOKAHEAD]]
    for i, (b, g) in enumerate(units):
        if i + ATTN_LOOKAHEAD < len(units):
            pending.append(score_matmul(*units[i + ATTN_LOOKAHEAD]))
        softmax_value_matmul(b, g, pending.pop(0))

    k4_s[:, 0:ATTN_BLOCK, :] = k4_s[:, tm:tm + ATTN_BLOCK, :]
    v4_s[:, 0:ATTN_BLOCK, :] = v4_s[:, tm:tm + ATTN_BLOCK, :]

    merged = (sgu_part + gate_b * _dot(yattn_s[...], pb_ref[...])).astype(BF16)
    for r in range(tm // OUT_ROW_CHUNK):
        rows = slice(r * OUT_ROW_CHUNK, (r + 1) * OUT_ROW_CHUNK)
        o_ref[0, rows, :] = x_ref[0, rows, :] + _mod_row(mod_ref, 2) * _dot(merged[rows, :], wo_ref[...])


def _full(shape):
    zeros = (0,) * len(shape)
    return pl.BlockSpec(shape, lambda i, j: zeros)


def _mix_layer(x, mod, n1w, cos_tab, sin_tab, w_in, sinks, ln_w, ln_b, sgu_w, sgu_b_full,
               proj_a, proj_b, w_out, mod_layer, cast_weights, layer):
    b, s, d = x.shape
    tm = TOKEN_TILE
    tile = pl.BlockSpec((1, tm, d), lambda i, j: (i, j, 0))
    tab = pl.BlockSpec((1, tm, LANES), lambda i, j: (i, j, 0))
    grid = (b, s // tm)
    cast_in, cast_out, cast_shapes = _cast_specs(cast_weights, layer, grid)
    return pl.pallas_call(
        functools.partial(_mix_kernel, n_cast=len(cast_weights), layer=mod_layer),
        grid=grid,
        in_specs=[
            tile,
            pl.BlockSpec((None,) + mod.shape[1:], lambda i, j: (mod_layer, 0, 0)),
            _full(n1w.shape),
            tab, tab,
            _full((d, IN_COLS)),
            pl.BlockSpec(memory_space=pltpu.SMEM),
            _full(ln_w.shape), _full(ln_b.shape),
            pl.BlockSpec((None,) + sgu_w.shape[1:], lambda i, j: (mod_layer, 0, 0, 0)),
            _full((SGU_CHUNK, SGU_WIDTH)),
            _full((SGU_WIDTH, d)), _full((Q_END, d)), _full((d, d)),
        ] + cast_in,
        out_specs=[tile] + cast_out,
        out_shape=[jax.ShapeDtypeStruct(x.shape, x.dtype)] + cast_shapes,
        scratch_shapes=[
            pltpu.VMEM((tm, Q_END), BF16),
            pltpu.VMEM((4, tm + ATTN_BLOCK, LANES), BF16),
            pltpu.VMEM((4, tm + ATTN_BLOCK, LANES), BF16),
            pltpu.VMEM((2, 4 * ATTN_BLOCK, LANES), BF16),
            pltpu.VMEM((tm, Q_END), BF16),
            pltpu.VMEM((tm, SGU_WIDTH), BF16),
        ],
        compiler_params=pltpu.CompilerParams(
            dimension_semantics=("arbitrary", "arbitrary"),
            vmem_limit_bytes=VMEM_LIMIT_BYTES),
        name="token_mixer",
    )(x, mod, n1w, cos_tab, sin_tab, w_in, sinks, ln_w, ln_b, sgu_w, sgu_b_full,
      proj_a, proj_b, w_out, *cast_weights)


CONV_PAD_ROWS = SUBLANES


FFN_N_INPUTS = 9


def _ffn_kernel(*refs, n_cast, layer, final_norm):
    (x_ref, mod_ref, n2w_ref, wg_ref, wu_ref, cw_ref, cb_ref, wd_ref, fnw_ref) = refs[:FFN_N_INPUTS]
    o_ref = refs[FFN_N_INPUTS + n_cast]
    a_s, carry_s, act_s = refs[FFN_N_INPUTS + 2 * n_cast + 1:]
    _cast_slabs(refs[FFN_N_INPUTS:FFN_N_INPUTS + n_cast],
                refs[FFN_N_INPUTS + n_cast + 1:FFN_N_INPUTS + 2 * n_cast + 1])
    tm = FFN_TOKEN_TILE
    j = pl.program_id(1)

    @pl.when(j == 0)
    def _():
        carry_s[...] = jnp.zeros((CONV_PAD_ROWS, FFN_DIM), F32)

    norm_scale = n2w_ref[layer:layer + 1, :] * (1.0 + _mod_row(mod_ref, 4))
    shift2 = _mod_row(mod_ref, 3)
    hb_chunks, a_chunks, up_chunks = [], [], []
    for r in range(tm // FFN_NORM_ROW_CHUNK):
        xr = x_ref[0, r * FFN_NORM_ROW_CHUNK:(r + 1) * FFN_NORM_ROW_CHUNK, :]
        hr = (_rms_norm(xr, norm_scale) + shift2).astype(BF16)
        hb_chunks.append(hr)
        a_chunks.append(_dot(hr, wg_ref[:, 0:MXU_COLS]))
        up_chunks.append(_dot(hr, wu_ref[:, 0:MXU_COLS]))
    hb = jnp.concatenate(hb_chunks, axis=0)
    first_a = jnp.concatenate(a_chunks, axis=0)
    first_up = jnp.concatenate(up_chunks, axis=0)

    for c in range(FFN_DIM // MXU_COLS):
        cols = slice(c * MXU_COLS, (c + 1) * MXU_COLS)
        a = first_a if c == 0 else _dot(hb, wg_ref[:, cols])
        stage = a_s.at[c % 2]
        stage[0:CONV_PAD_ROWS, :] = carry_s[:, cols]
        stage[CONV_PAD_ROWS:CONV_PAD_ROWS + tm, :] = a
        conv = cb_ref[layer:layer + 1, cols] + cw_ref[2:3, cols] * a
        for tap in range(CONV_WIDTH - 1):
            back = CONV_WIDTH - 1 - tap
            conv = conv + cw_ref[tap:tap + 1, cols] * stage[CONV_PAD_ROWS - back:CONV_PAD_ROWS - back + tm, :]
        carry_s[:, cols] = stage[tm:tm + CONV_PAD_ROWS, :]
        up = first_up if c == 0 else _dot(hb, wu_ref[:, cols])
        act_s[:, cols] = (conv * jax.nn.sigmoid(conv) * up).astype(BF16)

    for r in range(tm // OUT_ROW_CHUNK):
        rows = slice(r * OUT_ROW_CHUNK, (r + 1) * OUT_ROW_CHUNK)
        y = x_ref[0, rows, :] + _mod_row(mod_ref, 5) * _dot(act_s[rows, :], wd_ref[...])
        if final_norm:
            y = _rms_norm(y, fnw_ref[...])
        o_ref[0, rows, :] = y


def _ffn_layer(x, mod, n2w, w_gate, w_up, conv_w, conv_b, w_down, final_w, final_norm,
               mod_layer, cast_weights, layer):
    b, s, d = x.shape
    tm = FFN_TOKEN_TILE
    tile = pl.BlockSpec((1, tm, d), lambda i, j: (i, j, 0))
    grid = (b, s // tm)
    cast_in, cast_out, cast_shapes = _cast_specs(cast_weights, layer, grid)
    return pl.pallas_call(
        functools.partial(_ffn_kernel, n_cast=len(cast_weights), layer=mod_layer,
                          final_norm=final_norm),
        grid=grid,
        in_specs=[
            tile,
            pl.BlockSpec((None,) + mod.shape[1:], lambda i, j: (mod_layer, 0, 0)),
            _full(n2w.shape),
            _full((d, FFN_DIM)), _full((d, FFN_DIM)),
            pl.BlockSpec((None,) + conv_w.shape[1:], lambda i, j: (mod_layer, 0, 0)),
            _full(conv_b.shape),
            _full((FFN_DIM, d)),
            _full((1, d)),
        ] + cast_in,
        out_specs=[tile] + cast_out,
        out_shape=[jax.ShapeDtypeStruct(x.shape, x.dtype)] + cast_shapes,
        scratch_shapes=[
            pltpu.VMEM((2, CONV_PAD_ROWS + tm, MXU_COLS), F32),
            pltpu.VMEM((CONV_PAD_ROWS, FFN_DIM), F32),
            pltpu.VMEM((tm, FFN_DIM), BF16),
        ],
        compiler_params=pltpu.CompilerParams(
            dimension_semantics=("arbitrary", "arbitrary"),
            vmem_limit_bytes=VMEM_LIMIT_BYTES),
        name="conv_ffn",
    )(x, mod, n2w, w_gate, w_up, conv_w, conv_b, w_down, final_w, *cast_weights)


def kernel(x, c, positions, ada_w, ada_b, norm1_w, w_in, attn_sinks, sgu_ln_w, sgu_ln_b, sgu_w, sgu_b,
           proj_a, proj_b, w_out, norm2_w, ffn_w_gate, ffn_w_up, ffn_conv_w, ffn_conv_b, ffn_w_down,
           final_norm_w):
    batch, seq, d = x.shape
    depth = ada_w.shape[0]
    assert seq % TOKEN_TILE == 0 and seq % FFN_TOKEN_TILE == 0 and TOKEN_TILE % ATTN_BLOCK == 0
    assert d == D_MODEL

    mix_f32 = (w_in, proj_a, proj_b, w_out)
    ffn_f32 = (ffn_w_gate, ffn_w_up, ffn_w_down)
    rows = -(-batch // SUBLANES) * SUBLANES
    c_pad = jnp.pad(c, ((0, rows - batch), (0, 0)))
    mod, cos_tab, sin_tab, *mix_w = _prepare(c_pad, ada_w, ada_b, positions, mix_f32)

    for l in range(depth):
        sgu_b_full = jnp.repeat(sgu_b[l].T, SGU_WIDTH // SGU_GROUPS, axis=1)
        x, *ffn_w = _mix_layer(
            x, mod, norm1_w, cos_tab, sin_tab, mix_w[0], attn_sinks, sgu_ln_w, sgu_ln_b,
            sgu_w, sgu_b_full, mix_w[1], mix_w[2], mix_w[3], l, ffn_f32, l)
        last = l == depth - 1
        x, *mix_w = _ffn_layer(
            x, mod, norm2_w, ffn_w[0], ffn_w[1], ffn_conv_w, ffn_conv_b, ffn_w[2],
            final_norm_w.reshape(1, d), last, l, () if last else mix_f32, l + 1)
    return x
```

```python
import functools

import jax
import jax.numpy as jnp
from jax import lax
from jax.experimental import pallas as pl
from jax.experimental.pallas import tpu as pltpu

D_MODEL = 1024
N_Q_HEADS = 16
N_KV_HEADS = 2
HEAD_DIM = 64
Q_PER_KV = N_Q_HEADS // N_KV_HEADS
ATTN_BLOCK = 128
ROPE_THETA = 500000.0
ROT_DIM = HEAD_DIM // 4
SGU_WIDTH = 1024
SGU_GROUPS = 8
SGU_CHUNK = 128
FFN_DIM = 2816
CONV_WIDTH = 3
NORM_EPS = 1e-6

Q_END = N_Q_HEADS * HEAD_DIM
K_END = Q_END + N_KV_HEADS * HEAD_DIM
V_END = K_END + N_KV_HEADS * HEAD_DIM
Z_END = V_END + 2 * SGU_WIDTH
IN_COLS = Z_END + 2 * D_MODEL

LANES = 128
SUBLANES = 8
MXU_COLS = 256
TOKEN_TILE = 512
FFN_TOKEN_TILE = 1024
NORM_ROW_CHUNK = 128
FFN_NORM_ROW_CHUNK = 256
OUT_ROW_CHUNK = 256
ATTN_LOOKAHEAD = 2
VMEM_LIMIT_BYTES = 56 * 1024 * 1024
MASK_VALUE = -1e30
LOG2E = 1.4426950408889634

F32 = jnp.float32
BF16 = jnp.bfloat16


def _rms_norm(x, w):
    ms = jnp.mean(x * x, axis=-1, keepdims=True)
    return x * lax.rsqrt(ms + NORM_EPS) * w


def _mod_row(mod_ref, part):
    return mod_ref[pl.ds(pl.program_id(0), 1), part * D_MODEL:(part + 1) * D_MODEL]


def _dot(a, b):
    return jnp.dot(a, b, preferred_element_type=F32)


BF16_ROW_TILE = 16


def _cast_specs(stacked_weights, layer, grid):
    steps = grid[0] * grid[1]
    specs, shapes = [], []
    for w in stacked_weights:
        _, rows, cols = w.shape
        slab, share = rows // steps, 1
        while slab % BF16_ROW_TILE:
            slab, share = slab * 2, share * 2
        assert rows % slab == 0 and steps % share == 0
        idx = lambda i, j, share=share: ((i * grid[1] + j) // share, 0)
        specs.append((pl.BlockSpec((None, slab, cols), lambda i, j, idx=idx: (layer,) + idx(i, j)),
                      pl.BlockSpec((slab, cols), idx)))
        shapes.append(jax.ShapeDtypeStruct((rows, cols), BF16))
    return [s[0] for s in specs], [s[1] for s in specs], shapes


def _cast_slabs(src_refs, dst_refs):
    for src, dst in zip(src_refs, dst_refs):
        dst[...] = src[...].astype(BF16)


ROPE_ROW_TILE = 2048


def _split_bf16(v):
    hi = v.astype(BF16)
    return hi, (v - hi.astype(F32)).astype(BF16)


def _prep_kernel(c_ref, adaw_ref, adab_ref, pos_ref, invf_ref, *refs):
    n_cast = (len(refs) - 3) // 2
    mod_ref, cos_ref, sin_ref = refs[n_cast:n_cast + 3]
    _cast_slabs(refs[:n_cast], refs[n_cast + 3:])

    c = c_ref[...]
    rows = c.shape[0]
    c_hi, c_lo = _split_bf16(c * jax.nn.sigmoid(c))
    lhs = jnp.concatenate([c_hi, c_lo], axis=0)
    w_hi, w_lo = _split_bf16(adaw_ref[...])
    parts = _dot(lhs, w_hi) + _dot(lhs, w_lo)
    mod_ref[...] = parts[0:rows] + parts[rows:2 * rows] + adab_ref[...]

    nfreq = ROT_DIM // 2
    freq_row = lax.broadcasted_iota(jnp.int32, (nfreq, LANES), 0)
    inv_freq = jnp.zeros((nfreq, LANES), F32)
    for f in range(nfreq):
        inv_freq = jnp.where(freq_row == f, invf_ref[f], inv_freq)
    part_freq = lax.broadcasted_iota(jnp.int32, (2 * nfreq, LANES), 0) & (nfreq - 1)
    part_lane = lax.broadcasted_iota(jnp.int32, (2 * nfreq, LANES), 1) & (HEAD_DIM - 1)
    uses = (part_lane < ROT_DIM) & ((part_lane & (nfreq - 1)) == part_freq)
    cos_pattern = jnp.where(uses, 1.0, 0.0).astype(BF16)
    sin_pattern = jnp.where(uses, jnp.where(part_lane < nfreq, -1.0, 1.0), 0.0).astype(BF16)
    head_lane = lax.broadcasted_iota(jnp.int32, (LANES, LANES), 1) & (HEAD_DIM - 1)
    contract_first = (((0,), (0,)), ((), ()))

    def expand(values, pattern):
        hi = values.astype(BF16).astype(F32)
        parts = jnp.concatenate([hi, values - hi], axis=0).astype(BF16)
        return lax.dot_general(parts, pattern, contract_first, preferred_element_type=F32)

    for r in range(pos_ref.shape[1]):
        ang = inv_freq * pos_ref[0, r:r + 1, :].astype(F32)
        rows_r = slice(r * LANES, (r + 1) * LANES)
        cos_ref[0, rows_r, :] = jnp.where(head_lane < ROT_DIM, expand(jnp.cos(ang), cos_pattern), 1.0)
        sin_ref[0, rows_r, :] = expand(jnp.sin(ang), sin_pattern)


def _prepare(c, ada_w, ada_b, positions, cast_weights):
    b, s = positions.shape
    depth, d, n = ada_w.shape
    rows = c.shape[0]
    inv_freq = ROPE_THETA ** (-jnp.arange(0, ROT_DIM, 2, dtype=F32) / ROT_DIM)
    tab = jax.ShapeDtypeStruct((b, s, LANES), F32)
    grid = (b, s // ROPE_ROW_TILE)
    steps = grid[0] * grid[1]
    assert steps % depth == 0 and n % (steps // depth) == 0
    col_tiles = steps // depth
    ada_cols = n // col_tiles
    ada_idx = lambda i, j: ((i * grid[1] + j) // col_tiles, 0, (i * grid[1] + j) % col_tiles)
    cast_in, cast_out, cast_shapes = _cast_specs(cast_weights, 0, grid)
    return pl.pallas_call(
        _prep_kernel,
        grid=grid,
        in_specs=[
            pl.BlockSpec((rows, d), lambda i, j: (0, 0)),
            pl.BlockSpec((None, d, ada_cols), ada_idx),
            pl.BlockSpec((None, 1, ada_cols), ada_idx),
            pl.BlockSpec((1, ROPE_ROW_TILE // LANES, LANES), lambda i, j: (i, j, 0)),
            pl.BlockSpec(memory_space=pltpu.SMEM),
        ] + cast_in,
        out_specs=[pl.BlockSpec((None, rows, ada_cols), ada_idx)]
        + [pl.BlockSpec((1, ROPE_ROW_TILE, LANES), lambda i, j: (i, j, 0))] * 2 + cast_out,
        out_shape=[jax.ShapeDtypeStruct((depth, rows, n), F32), tab, tab] + cast_shapes,
        compiler_params=pltpu.CompilerParams(
            dimension_semantics=("arbitrary", "arbitrary"),
            vmem_limit_bytes=VMEM_LIMIT_BYTES),
        name="prepare",
    )(c, ada_w, ada_b.reshape(depth, 1, n), positions.reshape(b, s // LANES, LANES), inv_freq,
      *cast_weights)


MIX_N_INPUTS = 14


def _mix_kernel(*refs, n_cast, layer):
    (x_ref, mod_ref, n1w_ref, cos_ref, sin_ref, win_ref, sink_ref, lnw_ref, lnb_ref,
     sguw_ref, sgub_ref, pa_ref, pb_ref, wo_ref) = refs[:MIX_N_INPUTS]
    o_ref = refs[MIX_N_INPUTS + n_cast]
    q_s, k4_s, v4_s, bias_s, yattn_s, ysgu_s = refs[MIX_N_INPUTS + 2 * n_cast + 1:]
    _cast_slabs(refs[MIX_N_INPUTS:MIX_N_INPUTS + n_cast],
                refs[MIX_N_INPUTS + n_cast + 1:MIX_N_INPUTS + 2 * n_cast + 1])
    tm = TOKEN_TILE
    nblk = tm // ATTN_BLOCK
    band_keys = 2 * ATTN_BLOCK
    j = pl.program_id(1)

    @pl.when(j == 0)
    def _():
        k4_s[:, 0:ATTN_BLOCK, :] = jnp.zeros((4, ATTN_BLOCK, LANES), BF16)
        v4_s[:, 0:ATTN_BLOCK, :] = jnp.zeros((4, ATTN_BLOCK, LANES), BF16)

    norm_scale = n1w_ref[layer:layer + 1, :] * (1.0 + _mod_row(mod_ref, 1))
    shift1 = _mod_row(mod_ref, 0)
    hb_chunks, q_chunks = [], []
    for r in range(tm // NORM_ROW_CHUNK):
        xr = x_ref[0, r * NORM_ROW_CHUNK:(r + 1) * NORM_ROW_CHUNK, :]
        hr = (_rms_norm(xr, norm_scale) + shift1).astype(BF16)
        hb_chunks.append(hr)
        q_chunks.append(_dot(hr, win_ref[:, 0:Q_END]))
    hb = jnp.concatenate(hb_chunks, axis=0)
    q = jnp.concatenate(q_chunks, axis=0)
    kv = _dot(hb, win_ref[:, Q_END:V_END])
    z_v = _dot(hb, win_ref[:, V_END + SGU_WIDTH:Z_END])
    gates_a = _dot(hb, win_ref[:, Z_END:Z_END + D_MODEL])
    z_u = _dot(hb, win_ref[:, V_END:V_END + SGU_WIDTH])

    cos_t = cos_ref[0]
    sin_t = sin_ref[0]
    lane = lax.broadcasted_iota(jnp.int32, (tm, LANES), 1)
    first_half = (lane & (HEAD_DIM - 1)) < (ROT_DIM // 2)
    left_head = lane < HEAD_DIM

    def rope(t):
        partner = jnp.where(first_half,
                            pltpu.roll(t, LANES - ROT_DIM // 2, 1),
                            pltpu.roll(t, ROT_DIM // 2, 1))
        return t * cos_t + partner * sin_t

    for c in range(Q_END // LANES):
        qc = rope(q[:, c * LANES:(c + 1) * LANES]) * (HEAD_DIM ** -0.5 * LOG2E)
        q_s[:, c * LANES:(c + 1) * LANES] = qc.astype(BF16)

    k = rope(kv[:, 0:LANES])
    v = kv[:, LANES:2 * LANES]
    for src, dst in ((k, k4_s), (v, v4_s)):
        swapped = pltpu.roll(src, HEAD_DIM, 1)
        zero = jnp.zeros_like(src)
        rows = slice(ATTN_BLOCK, ATTN_BLOCK + tm)
        dst[0, rows, :] = jnp.where(left_head, src, zero).astype(BF16)
        dst[1, rows, :] = jnp.where(left_head, zero, swapped).astype(BF16)
        dst[2, rows, :] = jnp.where(left_head, swapped, zero).astype(BF16)
        dst[3, rows, :] = jnp.where(left_head, zero, src).astype(BF16)

    def gelu(t):
        return 0.5 * t * (1.0 + lax.erf(t * (2.0 ** -0.5)))

    v_sgu = gelu(z_v)
    u = gelu(z_u)
    mu = jnp.mean(v_sgu, axis=-1, keepdims=True)
    dv = v_sgu - mu
    var = jnp.mean(dv * dv, axis=-1, keepdims=True)
    vn = (dv * lax.rsqrt(var + NORM_EPS) * lnw_ref[layer:layer + 1, :]
          + lnb_ref[layer:layer + 1, :]).astype(BF16)
    ti = lax.broadcasted_iota(jnp.int32, (SGU_CHUNK, SGU_CHUNK), 0)
    si = lax.broadcasted_iota(jnp.int32, (SGU_CHUNK, SGU_CHUNK), 1)
    causal = si <= ti
    gdim = SGU_WIDTH // SGU_GROUPS
    nchunk = tm // SGU_CHUNK
    for g in range(SGU_GROUPS):
        cols = slice(g * gdim, (g + 1) * gdim)
        wm = jnp.where(causal, sguw_ref[g], 0.0).astype(BF16)
        rhs = jnp.concatenate(
            [vn[c * SGU_CHUNK:(c + 1) * SGU_CHUNK, cols] for c in range(nchunk)], axis=1)
        f = _dot(wm, rhs)
        for c in range(nchunk):
            rows = slice(c * SGU_CHUNK, (c + 1) * SGU_CHUNK)
            fc = f[:, c * gdim:(c + 1) * gdim] + sgub_ref[:, cols]
            ysgu_s[rows, cols] = (u[rows, cols] * fc).astype(BF16)

    sgu_part = jax.nn.sigmoid(gates_a) * _dot(ysgu_s[...], pa_ref[...])

    er = lax.broadcasted_iota(jnp.int32, (ATTN_BLOCK, LANES), 0)
    ec = lax.broadcasted_iota(jnp.int32, (ATTN_BLOCK, LANES), 1)
    eye = jnp.where(er == ec, 1.0, 0.0).astype(BF16)
    kr = lax.broadcasted_iota(jnp.int32, (band_keys, ATTN_BLOCK), 0)
    qc_i = lax.broadcasted_iota(jnp.int32, (band_keys, ATTN_BLOCK), 1)
    band_t = (kr > qc_i) & (kr <= qc_i + ATTN_BLOCK)
    for idx, ok in ((0, band_t & (kr >= ATTN_BLOCK)), (1, band_t)):
        bias_t = jnp.where(ok, 0.0, MASK_VALUE).astype(BF16)
        bias_s[idx, 0:band_keys, :] = bias_t
        bias_s[idx, band_keys:2 * band_keys, :] = bias_t
    sr = lax.broadcasted_iota(jnp.int32, (2 * band_keys, LANES), 0)
    sc = lax.broadcasted_iota(jnp.int32, (2 * band_keys, LANES), 1)
    ones_sel = jnp.where((sr < band_keys) == (sc < HEAD_DIM), 1.0, 0.0).astype(BF16)
    sink_value_row = (sr == 0) | (sr == band_keys)
    sink_lane = ec == 0
    contract_last = (((1,), (1,)), ((), ()))

    def score_matmul(b, g):
        r0 = b * ATTN_BLOCK
        bias_t = bias_s[jnp.where(j > 0, 1, 0)] if b == 0 else bias_s[1]
        kbd = jnp.concatenate([k4_s[2 * g, pl.ds(r0, band_keys), :],
                               k4_s[2 * g + 1, pl.ds(r0, band_keys), :]], axis=0)
        kbd = jnp.concatenate([kbd, bias_t], axis=1)
        qrows = []
        for pr in range(Q_PER_KV // 2):
            col = (g * Q_PER_KV + 2 * pr) * HEAD_DIM
            qrows.append(jnp.concatenate([q_s[pl.ds(r0, ATTN_BLOCK), col:col + LANES], eye], axis=1))
        return lax.dot_general(jnp.concatenate(qrows, axis=0), kbd, contract_last,
                               preferred_element_type=F32)

    def softmax_value_matmul(b, g, scores):
        r0 = b * ATTN_BLOCK
        vbd = jnp.concatenate([v4_s[2 * g, pl.ds(r0, band_keys), :],
                               v4_s[2 * g + 1, pl.ds(r0, band_keys), :]], axis=0)
        vbd = jnp.where(sink_value_row, jnp.zeros_like(vbd), vbd)
        value_op = jnp.concatenate([vbd, ones_sel], axis=1)
        probs = []
        for pr in range(Q_PER_KV // 2):
            rows = slice(pr * ATTN_BLOCK, (pr + 1) * ATTN_BLOCK)
            pair_probs = []
            for hh in range(2):
                c0 = hh * band_keys
                sink = sink_ref[layer, g * Q_PER_KV + 2 * pr + hh] * LOG2E
                first = jnp.where(sink_lane, sink, scores[rows, c0:c0 + LANES])
                rest = scores[rows, c0 + LANES:c0 + band_keys]
                m = jnp.max(jnp.maximum(first, rest), axis=-1, keepdims=True)
                pair_probs += [jnp.exp2(first - m).astype(BF16), jnp.exp2(rest - m).astype(BF16)]
            probs.append(jnp.concatenate(pair_probs, axis=1))
        o = _dot(jnp.concatenate(probs, axis=0), value_op)
        for pr in range(Q_PER_KV // 2):
            rows = slice(pr * ATTN_BLOCK, (pr + 1) * ATTN_BLOCK)
            col = (g * Q_PER_KV + 2 * pr) * HEAD_DIM
            yattn_s[pl.ds(r0, ATTN_BLOCK), col:col + LANES] = (
                o[rows, 0:LANES] / o[rows, LANES:]).astype(BF16)

    units = [(b, g) for b in range(nblk) for g in range(N_KV_HEADS)]
    pending = [score_matmul(*u) for u in units[:ATTN_LOOKAHEAD]]
    for i, (b, g) in enumerate(units):
        if i + ATTN_LOOKAHEAD < len(units):
            pending.append(score_matmul(*units[i + ATTN_LOOKAHEAD]))
        softmax_value_matmul(b, g, pending.pop(0))

    k4_s[:, 0:ATTN_BLOCK, :] = k4_s[:, tm:tm + ATTN_BLOCK, :]
    v4_s[:, 0:ATTN_BLOCK, :] = v4_s[:, tm:tm + ATTN_BLOCK, :]

    gate_b = jax.nn.sigmoid(_dot(hb, win_ref[:, Z_END + D_MODEL:IN_COLS]))
    merged = (sgu_part + gate_b * _dot(yattn_s[...], pb_ref[...])).astype(BF16)
    for r in range(tm // OUT_ROW_CHUNK):
        rows = slice(r * OUT_ROW_CHUNK, (r + 1) * OUT_ROW_CHUNK)
        o_ref[0, rows, :] = x_ref[0, rows, :] + _mod_row(mod_ref, 2) * _dot(merged[rows, :], wo_ref[...])


def _full(shape):
    zeros = (0,) * len(shape)
    return pl.BlockSpec(shape, lambda i, j: zeros)


def _mix_layer(x, mod, n1w, cos_tab, sin_tab, w_in, sinks, ln_w, ln_b, sgu_w, sgu_b_full,
               proj_a, proj_b, w_out, mod_layer, cast_weights, layer):
    b, s, d = x.shape
    tm = TOKEN_TILE
    tile = pl.BlockSpec((1, tm, d), lambda i, j: (i, j, 0))
    tab = pl.BlockSpec((1, tm, LANES), lambda i, j: (i, j, 0))
    grid = (b, s // tm)
    cast_in, cast_out, cast_shapes = _cast_specs(cast_weights, layer, grid)
    return pl.pallas_call(
        functools.partial(_mix_kernel, n_cast=len(cast_weights), layer=mod_layer),
        grid=grid,
        in_specs=[
            tile,
            pl.BlockSpec((None,) + mod.shape[1:], lambda i, j: (mod_layer, 0, 0)),
            _full(n1w.shape),
            tab, tab,
            _full((d, IN_COLS)),
            pl.BlockSpec(memory_space=pltpu.SMEM),
            _full(ln_w.shape), _full(ln_b.shape),
            pl.BlockSpec((None,) + sgu_w.shape[1:], lambda i, j: (mod_layer, 0, 0, 0)),
            _full((SGU_CHUNK, SGU_WIDTH)),
            _full((SGU_WIDTH, d)), _full((Q_END, d)), _full((d, d)),
        ] + cast_in,
        out_specs=[tile] + cast_out,
        out_shape=[jax.ShapeDtypeStruct(x.shape, x.dtype)] + cast_shapes,
        scratch_shapes=[
            pltpu.VMEM((tm, Q_END), BF16),
            pltpu.VMEM((4, tm + ATTN_BLOCK, LANES), BF16),
            pltpu.VMEM((4, tm + ATTN_BLOCK, LANES), BF16),
            pltpu.VMEM((2, 4 * ATTN_BLOCK, LANES), BF16),
            pltpu.VMEM((tm, Q_END), BF16),
            pltpu.VMEM((tm, SGU_WIDTH), BF16),
        ],
        compiler_params=pltpu.CompilerParams(
            dimension_semantics=("arbitrary", "arbitrary"),
            vmem_limit_bytes=VMEM_LIMIT_BYTES),
        name="token_mixer",
    )(x, mod, n1w, cos_tab, sin_tab, w_in, sinks, ln_w, ln_b, sgu_w, sgu_b_full,
      proj_a, proj_b, w_out, *cast_weights)


CONV_PAD_ROWS = SUBLANES


FFN_N_INPUTS = 9


def _ffn_kernel(*refs, n_cast, layer, final_norm):
    (x_ref, mod_ref, n2w_ref, wg_ref, wu_ref, cw_ref, cb_ref, wd_ref, fnw_ref) = refs[:FFN_N_INPUTS]
    o_ref = refs[FFN_N_INPUTS + n_cast]
    a_s, carry_s, act_s = refs[FFN_N_INPUTS + 2 * n_cast + 1:]
    _cast_slabs(refs[FFN_N_INPUTS:FFN_N_INPUTS + n_cast],
                refs[FFN_N_INPUTS + n_cast + 1:FFN_N_INPUTS + 2 * n_cast + 1])
    tm = FFN_TOKEN_TILE
    j = pl.program_id(1)

    @pl.when(j == 0)
    def _():
        carry_s[...] = jnp.zeros((CONV_PAD_ROWS, FFN_DIM), F32)

    norm_scale = n2w_ref[layer:layer + 1, :] * (1.0 + _mod_row(mod_ref, 4))
    shift2 = _mod_row(mod_ref, 3)
    hb_chunks, a_chunks, up_chunks = [], [], []
    for r in range(tm // FFN_NORM_ROW_CHUNK):
        xr = x_ref[0, r * FFN_NORM_ROW_CHUNK:(r + 1) * FFN_NORM_ROW_CHUNK, :]
        hr = (_rms_norm(xr, norm_scale) + shift2).astype(BF16)
        hb_chunks.append(hr)
        a_chunks.append(_dot(hr, wg_ref[:, 0:MXU_COLS]))
        up_chunks.append(_dot(hr, wu_ref[:, 0:MXU_COLS]))
    hb = jnp.concatenate(hb_chunks, axis=0)
    first_a = jnp.concatenate(a_chunks, axis=0)
    first_up = jnp.concatenate(up_chunks, axis=0)

    for c in range(FFN_DIM // MXU_COLS):
        cols = slice(c * MXU_COLS, (c + 1) * MXU_COLS)
        a = first_a if c == 0 else _dot(hb, wg_ref[:, cols])
        stage = a_s.at[c % 2]
        stage[0:CONV_PAD_ROWS, :] = carry_s[:, cols]
        stage[CONV_PAD_ROWS:CONV_PAD_ROWS + tm, :] = a
        conv = cb_ref[layer:layer + 1, cols] + cw_ref[2:3, cols] * a
        for tap in range(CONV_WIDTH - 1):
            back = CONV_WIDTH - 1 - tap
            conv = conv + cw_ref[tap:tap + 1, cols] * stage[CONV_PAD_ROWS - back:CONV_PAD_ROWS - back + tm, :]
        carry_s[:, cols] = stage[tm:tm + CONV_PAD_ROWS, :]
        up = first_up if c == 0 else _dot(hb, wu_ref[:, cols])
        act_s[:, cols] = (conv * jax.nn.sigmoid(conv) * up).astype(BF16)

    for r in range(tm // OUT_ROW_CHUNK):
        rows = slice(r * OUT_ROW_CHUNK, (r + 1) * OUT_ROW_CHUNK)
        y = x_ref[0, rows, :] + _mod_row(mod_ref, 5) * _dot(act_s[rows, :], wd_ref[...])
        if final_norm:
            y = _rms_norm(y, fnw_ref[...])
        o_ref[0, rows, :] = y


def _ffn_layer(x, mod, n2w, w_gate, w_up, conv_w, conv_b, w_down, final_w, final_norm,
               mod_layer, cast_weights, layer):
    b, s, d = x.shape
    tm = FFN_TOKEN_TILE
    tile = pl.BlockSpec((1, tm, d), lambda i, j: (i, j, 0))
    grid = (b, s // tm)
    cast_in, cast_out, cast_shapes = _cast_specs(cast_weights, layer, grid)
    return pl.pallas_call(
        functools.partial(_ffn_kernel, n_cast=len(cast_weights), layer=mod_layer,
                          final_norm=final_norm),
        grid=grid,
        in_specs=[
            tile,
            pl.BlockSpec((None,) + mod.shape[1:], lambda i, j: (mod_layer, 0, 0)),
            _full(n2w.shape),
            _full((d, FFN_DIM)), _full((d, FFN_DIM)),
            pl.BlockSpec((None,) + conv_w.shape[1:], lambda i, j: (mod_layer, 0, 0)),
            _full(conv_b.shape),
            _full((FFN_DIM, d)),
            _full((1, d)),
        ] + cast_in,
        out_specs=[tile] + cast_out,
        out_shape=[jax.ShapeDtypeStruct(x.shape, x.dtype)] + cast_shapes,
        scratch_shapes=[
            pltpu.VMEM((2, CONV_PAD_ROWS + tm, MXU_COLS), F32),
            pltpu.VMEM((CONV_PAD_ROWS, FFN_DIM), F32),
            pltpu.VMEM((tm, FFN_DIM), BF16),
        ],
        compiler_params=pltpu.CompilerParams(
            dimension_semantics=("arbitrary", "arbitrary"),
            vmem_limit_bytes=VMEM_LIMIT_BYTES),
        name="conv_ffn",
    )(x, mod, n2w, w_gate, w_up, conv_w, conv_b, w_down, final_w, *cast_weights)


def kernel(x, c, positions, ada_w, ada_b, norm1_w, w_in, attn_sinks, sgu_ln_w, sgu_ln_b, sgu_w, sgu_b,
           proj_a, proj_b, w_out, norm2_w, ffn_w_gate, ffn_w_up, ffn_conv_w, ffn_conv_b, ffn_w_down,
           final_norm_w):
    batch, seq, d = x.shape
    depth = ada_w.shape[0]
    assert seq % TOKEN_TILE == 0 and seq % FFN_TOKEN_TILE == 0 and TOKEN_TILE % ATTN_BLOCK == 0
    assert d == D_MODEL

    mix_f32 = (w_in, proj_a, proj_b, w_out)
    ffn_f32 = (ffn_w_gate, ffn_w_up, ffn_w_down)
    rows = -(-batch // SUBLANES) * SUBLANES
    c_pad = jnp.pad(c, ((0, rows - batch), (0, 0)))
    mod, cos_tab, sin_tab, *mix_w = _prepare(c_pad, ada_w, ada_b, positions, mix_f32)

    for l in range(depth):
        sgu_b_full = jnp.repeat(sgu_b[l].T, SGU_WIDTH // SGU_GROUPS, axis=1)
        x, *ffn_w = _mix_layer(
            x, mod, norm1_w, cos_tab, sin_tab, mix_w[0], attn_sinks, sgu_ln_w, sgu_ln_b,
            sgu_w, sgu_b_full, mix_w[1], mix_w[2], mix_w[3], l, ffn_f32, l)
        last = l == depth - 1
        x, *mix_w = _ffn_layer(
            x, mod, norm2_w, ffn_w[0], ffn_w[1], ffn_conv_w, ffn_conv_b, ffn_w[2],
            final_norm_w.reshape(1, d), last, l, () if last else mix_f32, l + 1)
    return x
```

```python
import functools

import jax
import jax.numpy as jnp
from jax import lax
from jax.experimental import pallas as pl
from jax.experimental.pallas import tpu as pltpu

D_MODEL = 1024
N_Q_HEADS = 16
N_KV_HEADS = 2
HEAD_DIM = 64
Q_PER_KV = N_Q_HEADS // N_KV_HEADS
ATTN_BLOCK = 128
ROPE_THETA = 500000.0
ROT_DIM = HEAD_DIM // 4
SGU_WIDTH = 1024
SGU_GROUPS = 8
SGU_CHUNK = 128
FFN_DIM = 2816
CONV_WIDTH = 3
NORM_EPS = 1e-6

Q_END = N_Q_HEADS * HEAD_DIM
K_END = Q_END + N_KV_HEADS * HEAD_DIM
V_END = K_END + N_KV_HEADS * HEAD_DIM
Z_END = V_END + 2 * SGU_WIDTH
IN_COLS = Z_END + 2 * D_MODEL

LANES = 128
SUBLANES = 8
MXU_COLS = 256
TOKEN_TILE = 512
FFN_TOKEN_TILE = 1024
NORM_ROW_CHUNK = 128
FFN_NORM_ROW_CHUNK = 256
OUT_ROW_CHUNK = 256
ATTN_LOOKAHEAD = 2
VMEM_LIMIT_BYTES = 56 * 1024 * 1024
MASK_VALUE = -1e30
LOG2E = 1.4426950408889634

F32 = jnp.float32
BF16 = jnp.bfloat16


def _rms_norm(x, w):
    ms = jnp.mean(x * x, axis=-1, keepdims=True)
    return x * lax.rsqrt(ms + NORM_EPS) * w


def _mod_row(mod_ref, part):
    return mod_ref[pl.ds(pl.program_id(0), 1), part * D_MODEL:(part + 1) * D_MODEL]


def _dot(a, b):
    return jnp.dot(a, b, preferred_element_type=F32)


BF16_ROW_TILE = 16


def _cast_specs(stacked_weights, layer, grid):
    steps = grid[0] * grid[1]
    specs, shapes = [], []
    for w in stacked_weights:
        _, rows, cols = w.shape
        slab, share = rows // steps, 1
        while slab % BF16_ROW_TILE:
            slab, share = slab * 2, share * 2
        assert rows % slab == 0 and steps % share == 0
        idx = lambda i, j, share=share: ((i * grid[1] + j) // share, 0)
        specs.append((pl.BlockSpec((None, slab, cols), lambda i, j, idx=idx: (layer,) + idx(i, j)),
                      pl.BlockSpec((slab, cols), idx)))
        shapes.append(jax.ShapeDtypeStruct((rows, cols), BF16))
    return [s[0] for s in specs], [s[1] for s in specs], shapes


def _cast_slabs(src_refs, dst_refs):
    for src, dst in zip(src_refs, dst_refs):
        dst[...] = src[...].astype(BF16)


ROPE_ROW_TILE = 2048


def _split_bf16(v):
    hi = v.astype(BF16)
    return hi, (v - hi.astype(F32)).astype(BF16)


def _prep_kernel(c_ref, adaw_ref, adab_ref, pos_ref, invf_ref, *refs):
    n_cast = (len(refs) - 3) // 2
    mod_ref, cos_ref, sin_ref = refs[n_cast:n_cast + 3]
    _cast_slabs(refs[:n_cast], refs[n_cast + 3:])

    c = c_ref[...]
    rows = c.shape[0]
    c_hi, c_lo = _split_bf16(c * jax.nn.sigmoid(c))
    lhs = jnp.concatenate([c_hi, c_lo], axis=0)
    w_hi, w_lo = _split_bf16(adaw_ref[...])
    parts = _dot(lhs, w_hi) + _dot(lhs, w_lo)
    mod_ref[...] = parts[0:rows] + parts[rows:2 * rows] + adab_ref[...]

    nfreq = ROT_DIM // 2
    freq_row = lax.broadcasted_iota(jnp.int32, (nfreq, LANES), 0)
    inv_freq = jnp.zeros((nfreq, LANES), F32)
    for f in range(nfreq):
        inv_freq = jnp.where(freq_row == f, invf_ref[f], inv_freq)
    part_freq = lax.broadcasted_iota(jnp.int32, (2 * nfreq, LANES), 0) & (nfreq - 1)
    part_lane = lax.broadcasted_iota(jnp.int32, (2 * nfreq, LANES), 1) & (HEAD_DIM - 1)
    uses = (part_lane < ROT_DIM) & ((part_lane & (nfreq - 1)) == part_freq)
    cos_pattern = jnp.where(uses, 1.0, 0.0).astype(BF16)
    sin_pattern = jnp.where(uses, jnp.where(part_lane < nfreq, -1.0, 1.0), 0.0).astype(BF16)
    head_lane = lax.broadcasted_iota(jnp.int32, (LANES, LANES), 1) & (HEAD_DIM - 1)
    contract_first = (((0,), (0,)), ((), ()))

    def expand(values, pattern):
        hi = values.astype(BF16).astype(F32)
        parts = jnp.concatenate([hi, values - hi], axis=0).astype(BF16)
        return lax.dot_general(parts, pattern, contract_first, preferred_element_type=F32)

    for r in range(pos_ref.shape[1]):
        ang = inv_freq * pos_ref[0, r:r + 1, :].astype(F32)
        rows_r = slice(r * LANES, (r + 1) * LANES)
        cos_ref[0, rows_r, :] = jnp.where(head_lane < ROT_DIM, expand(jnp.cos(ang), cos_pattern), 1.0)
        sin_ref[0, rows_r, :] = expand(jnp.sin(ang), sin_pattern)


def _prepare(c, ada_w, ada_b, positions, cast_weights):
    b, s = positions.shape
    depth, d, n = ada_w.shape
    rows = c.shape[0]
    inv_freq = ROPE_THETA ** (-jnp.arange(0, ROT_DIM, 2, dtype=F32) / ROT_DIM)
    tab = jax.ShapeDtypeStruct((b, s, LANES), F32)
    grid = (b, s // ROPE_ROW_TILE)
    steps = grid[0] * grid[1]
    assert steps % depth == 0 and n % (steps // depth) == 0
    col_tiles = steps // depth
    ada_cols = n // col_tiles
    ada_idx = lambda i, j: ((i * grid[1] + j) // col_tiles, 0, (i * grid[1] + j) % col_tiles)
    cast_in, cast_out, cast_shapes = _cast_specs(cast_weights, 0, grid)
    return pl.pallas_call(
        _prep_kernel,
        grid=grid,
        in_specs=[
            pl.BlockSpec((rows, d), lambda i, j: (0, 0)),
            pl.BlockSpec((None, d, ada_cols), ada_idx),
            pl.BlockSpec((None, 1, ada_cols), ada_idx),
            pl.BlockSpec((1, ROPE_ROW_TILE // LANES, LANES), lambda i, j: (i, j, 0)),
            pl.BlockSpec(memory_space=pltpu.SMEM),
        ] + cast_in,
        out_specs=[pl.BlockSpec((None, rows, ada_cols), ada_idx)]
        + [pl.BlockSpec((1, ROPE_ROW_TILE, LANES), lambda i, j: (i, j, 0))] * 2 + cast_out,
        out_shape=[jax.ShapeDtypeStruct((depth, rows, n), F32), tab, tab] + cast_shapes,
        compiler_params=pltpu.CompilerParams(
            dimension_semantics=("arbitrary", "arbitrary"),
            vmem_limit_bytes=VMEM_LIMIT_BYTES),
        name="prepare",
    )(c, ada_w, ada_b.reshape(depth, 1, n), positions.reshape(b, s // LANES, LANES), inv_freq,
      *cast_weights)


MIX_N_INPUTS = 14


def _mix_kernel(*refs, n_cast, layer):
    (x_ref, mod_ref, n1w_ref, cos_ref, sin_ref, win_ref, sink_ref, lnw_ref, lnb_ref,
     sguw_ref, sgub_ref, pa_ref, pb_ref, wo_ref) = refs[:MIX_N_INPUTS]
    o_ref = refs[MIX_N_INPUTS + n_cast]
    q_s, k4_s, v4_s, bias_s, yattn_s, ysgu_s = refs[MIX_N_INPUTS + 2 * n_cast + 1:]
    _cast_slabs(refs[MIX_N_INPUTS:MIX_N_INPUTS + n_cast],
                refs[MIX_N_INPUTS + n_cast + 1:MIX_N_INPUTS + 2 * n_cast + 1])
    tm = TOKEN_TILE
    nblk = tm // ATTN_BLOCK
    band_keys = 2 * ATTN_BLOCK
    j = pl.program_id(1)

    @pl.when(j == 0)
    def _():
        k4_s[:, 0:ATTN_BLOCK, :] = jnp.zeros((4, ATTN_BLOCK, LANES), BF16)
        v4_s[:, 0:ATTN_BLOCK, :] = jnp.zeros((4, ATTN_BLOCK, LANES), BF16)

    norm_scale = n1w_ref[layer:layer + 1, :] * (1.0 + _mod_row(mod_ref, 1))
    shift1 = _mod_row(mod_ref, 0)
    hb_chunks, q_chunks = [], []
    for r in range(tm // NORM_ROW_CHUNK):
        xr = x_ref[0, r * NORM_ROW_CHUNK:(r + 1) * NORM_ROW_CHUNK, :]
        hr = (_rms_norm(xr, norm_scale) + shift1).astype(BF16)
        hb_chunks.append(hr)
        q_chunks.append(_dot(hr, win_ref[:, 0:Q_END]))
    hb = jnp.concatenate(hb_chunks, axis=0)
    q = jnp.concatenate(q_chunks, axis=0)
    kv = _dot(hb, win_ref[:, Q_END:V_END])
    z_v = _dot(hb, win_ref[:, V_END + SGU_WIDTH:Z_END])
    gates = _dot(hb, win_ref[:, Z_END:IN_COLS])
    z_u = _dot(hb, win_ref[:, V_END:V_END + SGU_WIDTH])

    cos_t = cos_ref[0]
    sin_t = sin_ref[0]
    lane = lax.broadcasted_iota(jnp.int32, (tm, LANES), 1)
    first_half = (lane & (HEAD_DIM - 1)) < (ROT_DIM // 2)
    left_head = lane < HEAD_DIM

    def rope(t):
        partner = jnp.where(first_half,
                            pltpu.roll(t, LANES - ROT_DIM // 2, 1),
                            pltpu.roll(t, ROT_DIM // 2, 1))
        return t * cos_t + partner * sin_t

    for c in range(Q_END // LANES):
        qc = rope(q[:, c * LANES:(c + 1) * LANES]) * (HEAD_DIM ** -0.5 * LOG2E)
        q_s[:, c * LANES:(c + 1) * LANES] = qc.astype(BF16)

    k = rope(kv[:, 0:LANES])
    v = kv[:, LANES:2 * LANES]
    for src, dst in ((k, k4_s), (v, v4_s)):
        swapped = pltpu.roll(src, HEAD_DIM, 1)
        zero = jnp.zeros_like(src)
        rows = slice(ATTN_BLOCK, ATTN_BLOCK + tm)
        dst[0, rows, :] = jnp.where(left_head, src, zero).astype(BF16)
        dst[1, rows, :] = jnp.where(left_head, zero, swapped).astype(BF16)
        dst[2, rows, :] = jnp.where(left_head, swapped, zero).astype(BF16)
        dst[3, rows, :] = jnp.where(left_head, zero, src).astype(BF16)

    def twice_gelu(t):
        return t * (1.0 + lax.erf(t * (2.0 ** -0.5)))

    v_sgu = 0.5 * twice_gelu(z_v)
    u2 = twice_gelu(z_u)
    mu = jnp.mean(v_sgu, axis=-1, keepdims=True)
    dv = v_sgu - mu
    var = jnp.mean(dv * dv, axis=-1, keepdims=True)
    vn = (dv * lax.rsqrt(var + NORM_EPS) * lnw_ref[layer:layer + 1, :]
          + lnb_ref[layer:layer + 1, :]).astype(BF16)
    ti = lax.broadcasted_iota(jnp.int32, (SGU_CHUNK, SGU_CHUNK), 0)
    si = lax.broadcasted_iota(jnp.int32, (SGU_CHUNK, SGU_CHUNK), 1)
    causal = si <= ti
    gdim = SGU_WIDTH // SGU_GROUPS
    nchunk = tm // SGU_CHUNK
    for g in range(SGU_GROUPS):
        cols = slice(g * gdim, (g + 1) * gdim)
        wm = (jnp.where(causal, sguw_ref[g], 0.0) * 0.5).astype(BF16)
        half_bias = 0.5 * sgub_ref[:, cols]
        rhs = jnp.concatenate(
            [vn[c * SGU_CHUNK:(c + 1) * SGU_CHUNK, cols] for c in range(nchunk)], axis=1)
        f = _dot(wm, rhs)
        for c in range(nchunk):
            rows = slice(c * SGU_CHUNK, (c + 1) * SGU_CHUNK)
            fc = f[:, c * gdim:(c + 1) * gdim] + half_bias
            ysgu_s[rows, cols] = (u2[rows, cols] * fc).astype(BF16)

    sgu_part = jax.nn.sigmoid(gates[:, 0:D_MODEL]) * _dot(ysgu_s[...], pa_ref[...])
    gate_b = jax.nn.sigmoid(gates[:, D_MODEL:])

    er = lax.broadcasted_iota(jnp.int32, (ATTN_BLOCK, LANES), 0)
    ec = lax.broadcasted_iota(jnp.int32, (ATTN_BLOCK, LANES), 1)
    eye = jnp.where(er == ec, 1.0, 0.0).astype(BF16)
    kr = lax.broadcasted_iota(jnp.int32, (band_keys, ATTN_BLOCK), 0)
    qc_i = lax.broadcasted_iota(jnp.int32, (band_keys, ATTN_BLOCK), 1)
    band_t = (kr > qc_i) & (kr <= qc_i + ATTN_BLOCK)
    for idx, ok in ((0, band_t & (kr >= ATTN_BLOCK)), (1, band_t)):
        bias_t = jnp.where(ok, 0.0, MASK_VALUE).astype(BF16)
        bias_s[idx, 0:band_keys, :] = bias_t
        bias_s[idx, band_keys:2 * band_keys, :] = bias_t
    sr = lax.broadcasted_iota(jnp.int32, (2 * band_keys, LANES), 0)
    sc = lax.broadcasted_iota(jnp.int32, (2 * band_keys, LANES), 1)
    ones_sel = jnp.where((sr < band_keys) == (sc < HEAD_DIM), 1.0, 0.0).astype(BF16)
    sink_value_row = (sr == 0) | (sr == band_keys)
    sink_lane = ec == 0
    contract_last = (((1,), (1,)), ((), ()))

    def score_matmul(b, g):
        r0 = b * ATTN_BLOCK
        bias_t = bias_s[jnp.where(j > 0, 1, 0)] if b == 0 else bias_s[1]
        kbd = jnp.concatenate([k4_s[2 * g, pl.ds(r0, band_keys), :],
                               k4_s[2 * g + 1, pl.ds(r0, band_keys), :]], axis=0)
        kbd = jnp.concatenate([kbd, bias_t], axis=1)
        qrows = []
        for pr in range(Q_PER_KV // 2):
            col = (g * Q_PER_KV + 2 * pr) * HEAD_DIM
            qrows.append(jnp.concatenate([q_s[pl.ds(r0, ATTN_BLOCK), col:col + LANES], eye], axis=1))
        return lax.dot_general(jnp.concatenate(qrows, axis=0), kbd, contract_last,
                               preferred_element_type=F32)

    def softmax_value_matmul(b, g, scores):
        r0 = b * ATTN_BLOCK
        vbd = jnp.concatenate([v4_s[2 * g, pl.ds(r0, band_keys), :],
                               v4_s[2 * g + 1, pl.ds(r0, band_keys), :]], axis=0)
        vbd = jnp.where(sink_value_row, jnp.zeros_like(vbd), vbd)
        value_op = jnp.concatenate([vbd, ones_sel], axis=1)
        probs = []
        for pr in range(Q_PER_KV // 2):
            rows = slice(pr * ATTN_BLOCK, (pr + 1) * ATTN_BLOCK)
            pair_probs = []
            for hh in range(2):
                c0 = hh * band_keys
                sink = sink_ref[layer, g * Q_PER_KV + 2 * pr + hh] * LOG2E
                first = jnp.where(sink_lane, sink, scores[rows, c0:c0 + LANES])
                rest = scores[rows, c0 + LANES:c0 + band_keys]
                m = jnp.max(jnp.maximum(first, rest), axis=-1, keepdims=True)
                pair_probs += [jnp.exp2(first - m).astype(BF16), jnp.exp2(rest - m).astype(BF16)]
            probs.append(jnp.concatenate(pair_probs, axis=1))
        o = _dot(jnp.concatenate(probs, axis=0), value_op)
        for pr in range(Q_PER_KV // 2):
            rows = slice(pr * ATTN_BLOCK, (pr + 1) * ATTN_BLOCK)
            col = (g * Q_PER_KV + 2 * pr) * HEAD_DIM
            yattn_s[pl.ds(r0, ATTN_BLOCK), col:col + LANES] = (
                o[rows, 0:LANES] / o[rows, LANES:]).astype(BF16)

    units = [(b, g) for b in range(nblk) for g in range(N_KV_HEADS)]
    pending = [score_matmul(*u) for u in units[:ATTN_LOOKAHEAD]]
    for i, (b, g) in enumerate(units):
        if i + ATTN_LOOKAHEAD < len(units):
            pending.append(score_matmul(*units[i + ATTN_LOOKAHEAD]))
        softmax_value_matmul(b, g, pending.pop(0))

    k4_s[:, 0:ATTN_BLOCK, :] = k4_s[:, tm:tm + ATTN_BLOCK, :]
    v4_s[:, 0:ATTN_BLOCK, :] = v4_s[:, tm:tm + ATTN_BLOCK, :]

    merged = (sgu_part + gate_b * _dot(yattn_s[...], pb_ref[...])).astype(BF16)
    for r in range(tm // OUT_ROW_CHUNK):
        rows = slice(r * OUT_ROW_CHUNK, (r + 1) * OUT_ROW_CHUNK)
        o_ref[0, rows, :] = x_ref[0, rows, :] + _mod_row(mod_ref, 2) * _dot(merged[rows, :], wo_ref[...])


def _full(shape):
    zeros = (0,) * len(shape)
    return pl.BlockSpec(shape, lambda i, j: zeros)


def _mix_layer(x, mod, n1w, cos_tab, sin_tab, w_in, sinks, ln_w, ln_b, sgu_w, sgu_b_full,
               proj_a, proj_b, w_out, mod_layer, cast_weights, layer):
    b, s, d = x.shape
    tm = TOKEN_TILE
    tile = pl.BlockSpec((1, tm, d), lambda i, j: (i, j, 0))
    tab = pl.BlockSpec((1, tm, LANES), lambda i, j: (i, j, 0))
    grid = (b, s // tm)
    cast_in, cast_out, cast_shapes = _cast_specs(cast_weights, layer, grid)
    return pl.pallas_call(
        functools.partial(_mix_kernel, n_cast=len(cast_weights), layer=mod_layer),
        grid=grid,
        in_specs=[
            tile,
            pl.BlockSpec((None,) + mod.shape[1:], lambda i, j: (mod_layer, 0, 0)),
            _full(n1w.shape),
            tab, tab,
            _full((d, IN_COLS)),
            pl.BlockSpec(memory_space=pltpu.SMEM),
            _full(ln_w.shape), _full(ln_b.shape),
            pl.BlockSpec((None,) + sgu_w.shape[1:], lambda i, j: (mod_layer, 0, 0, 0)),
            _full((SGU_CHUNK, SGU_WIDTH)),
            _full((SGU_WIDTH, d)), _full((Q_END, d)), _full((d, d)),
        ] + cast_in,
        out_specs=[tile] + cast_out,
        out_shape=[jax.ShapeDtypeStruct(x.shape, x.dtype)] + cast_shapes,
        scratch_shapes=[
            pltpu.VMEM((tm, Q_END), BF16),
            pltpu.VMEM((4, tm + ATTN_BLOCK, LANES), BF16),
            pltpu.VMEM((4, tm + ATTN_BLOCK, LANES), BF16),
            pltpu.VMEM((2, 4 * ATTN_BLOCK, LANES), BF16),
            pltpu.VMEM((tm, Q_END), BF16),
            pltpu.VMEM((tm, SGU_WIDTH), BF16),
        ],
        compiler_params=pltpu.CompilerParams(
            dimension_semantics=("arbitrary", "arbitrary"),
            vmem_limit_bytes=VMEM_LIMIT_BYTES),
        name="token_mixer",
    )(x, mod, n1w, cos_tab, sin_tab, w_in, sinks, ln_w, ln_b, sgu_w, sgu_b_full,
      proj_a, proj_b, w_out, *cast_weights)


CONV_PAD_ROWS = SUBLANES


FFN_N_INPUTS = 9


def _ffn_kernel(*refs, n_cast, layer, final_norm):
    (x_ref, mod_ref, n2w_ref, wg_ref, wu_ref, cw_ref, cb_ref, wd_ref, fnw_ref) = refs[:FFN_N_INPUTS]
    o_ref = refs[FFN_N_INPUTS + n_cast]
    a_s, carry_s, act_s = refs[FFN_N_INPUTS + 2 * n_cast + 1:]
    _cast_slabs(refs[FFN_N_INPUTS:FFN_N_INPUTS + n_cast],
                refs[FFN_N_INPUTS + n_cast + 1:FFN_N_INPUTS + 2 * n_cast + 1])
    tm = FFN_TOKEN_TILE
    j = pl.program_id(1)

    @pl.when(j == 0)
    def _():
        carry_s[...] = jnp.zeros((CONV_PAD_ROWS, FFN_DIM), F32)

    norm_scale = n2w_ref[layer:layer + 1, :] * (1.0 + _mod_row(mod_ref, 4))
    shift2 = _mod_row(mod_ref, 3)
    hb_chunks, a_chunks, up_chunks = [], [], []
    for r in range(tm // FFN_NORM_ROW_CHUNK):
        xr = x_ref[0, r * FFN_NORM_ROW_CHUNK:(r + 1) * FFN_NORM_ROW_CHUNK, :]
        hr = (_rms_norm(xr, norm_scale) + shift2).astype(BF16)
        hb_chunks.append(hr)
        a_chunks.append(_dot(hr, wg_ref[:, 0:MXU_COLS]))
        up_chunks.append(_dot(hr, wu_ref[:, 0:MXU_COLS]))
    hb = jnp.concatenate(hb_chunks, axis=0)
    first_a = jnp.concatenate(a_chunks, axis=0)
    first_up = jnp.concatenate(up_chunks, axis=0)

    for c in range(FFN_DIM // MXU_COLS):
        cols = slice(c * MXU_COLS, (c + 1) * MXU_COLS)
        a = first_a if c == 0 else _dot(hb, wg_ref[:, cols])
        stage = a_s.at[c % 2]
        stage[0:CONV_PAD_ROWS, :] = carry_s[:, cols]
        stage[CONV_PAD_ROWS:CONV_PAD_ROWS + tm, :] = a
        conv = cb_ref[layer:layer + 1, cols] + cw_ref[2:3, cols] * a
        for tap in range(CONV_WIDTH - 1):
            back = CONV_WIDTH - 1 - tap
            conv = conv + cw_ref[tap:tap + 1, cols] * stage[CONV_PAD_ROWS - back:CONV_PAD_ROWS - back + tm, :]
        carry_s[:, cols] = stage[tm:tm + CONV_PAD_ROWS, :]
        up = first_up if c == 0 else _dot(hb, wu_ref[:, cols])
        act_s[:, cols] = (conv * jax.nn.sigmoid(conv) * up).astype(BF16)

    for r in range(tm // OUT_ROW_CHUNK):
        rows = slice(r * OUT_ROW_CHUNK, (r + 1) * OUT_ROW_CHUNK)
        y = x_ref[0, rows, :] + _mod_row(mod_ref, 5) * _dot(act_s[rows, :], wd_ref[...])
        if final_norm:
            y = _rms_norm(y, fnw_ref[...])
        o_ref[0, rows, :] = y


def _ffn_layer(x, mod, n2w, w_gate, w_up, conv_w, conv_b, w_down, final_w, final_norm,
               mod_layer, cast_weights, layer):
    b, s, d = x.shape
    tm = FFN_TOKEN_TILE
    tile = pl.BlockSpec((1, tm, d), lambda i, j: (i, j, 0))
    grid = (b, s // tm)
    cast_in, cast_out, cast_shapes = _cast_specs(cast_weights, layer, grid)
    return pl.pallas_call(
        functools.partial(_ffn_kernel, n_cast=len(cast_weights), layer=mod_layer,
                          final_norm=final_norm),
        grid=grid,
        in_specs=[
            tile,
            pl.BlockSpec((None,) + mod.shape[1:], lambda i, j: (mod_layer, 0, 0)),
            _full(n2w.shape),
            _full((d, FFN_DIM)), _full((d, FFN_DIM)),
            pl.BlockSpec((None,) + conv_w.shape[1:], lambda i, j: (mod_layer, 0, 0)),
            _full(conv_b.shape),
            _full((FFN_DIM, d)),
            _full((1, d)),
        ] + cast_in,
        out_specs=[tile] + cast_out,
        out_shape=[jax.ShapeDtypeStruct(x.shape, x.dtype)] + cast_shapes,
        scratch_shapes=[
            pltpu.VMEM((2, CONV_PAD_ROWS + tm, MXU_COLS), F32),
            pltpu.VMEM((CONV_PAD_ROWS, FFN_DIM), F32),
            pltpu.VMEM((tm, FFN_DIM), BF16),
        ],
        compiler_params=pltpu.CompilerParams(
            dimension_semantics=("arbitrary", "arbitrary"),
            vmem_limit_bytes=VMEM_LIMIT_BYTES),
        name="conv_ffn",
    )(x, mod, n2w, w_gate, w_up, conv_w, conv_b, w_down, final_w, *cast_weights)


def kernel(x, c, positions, ada_w, ada_b, norm1_w, w_in, attn_sinks, sgu_ln_w, sgu_ln_b, sgu_w, sgu_b,
           proj_a, proj_b, w_out, norm2_w, ffn_w_gate, ffn_w_up, ffn_conv_w, ffn_conv_b, ffn_w_down,
           final_norm_w):
    batch, seq, d = x.shape
    depth = ada_w.shape[0]
    assert seq % TOKEN_TILE == 0 and seq % FFN_TOKEN_TILE == 0 and TOKEN_TILE % ATTN_BLOCK == 0
    assert d == D_MODEL

    mix_f32 = (w_in, proj_a, proj_b, w_out)
    ffn_f32 = (ffn_w_gate, ffn_w_up, ffn_w_down)
    rows = -(-batch // SUBLANES) * SUBLANES
    c_pad = jnp.pad(c, ((0, rows - batch), (0, 0)))
    mod, cos_tab, sin_tab, *mix_w = _prepare(c_pad, ada_w, ada_b, positions, mix_f32)

    for l in range(depth):
        sgu_b_full = jnp.repeat(sgu_b[l].T, SGU_WIDTH // SGU_GROUPS, axis=1)
        x, *ffn_w = _mix_layer(
            x, mod, norm1_w, cos_tab, sin_tab, mix_w[0], attn_sinks, sgu_ln_w, sgu_ln_b,
            sgu_w, sgu_b_full, mix_w[1], mix_w[2], mix_w[3], l, ffn_f32, l)
        last = l == depth - 1
        x, *mix_w = _ffn_layer(
            x, mod, norm2_w, ffn_w[0], ffn_w[1], ffn_conv_w, ffn_conv_b, ffn_w[2],
            final_norm_w.reshape(1, d), last, l, () if last else mix_f32, l + 1)
    return x
```
